```python
import math
import jax, jax.numpy as jnp
from jax import lax
import numpy as np

D_MODEL = 1024
BATCH = 8
SEQ = 2048
DEPTH = 4

HEAD_DIM = 64
GM_GROUPS = 4
GM_CHUNK = 128
GM_WIDTH = GM_GROUPS * HEAD_DIM
DA_HEADS = 4
DA_QK_DIM = HEAD_DIM // 2
DA_V_DIM = HEAD_DIM
DA_QK_WIDTH = DA_HEADS * 2 * DA_QK_DIM
DA_WIDTH = DA_HEADS * DA_V_DIM
NSA_HEADS = 8
NSA_KV_GROUPS = 2
NSA_WIDTH = NSA_HEADS * HEAD_DIM
NSA_KV_WIDTH = NSA_KV_GROUPS * HEAD_DIM
CMP_LEN = 32
CMP_STRIDE = 16
CMP_HIDDEN = 256
SLC_LEN = 64
SLC_TOPK = 16
WIN = 512
FORCE_BONUS = 1000.0
Q_BLOCK = 128
SLC_Q_BLOCK = 64
D_FF = 2816
ROPE_THETA = 10000.0
EPS = 1e-6
N_MOD = 9
D_MIX = GM_WIDTH + DA_WIDTH + NSA_WIDTH
IN_WIDTHS = (GM_WIDTH, GM_WIDTH,
             DA_QK_WIDTH, DA_QK_WIDTH, DA_WIDTH,
             NSA_WIDTH, NSA_KV_WIDTH, NSA_KV_WIDTH, NSA_KV_WIDTH,
             NSA_KV_WIDTH, NSA_KV_WIDTH, NSA_KV_WIDTH, 3 * NSA_HEADS)
D_IN = 2 * GM_WIDTH + 2 * DA_QK_WIDTH + DA_WIDTH + NSA_WIDTH + 6 * NSA_KV_WIDTH + 3 * NSA_HEADS

kernel_name = "hybrid_gmlp_diffattn_nsa_macaron_adaln"


def _rmsnorm(x, g):
    xf = x.astype(jnp.float32)
    y = xf * lax.rsqrt(jnp.mean(xf * xf, axis=-1, keepdims=True) + EPS)
    return (y * g.astype(jnp.float32)).astype(x.dtype)


def _rope(x, pos):
    d = x.shape[-1]
    half = d // 2
    inv = ROPE_THETA ** (-jnp.arange(half, dtype=jnp.float32) / half)
    ang = pos.astype(jnp.float32)[:, None] * inv[None, :]
    cos, sin = jnp.cos(ang), jnp.sin(ang)
    xf = x.astype(jnp.float32)
    x1, x2 = xf[..., :half], xf[..., half:]
    return jnp.concatenate([x1 * cos - x2 * sin, x2 * cos + x1 * sin], axis=-1).astype(x.dtype)


def _masked_softmax(s, mask):
    p = jax.nn.softmax(jnp.where(mask, s, jnp.finfo(jnp.float32).min), axis=-1)
    return jnp.where(mask, p, 0.0)


def _swiglu(h, w_gate, w_up, w_down):
    return (jax.nn.silu(h @ w_gate) * (h @ w_up)) @ w_down


def _gmlp_sgu(u, v, ln_g, w_s, b_s):
    B_, S_, _ = u.shape
    u = jax.nn.gelu(u)
    vf = jax.nn.gelu(v).reshape(B_, S_, GM_GROUPS, HEAD_DIM).astype(jnp.float32)
    mu = jnp.mean(vf, axis=-1, keepdims=True)
    var = jnp.mean((vf - mu) ** 2, axis=-1, keepdims=True)
    vn = ((vf - mu) * lax.rsqrt(var + EPS)).astype(v.dtype) * ln_g.reshape(GM_GROUPS, HEAD_DIM)
    vc = vn.reshape(B_, S_ // GM_CHUNK, GM_CHUNK, GM_GROUPS, HEAD_DIM)
    causal = jnp.tril(jnp.ones((GM_CHUNK, GM_CHUNK), dtype=bool))
    w = jnp.where(causal[None], w_s, jnp.zeros_like(w_s))
    s = jnp.einsum('gts,bnsgd->bntgd', w, vc) + b_s.T[None, None, :, :, None]
    return u * s.reshape(B_, S_, GM_WIDTH)


def _diff_attention(q, k, v, lam, sub_g, lam_init):
    B_, S_, _ = q.shape
    pos = jnp.arange(S_)
    q = _rope(q.reshape(B_, S_, DA_HEADS, 2, DA_QK_DIM).transpose(0, 2, 3, 1, 4), pos)
    k = _rope(k.reshape(B_, S_, DA_HEADS, 2, DA_QK_DIM).transpose(0, 2, 3, 1, 4), pos)
    v = v.reshape(B_, S_, DA_HEADS, DA_V_DIM).transpose(0, 2, 1, 3)
    scale = DA_QK_DIM ** -0.5
    nb = S_ // Q_BLOCK
    qb = q.reshape(B_, DA_HEADS, 2, nb, Q_BLOCK, DA_QK_DIM).transpose(3, 0, 1, 2, 4, 5)

    def block(args):
        qi, i = args
        s = jnp.einsum('bhmqd,bhmkd->bhmqk', qi, k).astype(jnp.float32) * scale
        t = i * Q_BLOCK + jnp.arange(Q_BLOCK)
        p = _masked_softmax(s, t[:, None] >= pos[None, :])
        a = p[:, :, 0] - lam * p[:, :, 1]
        return jnp.einsum('bhqk,bhkd->bhqd', a.astype(v.dtype), v)

    o = lax.map(block, (qb, jnp.arange(nb)))
    o = o.transpose(1, 2, 0, 3, 4).reshape(B_, DA_HEADS, S_, DA_V_DIM)
    o = _rmsnorm(o, sub_g) * (1.0 - lam_init)
    return o.transpose(0, 2, 1, 3).reshape(B_, S_, DA_WIDTH)


def _compress(x, pe, w1, w2):
    S_ = x.shape[2]
    nc = (S_ - CMP_LEN) // CMP_STRIDE + 1
    idx = jnp.arange(nc)[:, None] * CMP_STRIDE + jnp.arange(CMP_LEN)[None, :]
    blocks = x[:, :, idx] + pe
    flat = blocks.reshape(blocks.shape[0], blocks.shape[1], nc, CMP_LEN * HEAD_DIM)
    return jax.nn.silu(flat @ w1) @ w2


def _cmp_to_slc(nc, ns):
    cs = np.arange(nc) * CMP_STRIDE
    ce = cs + CMP_LEN
    bs = np.arange(ns) * SLC_LEN
    be = bs + SLC_LEN
    ov = np.clip(np.minimum(ce[:, None], be[None, :]) - np.maximum(cs[:, None], bs[None, :]), 0, None)
    return (ov / CMP_STRIDE).astype(np.float32)


def _nsa(q, kc, vc, ks, vs, kw, vw, gates, pe, w1, w2):
    B_, S_, _ = q.shape
    G, Hg, Dh = NSA_KV_GROUPS, NSA_HEADS // NSA_KV_GROUPS, HEAD_DIM
    pos = jnp.arange(S_)
    scale = Dh ** -0.5
    q = _rope(q.reshape(B_, S_, G, Hg, Dh).transpose(0, 2, 3, 1, 4), pos)

    def kv(t):
        return t.reshape(B_, S_, G, Dh).transpose(0, 2, 1, 3)

    nc = (S_ - CMP_LEN) // CMP_STRIDE + 1
    cmp_end = jnp.arange(nc) * CMP_STRIDE + CMP_LEN - 1
    k_cmp = _rope(_compress(kv(kc), pe[0], w1[0], w2[0]), cmp_end)
    v_cmp = _compress(kv(vc), pe[1], w1[1], w2[1])
    s = jnp.einsum('bgjtd,bgcd->bgjtc', q, k_cmp).astype(jnp.float32) * scale
    p_cmp = _masked_softmax(s, cmp_end[None, :] <= pos[:, None])
    o_cmp = jnp.einsum('bgjtc,bgcd->bgjtd', p_cmp.astype(q.dtype), v_cmp)

    ns = S_ // SLC_LEN
    topk = min(SLC_TOPK, ns)
    imp = jnp.einsum('bgjtc,cn->bgtn', p_cmp, jnp.asarray(_cmp_to_slc(nc, ns)))
    blk = jnp.arange(ns)[None, :]
    cur = (pos // SLC_LEN)[:, None]
    valid = blk <= cur
    forced = valid & ((blk == 0) | (blk >= cur - 1))
    score = jnp.where(forced, FORCE_BONUS, jnp.where(valid, imp, -1.0))
    top_val, top_idx = lax.top_k(score, topk)
    top_ok = top_val >= 0.0
    k_slc = _rope(kv(ks), pos).reshape(B_, G, ns, SLC_LEN, Dh)
    v_slc = kv(vs).reshape(B_, G, ns, SLC_LEN, Dh)
    nqb = S_ // SLC_Q_BLOCK
    qs = q.reshape(B_, G, Hg, nqb, SLC_Q_BLOCK, Dh).transpose(3, 0, 1, 2, 4, 5)
    ib = top_idx.reshape(B_, G, nqb, SLC_Q_BLOCK, topk).transpose(2, 0, 1, 3, 4)
    okb = top_ok.reshape(B_, G, nqb, SLC_Q_BLOCK, topk).transpose(2, 0, 1, 3, 4)
    gather = jax.vmap(jax.vmap(lambda blocks, ix: blocks[ix]))

    def slc_block(args):
        qi, ix, ok, i = args
        kg = gather(k_slc, ix)
        vg = gather(v_slc, ix)
        t = i * SLC_Q_BLOCK + jnp.arange(SLC_Q_BLOCK)
        kpos = ix[..., None] * SLC_LEN + jnp.arange(SLC_LEN)
        mask = ok[..., None] & (kpos <= t[:, None, None])
        sc = jnp.einsum('bgjqd,bgqnld->bgjqnl', qi, kg).astype(jnp.float32) * scale
        sc = sc.reshape(B_, G, Hg, SLC_Q_BLOCK, topk * SLC_LEN)
        p = _masked_softmax(sc, mask.reshape(B_, G, 1, SLC_Q_BLOCK, topk * SLC_LEN))
        p = p.reshape(B_, G, Hg, SLC_Q_BLOCK, topk, SLC_LEN)
        return jnp.einsum('bgjqnl,bgqnld->bgjqd', p.astype(vg.dtype), vg)

    o_slc = lax.map(slc_block, (qs, ib, okb, jnp.arange(nqb)))
    o_slc = o_slc.transpose(1, 2, 3, 0, 4, 5).reshape(B_, G, Hg, S_, Dh)

    pad = ((0, 0), (0, 0), (WIN, 0), (0, 0))
    k_win = jnp.pad(_rope(kv(kw), pos), pad)
    v_win = jnp.pad(kv(vw), pad)
    nb = S_ // Q_BLOCK
    qw = q.reshape(B_, G, Hg, nb, Q_BLOCK, Dh).transpose(3, 0, 1, 2, 4, 5)

    def win_block(args):
        qi, i = args
        start = i * Q_BLOCK
        kb = lax.dynamic_slice_in_dim(k_win, start, WIN + Q_BLOCK, axis=2)
        vb = lax.dynamic_slice_in_dim(v_win, start, WIN + Q_BLOCK, axis=2)
        t = start + jnp.arange(Q_BLOCK)
        kpos = start - WIN + jnp.arange(WIN + Q_BLOCK)
        dist = t[:, None] - kpos[None, :]
        mask = (kpos[None, :] >= 0) & (dist >= 0) & (dist < WIN)
        sc = jnp.einsum('bgjqd,bgkd->bgjqk', qi, kb).astype(jnp.float32) * scale
        p = _masked_softmax(sc, mask)
        return jnp.einsum('bgjqk,bgkd->bgjqd', p.astype(vb.dtype), vb)

    o_win = lax.map(win_block, (qw, jnp.arange(nb)))
    o_win = o_win.transpose(1, 2, 3, 0, 4, 5).reshape(B_, G, Hg, S_, Dh)

    g = jax.nn.sigmoid(gates.astype(jnp.float32)).reshape(B_, S_, G, Hg, 3).transpose(0, 2, 3, 1, 4).astype(q.dtype)
    o = g[..., 0:1] * o_cmp + g[..., 1:2] * o_slc + g[..., 2:3] * o_win
    return o.transpose(0, 3, 1, 2, 4).reshape(B_, S_, NSA_WIDTH)


def _token_mix(h, w_in, w_out, gm_ln_g, gm_w_s, gm_b_s, da_lambda, da_sub_g, cmp_pe, cmp_w1, cmp_w2, lam_init):
    z = h @ w_in
    pts = np.cumsum(np.array(IN_WIDTHS))[:-1].tolist()
    (gu, gv, dq, dk, dv, nq, nkc, nvc, nks, nvs, nkw, nvw, ng) = jnp.split(z, pts, axis=-1)
    y_a = _gmlp_sgu(gu, gv, gm_ln_g, gm_w_s, gm_b_s)
    lf = da_lambda.astype(jnp.float32)
    lam = jnp.exp(jnp.sum(lf[0] * lf[1])) - jnp.exp(jnp.sum(lf[2] * lf[3])) + lam_init
    y_b = _diff_attention(dq, dk, dv, lam, da_sub_g, lam_init)
    y_c = _nsa(nq, nkc, nvc, nks, nvs, nkw, nvw, ng, cmp_pe, cmp_w1, cmp_w2)
    return jnp.concatenate([y_a, y_b, y_c], axis=-1) @ w_out


def setup_inputs(seed: int = 0) -> dict:
    key = jax.random.key(seed)
    ks = jax.random.split(key, 19)
    f32 = jnp.float32

    def nrm(k, shape, std):
        return std * jax.random.normal(k, shape, f32)

    L = DEPTH
    return {
        "x": nrm(ks[0], (BATCH, SEQ, D_MODEL), 1.0),
        "c": nrm(ks[1], (BATCH, D_MODEL), 1.0),
        "w_ada": nrm(ks[2], (L, D_MODEL, N_MOD * D_MODEL), 0.02),
        "b_ada": nrm(ks[3], (L, N_MOD * D_MODEL), 0.02),
        "norm_g": 1.0 + nrm(ks[4], (L, 3, D_MODEL), 0.05),
        "ffn_w_gate": nrm(ks[5], (L, 2, D_MODEL, D_FF), D_MODEL ** -0.5),
        "ffn_w_up": nrm(ks[6], (L, 2, D_MODEL, D_FF), D_MODEL ** -0.5),
        "ffn_w_down": nrm(ks[7], (L, 2, D_FF, D_MODEL), D_FF ** -0.5),
        "w_in": nrm(ks[8], (L, D_MODEL, D_IN), D_MODEL ** -0.5),
        "w_out": nrm(ks[9], (L, D_MIX, D_MODEL), D_MIX ** -0.5),
        "gm_ln_g": 1.0 + nrm(ks[10], (L, GM_WIDTH), 0.05),
        "gm_w_s": nrm(ks[11], (L, GM_GROUPS, GM_CHUNK, GM_CHUNK), GM_CHUNK ** -0.5),
        "gm_b_s": 1.0 + nrm(ks[12], (L, GM_GROUPS, GM_CHUNK), 0.1),
        "da_lambda": nrm(ks[13], (L, 4, DA_QK_DIM), 0.1),
        "da_sub_g": 1.0 + nrm(ks[14], (L, DA_V_DIM), 0.05),
        "nsa_cmp_pe": nrm(ks[15], (L, 2, CMP_LEN, HEAD_DIM), 0.02),
        "nsa_cmp_w1": nrm(ks[16], (L, 2, CMP_LEN * HEAD_DIM, CMP_HIDDEN), (CMP_LEN * HEAD_DIM) ** -0.5),
        "nsa_cmp_w2": nrm(ks[17], (L, 2, CMP_HIDDEN, HEAD_DIM), CMP_HIDDEN ** -0.5),
        "final_g": 1.0 + nrm(ks[18], (D_MODEL,), 0.05),
    }


def reference(x, c, w_ada, b_ada, norm_g, ffn_w_gate, ffn_w_up, ffn_w_down, w_in, w_out,
              gm_ln_g, gm_w_s, gm_b_s, da_lambda, da_sub_g, nsa_cmp_pe, nsa_cmp_w1, nsa_cmp_w2, final_g):
    c_act = jax.nn.silu(c)
    for l in range(DEPTH):
        mod = (c_act @ w_ada[l] + b_ada[l])[:, None, :]
        sh0, sc0, g0, sh1, sc1, g1, sh2, sc2, g2 = jnp.split(mod, N_MOD, axis=-1)
        lam_init = 0.8 - 0.6 * math.exp(-0.3 * l)
        h = _rmsnorm(x, norm_g[l, 0]) * (1.0 + sc0) + sh0
        x = x + 0.5 * g0 * _swiglu(h, ffn_w_gate[l, 0], ffn_w_up[l, 0], ffn_w_down[l, 0])
        h = _rmsnorm(x, norm_g[l, 1]) * (1.0 + sc1) + sh1
        x = x + g1 * _token_mix(h, w_in[l], w_out[l], gm_ln_g[l], gm_w_s[l], gm_b_s[l], da_lambda[l],
                                da_sub_g[l], nsa_cmp_pe[l], nsa_cmp_w1[l], nsa_cmp_w2[l], lam_init)
        h = _rmsnorm(x, norm_g[l, 2]) * (1.0 + sc2) + sh2
        x = x + 0.5 * g2 * _swiglu(h, ffn_w_gate[l, 1], ffn_w_up[l, 1], ffn_w_down[l, 1])
    return _rmsnorm(x, final_g)
```

```python
import functools
import math

import numpy as np
import jax
import jax.numpy as jnp
from jax import lax
from jax.experimental import pallas as pl
from jax.experimental.pallas import tpu as pltpu

F32 = jnp.float32
BF16 = jnp.bfloat16

HEAD_DIM = 64
GM_GROUPS = 4
GM_CHUNK = 128
GM_WIDTH = GM_GROUPS * HEAD_DIM
DA_HEADS = 4
DA_QK_DIM = HEAD_DIM // 2
DA_WIDTH = DA_HEADS * HEAD_DIM
NSA_HEADS = 8
NSA_KV_GROUPS = 2
NSA_HG = NSA_HEADS // NSA_KV_GROUPS
NSA_WIDTH = NSA_HEADS * HEAD_DIM
NSA_KV_WIDTH = NSA_KV_GROUPS * HEAD_DIM
N_GATES = 3 * NSA_HEADS
CMP_LEN = 32
CMP_STRIDE = 16
SLC_LEN = 64
SLC_TOPK = 16
WIN = 512
FORCE_BONUS = 1000.0
ROPE_THETA = 10000.0
EPS = 1e-6
N_MOD = 9
NEG = -1e30

LANE = 128
FF_CHUNK = 256
ATT_TILE = 256
VMEM_LIMIT = 56 * 1024 * 1024

OFF_GU, OFF_GV, OFF_DQ, OFF_DK, OFF_DV = 0, 256, 512, 768, 1024
OFF_NQ, OFF_KC, OFF_VC, OFF_KS, OFF_VS, OFF_KW, OFF_VW, OFF_NG = 1280, 1792, 1920, 2048, 2176, 2304, 2432, 2560
OFF_G0 = 2560
OFF_G1 = OFF_G0 + LANE
ROT_DQ = OFF_G1 + LANE
ROT_DK = ROT_DQ + 256
ROT_NQ = ROT_DK + 256
ROT_KS = ROT_NQ + 512
ROT_KW = ROT_KS + 128
W_CAT = ROT_KW + 128


def _params(sem):
    return pltpu.CompilerParams(dimension_semantics=sem, vmem_limit_bytes=VMEM_LIMIT)


def _resident(shape, index_map):
    return pl.BlockSpec(shape, index_map, pipeline_mode=pl.Buffered(1))


def _dot(a, b):
    return jnp.dot(a, b, preferred_element_type=F32)


def _dot_nt(a, b):
    return lax.dot_general(a, b, (((1,), (1,)), ((), ())), preferred_element_type=F32)


def _mod_norm(x, g, shift, scale):
    ms = jnp.mean(x * x, axis=-1, keepdims=True)
    return x * lax.rsqrt(ms + EPS) * g * (1.0 + scale) + shift


def _gelu(x):
    c = math.sqrt(2.0 / math.pi)
    return 0.5 * x * (1.0 + jnp.tanh(c * (x + 0.044715 * (x * x * x))))


def _silu(x):
    return x * jax.nn.sigmoid(x)


def _adaln_kernel(c_ref, w_ref, b_ref, o_ref):
    ca = _silu(c_ref[...])
    o_ref[...] = _dot(ca, w_ref[...]) + b_ref[...]


def _adaln(c, w_ada, b_ada):
    L, D, ND = w_ada.shape
    B = c.shape[0]
    tn = ND // 8
    return pl.pallas_call(
        _adaln_kernel,
        grid=(L, ND // tn),
        in_specs=[
            pl.BlockSpec((B, D), lambda l, j: (0, 0)),
            pl.BlockSpec((None, D, tn), lambda l, j: (l, 0, j)),
            pl.BlockSpec((None, 1, tn), lambda l, j: (l, 0, j)),
        ],
        out_specs=pl.BlockSpec((None, B, tn), lambda l, j: (l, 0, j)),
        out_shape=jax.ShapeDtypeStruct((L, B, ND), F32),
        compiler_params=_params(("arbitrary", "arbitrary")),
        name="adaln",
    )(c, w_ada, b_ada.reshape(L, 1, ND))


def _ffn_kernel(*refs, rows, nchunk, final):
    if final:
        x_ref, mod_ref, g_ref, wgu_ref, wd_ref, fg_ref, o_ref, h_ref, acc_ref = refs
    else:
        x_ref, mod_ref, g_ref, wgu_ref, wd_ref, o_ref, h_ref, acc_ref = refs
    r_sh, r_sc, r_gt = rows
    x = x_ref[...]
    h_ref[...] = _mod_norm(x, g_ref[...], mod_ref[r_sh:r_sh + 1, :], mod_ref[r_sc:r_sc + 1, :]).astype(BF16)
    acc_ref[...] = jnp.zeros_like(acc_ref)

    def body(j, carry):
        gu = _dot(h_ref[...], wgu_ref[j])
        gate = gu[:, :FF_CHUNK]
        up = gu[:, FF_CHUNK:]
        a = (_silu(gate) * up).astype(BF16)
        acc_ref[...] += _dot(a, wd_ref[j])
        return carry

    lax.fori_loop(0, nchunk, body, 0)
    y = x + 0.5 * mod_ref[r_gt:r_gt + 1, :] * acc_ref[...]
    if final:
        y = y * lax.rsqrt(jnp.mean(y * y, axis=-1, keepdims=True) + EPS) * fg_ref[...]
    o_ref[...] = y


def _ffn(x, mod, g, wgu, wd, rows, final_g=None, tm=512):
    B, S, D = x.shape
    nchunk = wgu.shape[0]
    final = final_g is not None
    in_specs = [
        pl.BlockSpec((None, tm, D), lambda b, i: (b, i, 0)),
        pl.BlockSpec((None, N_MOD, D), lambda b, i: (b, 0, 0)),
        pl.BlockSpec((1, D), lambda b, i: (0, 0)),
        _resident(wgu.shape, lambda b, i: (0, 0, 0)),
        _resident(wd.shape, lambda b, i: (0, 0, 0)),
    ]
    args = [x, mod, g.reshape(1, D), wgu, wd]
    if final:
        in_specs.append(pl.BlockSpec((1, D), lambda b, i: (0, 0)))
        args.append(final_g.reshape(1, D))
    return pl.pallas_call(
        functools.partial(_ffn_kernel, rows=rows, nchunk=nchunk, final=final),
        grid=(B, S // tm),
        in_specs=in_specs,
        out_specs=pl.BlockSpec((None, tm, D), lambda b, i: (b, i, 0)),
        out_shape=jax.ShapeDtypeStruct((B, S, D), F32),
        scratch_shapes=[pltpu.VMEM((tm, D), BF16), pltpu.VMEM((tm, D), F32)],
        compiler_params=_params(("arbitrary", "arbitrary")),
        name="ffn_final" if final else "ffn",
    )(*args)


def _inproj_kernel(x_ref, mod_ref, g_ref, w_ref, tab_ref,
                   gu_ref, gv_ref, dq_ref, dk_ref, dv_ref, nq_ref,
                   kc_ref, vc_ref, ks_ref, vs_ref, kw_ref, vw_ref, ng_ref, h_ref):
    h_ref[...] = _mod_norm(x_ref[...], g_ref[...], mod_ref[3:4, :], mod_ref[4:5, :]).astype(BF16)

    def mm(off, width):
        return _dot(h_ref[...], w_ref[:, off:off + width])

    def heads(ref, z):
        for hd in range(z.shape[1] // HEAD_DIM):
            ref[hd] = z[:, hd * HEAD_DIM:(hd + 1) * HEAD_DIM]

    def roped(off, rot_off, width, t_cos, t_sin):
        z = mm(off, width)
        zr = mm(rot_off, width)
        cos = tab_ref[t_cos]
        sin = tab_ref[t_sin]
        blocks = [z[:, c:c + LANE] * cos + zr[:, c:c + LANE] * sin for c in range(0, width, LANE)]
        return blocks[0] if len(blocks) == 1 else jnp.concatenate(blocks, axis=1)

    gu_ref[...] = mm(OFF_GU, GM_WIDTH)
    gv_ref[...] = mm(OFF_GV, GM_WIDTH)
    heads(dq_ref, roped(OFF_DQ, ROT_DQ, 256, 0, 1))
    heads(dk_ref, roped(OFF_DK, ROT_DK, 256, 2, 3))
    heads(dv_ref, mm(OFF_DV, DA_WIDTH))
    heads(nq_ref, roped(OFF_NQ, ROT_NQ, NSA_WIDTH, 4, 5))
    heads(kc_ref, mm(OFF_KC, NSA_KV_WIDTH))
    heads(vc_ref, mm(OFF_VC, NSA_KV_WIDTH))
    heads(ks_ref, roped(OFF_KS, ROT_KS, NSA_KV_WIDTH, 6, 7))
    heads(vs_ref, mm(OFF_VS, NSA_KV_WIDTH))
    heads(kw_ref, roped(OFF_KW, ROT_KW, NSA_KV_WIDTH, 6, 7))
    heads(vw_ref, mm(OFF_VW, NSA_KV_WIDTH))
    gates = jax.nn.sigmoid(mm(OFF_G0, 2 * LANE))
    ng_ref[0] = gates[:, :LANE]
    ng_ref[1] = gates[:, LANE:]


def _inproj(x, mod, g, wcat, tabs, tm=256):
    B, S, D = x.shape

    def hm(nh):
        return (jax.ShapeDtypeStruct((B, nh, S, HEAD_DIM), F32),
                pl.BlockSpec((None, nh, tm, HEAD_DIM), lambda b, i: (b, 0, i, 0)))

    def tok(w):
        return (jax.ShapeDtypeStruct((B, S, w), F32), pl.BlockSpec((None, tm, w), lambda b, i: (b, i, 0)))

    outs = [tok(GM_WIDTH), tok(GM_WIDTH), hm(DA_HEADS), hm(DA_HEADS), hm(DA_HEADS), hm(NSA_HEADS)]
    outs += [hm(NSA_KV_GROUPS)] * 6
    outs.append((jax.ShapeDtypeStruct((B, NSA_KV_GROUPS, S, LANE), F32),
                 pl.BlockSpec((None, NSA_KV_GROUPS, tm, LANE), lambda b, i: (b, 0, i, 0))))
    return pl.pallas_call(
        _inproj_kernel,
        grid=(B, S // tm),
        in_specs=[
            pl.BlockSpec((None, tm, D), lambda b, i: (b, i, 0)),
            pl.BlockSpec((None, N_MOD, D), lambda b, i: (b, 0, 0)),
            pl.BlockSpec((1, D), lambda b, i: (0, 0)),
            _resident(wcat.shape, lambda b, i: (0, 0)),
            pl.BlockSpec((tabs.shape[0], tm, LANE), lambda b, i: (0, i, 0)),
        ],
        out_specs=[o[1] for o in outs],
        out_shape=[o[0] for o in outs],
        scratch_shapes=[pltpu.VMEM((tm, D), BF16)],
        compiler_params=_params(("arbitrary", "arbitrary")),
        name="inproj",
    )(x, mod, g.reshape(1, D), wcat, tabs)


def _gmlp_kernel(gu_ref, gv_ref, lng_ref, ws_ref, bst_ref, o_ref):
    u = _gelu(gu_ref[...])
    v = _gelu(gv_ref[...])
    nch = u.shape[0] // GM_CHUNK
    r = lax.broadcasted_iota(jnp.int32, (GM_CHUNK, GM_CHUNK), 0)
    c = lax.broadcasted_iota(jnp.int32, (GM_CHUNK, GM_CHUNK), 1)
    tril = r >= c
    for g in range(GM_GROUPS):
        lo, hi = g * HEAD_DIM, (g + 1) * HEAD_DIM
        vg = v[:, lo:hi]
        mu = jnp.mean(vg, axis=-1, keepdims=True)
        d = vg - mu
        var = jnp.mean(d * d, axis=-1, keepdims=True)
        vn = (d * lax.rsqrt(var + EPS) * lng_ref[:, lo:hi]).astype(BF16)
        w = jnp.where(tril, ws_ref[g], 0.0).astype(BF16)
        bcol = bst_ref[:, g:g + 1]
        for ch in range(nch):
            t0, t1 = ch * GM_CHUNK, (ch + 1) * GM_CHUNK
            s = _dot(w, vn[t0:t1]) + bcol
            o_ref[t0:t1, lo:hi] = u[t0:t1, lo:hi] * s


def _gmlp(gu, gv, ln_g, w_s, b_s, tm=512):
    B, S, W = gu.shape
    tok = pl.BlockSpec((None, tm, W), lambda b, i: (b, i, 0))
    return pl.pallas_call(
        _gmlp_kernel,
        grid=(B, S // tm),
        in_specs=[tok, tok,
                  pl.BlockSpec((1, W), lambda b, i: (0, 0)),
                  pl.BlockSpec(w_s.shape, lambda b, i: (0, 0, 0)),
                  pl.BlockSpec((GM_CHUNK, GM_GROUPS), lambda b, i: (0, 0))],
        out_specs=tok,
        out_shape=jax.ShapeDtypeStruct((B, S, W), F32),
        compiler_params=_params(("arbitrary", "arbitrary")),
        name="gmlp",
    )(gu, gv, ln_g.reshape(1, W), w_s, b_s.T)


def _online_step(s, v, m_ref, l_ref, acc_ref):
    m_old = m_ref[...]
    m_new = jnp.maximum(m_old, jnp.max(s, axis=-1, keepdims=True))
    alpha = jnp.exp(m_old - m_new)
    p = jnp.exp(s - m_new)
    l_ref[...] = alpha * l_ref[...] + jnp.sum(p, axis=-1, keepdims=True)
    acc_ref[...] = alpha * acc_ref[...] + _dot(p.astype(BF16), v)
    m_ref[...] = m_new


def _online_init(m_ref, l_ref, acc_ref):
    m_ref[...] = jnp.full_like(m_ref, NEG)
    l_ref[...] = jnp.zeros_like(l_ref)
    acc_ref[...] = jnp.zeros_like(acc_ref)


def _da_kernel(q_ref, k_ref, v_ref, lam_ref, subg_ref, o_ref, m_ref, l_ref, acc_ref, *, lam_init):
    i = pl.program_id(1)
    T = q_ref.shape[1]
    lf = lam_ref[...]
    lam = (jnp.exp(jnp.sum(lf[0:1] * lf[1:2], axis=-1, keepdims=True))
           - jnp.exp(jnp.sum(lf[2:3] * lf[3:4], axis=-1, keepdims=True)) + lam_init)
    lane = lax.broadcasted_iota(jnp.int32, (1, HEAD_DIM), 1)
    first = lane < DA_QK_DIM
    rc = lax.broadcasted_iota(jnp.int32, (T, T), 0) - lax.broadcasted_iota(jnp.int32, (T, T), 1)
    for hd in range(DA_HEADS):
        q = q_ref[hd]
        qs = jnp.concatenate([jnp.where(first, q, 0.0), jnp.where(first, 0.0, q)], axis=0).astype(BF16)
        _online_init(m_ref, l_ref, acc_ref)

        def body(j, carry):
            k = k_ref[hd, pl.ds(pl.multiple_of(j * T, T), T), :].astype(BF16)
            v = v_ref[hd, pl.ds(pl.multiple_of(j * T, T), T), :].astype(BF16)
            s = _dot_nt(qs, k).reshape(2, T, T)
            mask = (rc + (i - j) * T) >= 0
            s = jnp.where(mask[None], s, NEG).reshape(2 * T, T)
            _online_step(s, v, m_ref, l_ref, acc_ref)
            return carry

        lax.fori_loop(0, i + 1, body, 0)
        o2 = acc_ref[...] / l_ref[...]
        o = o2[:T] - lam * o2[T:]
        o = o * lax.rsqrt(jnp.mean(o * o, axis=-1, keepdims=True) + EPS) * subg_ref[...] * (1.0 - lam_init)
        o_ref[:, hd * HEAD_DIM:(hd + 1) * HEAD_DIM] = o


def _diff_attn(dq, dk, dv, da_lambda, sub_g, lam_init, T=ATT_TILE):
    B, H, S, Dh = dq.shape
    return pl.pallas_call(
        functools.partial(_da_kernel, lam_init=lam_init),
        grid=(B, S // T),
        in_specs=[
            pl.BlockSpec((None, H, T, Dh), lambda b, i: (b, 0, i, 0)),
            pl.BlockSpec((None, H, S, Dh), lambda b, i: (b, 0, 0, 0)),
            pl.BlockSpec((None, H, S, Dh), lambda b, i: (b, 0, 0, 0)),
            pl.BlockSpec(da_lambda.shape, lambda b, i: (0, 0)),
            pl.BlockSpec((1, Dh), lambda b, i: (0, 0)),
        ],
        out_specs=pl.BlockSpec((None, T, H * Dh), lambda b, i: (b, i, 0)),
        out_shape=jax.ShapeDtypeStruct((B, S, H * Dh), F32),
        scratch_shapes=[pltpu.VMEM((2 * T, 1), F32), pltpu.VMEM((2 * T, 1), F32), pltpu.VMEM((2 * T, Dh), F32)],
        compiler_params=_params(("arbitrary", "arbitrary")),
        name="diff_attn",
    )(dq, dk, dv, da_lambda, sub_g.reshape(1, Dh))


def _compress_kernel(kc_ref, vc_ref, w1_ref, pe_ref, w2_ref, w2rot_ref, tab_ref, ko_ref, vo_ref):
    R = kc_ref.shape[0]
    half = w1_ref.shape[1] // 2

    def hidden(x_ref, t):
        xr = x_ref[...].astype(BF16)
        top = _dot(xr, w1_ref[t, :half, :])
        bot = _dot(xr, w1_ref[t, half:, :])
        pe_rows = jnp.broadcast_to(pe_ref[t], (8, 2 * half)).astype(BF16)
        pe = _dot(pe_rows, w1_ref[t])[0:1]
        return _silu(top + pltpu.roll(bot, R - 1, 0) + pe).astype(BF16)

    ak = hidden(kc_ref, 0)
    ko_ref[...] = _dot(ak, w2_ref[0]) * tab_ref[0] + _dot(ak, w2rot_ref[...]) * tab_ref[1]
    vo_ref[...] = _dot(hidden(vc_ref, 1), w2_ref[1])


def _compress(kc, vc, w1, pe, w2, w2rot, tab):
    B, G, S, Dh = kc.shape
    R = S // CMP_STRIDE
    W = CMP_STRIDE * Dh
    blk = pl.BlockSpec((None, None, R, W), lambda b, g: (b, g, 0, 0))
    out = pl.BlockSpec((None, None, R, Dh), lambda b, g: (b, g, 0, 0))
    full = lambda a: pl.BlockSpec(a.shape, lambda b, g: (0,) * a.ndim)
    return pl.pallas_call(
        _compress_kernel,
        grid=(B, G),
        in_specs=[blk, blk, full(w1), full(pe), full(w2), full(w2rot), full(tab)],
        out_specs=[out, out],
        out_shape=[jax.ShapeDtypeStruct((B, G, R, Dh), F32)] * 2,
        compiler_params=_params(("arbitrary", "arbitrary")),
        name="nsa_compress",
    )(kc.reshape(B, G, R, W), vc.reshape(B, G, R, W), w1, pe, w2, w2rot, tab)


def _cmp_kernel(q_ref, kc_ref, vc_ref, ov_ref, o_ref, sel_ref, *, ns, topk):
    i = pl.program_id(2)
    Hg, T, Dh = q_ref.shape
    R = kc_ref.shape[0]
    q = q_ref[...].reshape(Hg * T, Dh).astype(BF16)
    s = _dot_nt(q, kc_ref[...].astype(BF16)).reshape(Hg, T, R)
    pos = i * T + lax.broadcasted_iota(jnp.int32, (T, R), 0)
    cend = lax.broadcasted_iota(jnp.int32, (T, R), 1) * CMP_STRIDE + (CMP_LEN - 1)
    mask = (cend <= pos)[None]
    s = jnp.where(mask, s, NEG)
    e = jnp.where(mask, jnp.exp(s - jnp.max(s, axis=-1, keepdims=True)), 0.0)
    den = jnp.sum(e, axis=-1, keepdims=True)
    p = e / jnp.where(den > 0.0, den, 1.0)
    o_ref[...] = _dot(p.reshape(Hg * T, R).astype(BF16), vc_ref[...].astype(BF16)).reshape(Hg, T, Dh)

    psum = jnp.sum(p, axis=0)
    p_hi = psum.astype(BF16)
    p_lo = (psum - p_hi.astype(F32)).astype(BF16)
    ov = ov_ref[...]
    imp = _dot(p_hi, ov) + _dot(p_lo, ov)
    blk = lax.broadcasted_iota(jnp.int32, (T, LANE), 1)
    cur = (i * T + lax.broadcasted_iota(jnp.int32, (T, LANE), 0)) // SLC_LEN
    valid = blk <= cur
    forced = valid & ((blk == 0) | (blk >= cur - 1))
    score = jnp.where(forced, FORCE_BONUS, jnp.where(valid, imp, -1.0))
    score = jnp.where(blk < ns, score, -2.0)
    rank = jnp.zeros((T, LANE), F32)
    for j in range(ns):
        col = score[:, j:j + 1]
        ahead = (col > score) | ((col == score) & (blk > j))
        rank = rank + jnp.where(ahead, 1.0, 0.0)
    sel_ref[...] = jnp.where((rank < topk) & (score >= 0.0), 1.0, 0.0)


def _cmp_attn(nq, kcmp, vcmp, ov, ns, topk, T=ATT_TILE):
    B, H, S, Dh = nq.shape
    G = kcmp.shape[1]
    R = kcmp.shape[2]
    Hg = H // G
    qspec = pl.BlockSpec((None, Hg, T, Dh), lambda b, g, i: (b, g, i, 0))
    cspec = pl.BlockSpec((None, None, R, Dh), lambda b, g, i: (b, g, 0, 0))
    return pl.pallas_call(
        functools.partial(_cmp_kernel, ns=ns, topk=topk),
        grid=(B, G, S // T),
        in_specs=[qspec, cspec, cspec, pl.BlockSpec(ov.shape, lambda b, g, i: (0, 0))],
        out_specs=[qspec, pl.BlockSpec((None, None, T, LANE), lambda b, g, i: (b, g, i, 0))],
        out_shape=[jax.ShapeDtypeStruct((B, H, S, Dh), F32), jax.ShapeDtypeStruct((B, G, S, LANE), F32)],
        compiler_params=_params(("arbitrary", "arbitrary", "arbitrary")),
        name="nsa_cmp",
    )(nq, kcmp, vcmp, ov)


def _nsa_kernel(q_ref, ks_ref, vs_ref, kw_ref, vw_ref, oc_ref, sel_ref, ng_ref, ex_ref, o_ref,
                m_ref, l_ref, acc_ref):
    i = pl.program_id(2)
    Hg, T, Dh = q_ref.shape
    q = q_ref[...].reshape(Hg * T, Dh).astype(BF16)
    selb = sel_ref[...].astype(BF16)
    rc = lax.broadcasted_iota(jnp.int32, (T, T), 0) - lax.broadcasted_iota(jnp.int32, (T, T), 1)

    def attend(k_ref, v_ref, j, mask):
        k = k_ref[pl.ds(pl.multiple_of(j * T, T), T), :].astype(BF16)
        v = v_ref[pl.ds(pl.multiple_of(j * T, T), T), :].astype(BF16)
        s = _dot_nt(q, k).reshape(Hg, T, T)
        s = jnp.where(mask[None], s, NEG).reshape(Hg * T, T)
        _online_step(s, v, m_ref, l_ref, acc_ref)

    _online_init(m_ref, l_ref, acc_ref)

    def slc_body(j, carry):
        chosen = _dot(selb, ex_ref[j]) > 0.5
        attend(ks_ref, vs_ref, j, chosen & ((rc + (i - j) * T) >= 0))
        return carry

    lax.fori_loop(0, i + 1, slc_body, 0)
    o_slc = acc_ref[...] / l_ref[...]

    _online_init(m_ref, l_ref, acc_ref)

    def win_body(d, carry):
        dist = rc + d * T
        attend(kw_ref, vw_ref, i - d, (dist >= 0) & (dist < WIN))
        return carry

    lax.fori_loop(0, jnp.minimum(i, WIN // T) + 1, win_body, 0)
    o_win = acc_ref[...] / l_ref[...]

    gates = ng_ref[...]
    for hd in range(Hg):
        g_cmp = gates[:, 3 * hd:3 * hd + 1]
        g_slc = gates[:, 3 * hd + 1:3 * hd + 2]
        g_win = gates[:, 3 * hd + 2:3 * hd + 3]
        o = g_cmp * oc_ref[hd] + g_slc * o_slc[hd * T:(hd + 1) * T] + g_win * o_win[hd * T:(hd + 1) * T]
        o_ref[:, hd * Dh:(hd + 1) * Dh] = o


def _nsa_main(nq, ks, vs, kw, vw, o_cmp, sel, ng, expand, T=ATT_TILE):
    B, H, S, Dh = nq.shape
    G = ks.shape[1]
    Hg = H // G
    qspec = pl.BlockSpec((None, Hg, T, Dh), lambda b, g, i: (b, g, i, 0))
    kvspec = pl.BlockSpec((None, None, S, Dh), lambda b, g, i: (b, g, 0, 0))
    tspec = pl.BlockSpec((None, None, T, LANE), lambda b, g, i: (b, g, i, 0))
    return pl.pallas_call(
        _nsa_kernel,
        grid=(B, G, S // T),
        in_specs=[qspec, kvspec, kvspec, kvspec, kvspec, qspec, tspec, tspec,
                  pl.BlockSpec(expand.shape, lambda b, g, i: (0, 0, 0))],
        out_specs=pl.BlockSpec((None, T, Hg * Dh), lambda b, g, i: (b, i, g)),
        out_shape=jax.ShapeDtypeStruct((B, S, H * Dh), F32),
        scratch_shapes=[pltpu.VMEM((Hg * T, 1), F32), pltpu.VMEM((Hg * T, 1), F32), pltpu.VMEM((Hg * T, Dh), F32)],
        compiler_params=_params(("arbitrary", "arbitrary", "arbitrary")),
        name="nsa_main",
    )(nq, ks, vs, kw, vw, o_cmp, sel, ng, expand)


def _outproj_kernel(x_ref, mod_ref, ya_ref, yb_ref, yc_ref, w_ref, o_ref):
    y = _dot(ya_ref[...].astype(BF16), w_ref[0:GM_WIDTH, :])
    y += _dot(yb_ref[...].astype(BF16), w_ref[GM_WIDTH:GM_WIDTH + DA_WIDTH, :])
    y += _dot(yc_ref[...].astype(BF16), w_ref[GM_WIDTH + DA_WIDTH:, :])
    o_ref[...] = x_ref[...] + mod_ref[5:6, :] * y


def _outproj(x, mod, ya, yb, yc, w_out, tm=512):
    B, S, D = x.shape
    tok = lambda w: pl.BlockSpec((None, tm, w), lambda b, i: (b, i, 0))
    return pl.pallas_call(
        _outproj_kernel,
        grid=(B, S // tm),
        in_specs=[tok(D), pl.BlockSpec((None, N_MOD, D), lambda b, i: (b, 0, 0)),
                  tok(GM_WIDTH), tok(DA_WIDTH), tok(NSA_WIDTH),
                  _resident(w_out.shape, lambda b, i: (0, 0))],
        out_specs=tok(D),
        out_shape=jax.ShapeDtypeStruct((B, S, D), F32),
        compiler_params=_params(("arbitrary", "arbitrary")),
        name="outproj",
    )(x, mod, ya, yb, yc, w_out)


def _rope_table(pos, period, width, scale=1.0):
    half = period // 2
    lane = np.arange(width)
    inv = ROPE_THETA ** (-((lane % period) % half).astype(np.float32) / half)
    ang = pos.astype(jnp.float32)[:, None] * jnp.asarray(inv, F32)[None, :]
    return jnp.cos(ang) * scale, jnp.sin(ang) * scale


def _rope_tables(S):
    pos = jnp.arange(S)
    tabs = []
    tabs += _rope_table(pos, DA_QK_DIM, LANE, DA_QK_DIM ** -0.5)
    tabs += _rope_table(pos, DA_QK_DIM, LANE)
    tabs += _rope_table(pos, HEAD_DIM, LANE, HEAD_DIM ** -0.5)
    tabs += _rope_table(pos, HEAD_DIM, LANE)
    cmp_end = jnp.arange(S // CMP_STRIDE) * CMP_STRIDE + CMP_LEN - 1
    return jnp.stack(tabs), jnp.stack(_rope_table(cmp_end, HEAD_DIM, HEAD_DIM))


def _rot_cols(w, period):
    d, n = w.shape
    half = period // 2
    c = w.reshape(d, n // period, period)
    return jnp.concatenate([-c[..., half:], c[..., :half]], axis=-1).reshape(d, n)


def _layout_w_in(w):
    d = w.shape[0]
    pad = jnp.zeros((d, LANE - N_GATES // 2), w.dtype)
    parts = [
        w[:, :OFF_NG],
        w[:, OFF_NG:OFF_NG + N_GATES // 2], pad,
        w[:, OFF_NG + N_GATES // 2:OFF_NG + N_GATES], pad,
        _rot_cols(w[:, OFF_DQ:OFF_DQ + 256], DA_QK_DIM),
        _rot_cols(w[:, OFF_DK:OFF_DK + 256], DA_QK_DIM),
        _rot_cols(w[:, OFF_NQ:OFF_NQ + NSA_WIDTH], HEAD_DIM),
        _rot_cols(w[:, OFF_KS:OFF_KS + NSA_KV_WIDTH], HEAD_DIM),
        _rot_cols(w[:, OFF_KW:OFF_KW + NSA_KV_WIDTH], HEAD_DIM),
    ]
    return jnp.concatenate(parts, axis=1).astype(BF16)


def _layout_ffn(w_gate, w_up, w_down):
    d, f = w_gate.shape
    n = f // FF_CHUNK
    wg = w_gate.reshape(d, n, FF_CHUNK).transpose(1, 0, 2)
    wu = w_up.reshape(d, n, FF_CHUNK).transpose(1, 0, 2)
    return jnp.concatenate([wg, wu], axis=-1).astype(BF16), w_down.reshape(n, FF_CHUNK, d).astype(BF16)


def _overlap_table(S):
    nc = (S - CMP_LEN) // CMP_STRIDE + 1
    ns = S // SLC_LEN
    cs = np.arange(nc) * CMP_STRIDE
    bs = np.arange(ns) * SLC_LEN
    ov = np.clip(np.minimum(cs[:, None] + CMP_LEN, bs[None, :] + SLC_LEN) - np.maximum(cs[:, None], bs[None, :]), 0, None)
    out = np.zeros((S // CMP_STRIDE, LANE), np.float32)
    out[:nc, :ns] = ov / CMP_STRIDE
    return jnp.asarray(out, BF16)


def _expand_table(S, T):
    key_block = np.arange(S).reshape(S // T, 1, T) // SLC_LEN
    return jnp.asarray(key_block == np.arange(LANE).reshape(1, LANE, 1), BF16)


def kernel(x, c, w_ada, b_ada, norm_g, ffn_w_gate, ffn_w_up, ffn_w_down, w_in, w_out,
           gm_ln_g, gm_w_s, gm_b_s, da_lambda, da_sub_g, nsa_cmp_pe, nsa_cmp_w1, nsa_cmp_w2, final_g):
    B, S, D = x.shape
    depth = w_in.shape[0]
    ns = S // SLC_LEN
    topk = min(SLC_TOPK, ns)
    assert S % max(ATT_TILE, 512) == 0 and ns <= LANE and ffn_w_gate.shape[-1] % FF_CHUNK == 0

    mod_all = _adaln(c, w_ada, b_ada).reshape(depth, B, N_MOD, D)
    tabs, cmp_tab = _rope_tables(S)
    ov = _overlap_table(S)
    expand = _expand_table(S, ATT_TILE)

    for l in range(depth):
        mod = mod_all[l]
        lam_init = 0.8 - 0.6 * math.exp(-0.3 * l)
        wgu, wd = _layout_ffn(ffn_w_gate[l, 0], ffn_w_up[l, 0], ffn_w_down[l, 0])
        x = _ffn(x, mod, norm_g[l, 0], wgu, wd, rows=(0, 1, 2))

        (gu, gv, dq, dk, dv, nq, kc, vc, ks, vs, kw, vw, ng) = _inproj(
            x, mod, norm_g[l, 1], _layout_w_in(w_in[l]), tabs)
        y_a = _gmlp(gu, gv, gm_ln_g[l], gm_w_s[l], gm_b_s[l])
        y_b = _diff_attn(dq, dk, dv, da_lambda[l], da_sub_g[l], lam_init)
        kcmp, vcmp = _compress(
            kc, vc, nsa_cmp_w1[l].astype(BF16), nsa_cmp_pe[l].reshape(2, 1, CMP_LEN * HEAD_DIM),
            nsa_cmp_w2[l].astype(BF16), _rot_cols(nsa_cmp_w2[l, 0], HEAD_DIM).astype(BF16), cmp_tab)
        o_cmp, sel = _cmp_attn(nq, kcmp, vcmp, ov, ns, topk)
        y_c = _nsa_main(nq, ks, vs, kw, vw, o_cmp, sel, ng, expand)
        x = _outproj(x, mod, y_a, y_b, y_c, w_out[l].astype(BF16))

        wgu, wd = _layout_ffn(ffn_w_gate[l, 1], ffn_w_up[l, 1], ffn_w_down[l, 1])
        x = _ffn(x, mod, norm_g[l, 2], wgu, wd, rows=(6, 7, 8), final_g=final_g if l == depth - 1 else None)
    return x
```

```python
import functools
import math

import numpy as np
import jax
import jax.numpy as jnp
from jax import lax
from jax.experimental import pallas as pl
from jax.experimental.pallas import tpu as pltpu

F32 = jnp.float32
BF16 = jnp.bfloat16

HEAD_DIM = 64
GM_GROUPS = 4
GM_CHUNK = 128
GM_WIDTH = GM_GROUPS * HEAD_DIM
DA_HEADS = 4
DA_QK_DIM = HEAD_DIM // 2
DA_WIDTH = DA_HEADS * HEAD_DIM
NSA_HEADS = 8
NSA_KV_GROUPS = 2
NSA_HG = NSA_HEADS // NSA_KV_GROUPS
NSA_WIDTH = NSA_HEADS * HEAD_DIM
NSA_KV_WIDTH = NSA_KV_GROUPS * HEAD_DIM
N_GATES = 3 * NSA_HEADS
GATE_ROWS = 16
CMP_LEN = 32
CMP_STRIDE = 16
SLC_LEN = 64
SLC_TOPK = 16
WIN = 512
FORCE_BONUS = 1000.0
ROPE_THETA = 10000.0
EPS = 1e-6
N_MOD = 9
NEG = -1e30
LOG2E = math.log2(math.e)

LANE = 128
SUBLANE = 8
FF_CHUNK = 256
ATT_TILE = 256
CHUNK = 256
VMEM_LIMIT = 56 * 1024 * 1024

OFF_GU, OFF_GV, OFF_DQ, OFF_DK, OFF_DV = 0, 256, 512, 768, 1024
OFF_NQ, OFF_KC, OFF_VC, OFF_KS, OFF_VS, OFF_KW, OFF_VW, OFF_NG = 1280, 1792, 1920, 2048, 2176, 2304, 2432, 2560
OFF_G0 = 2560
OFF_G1 = OFF_G0 + LANE
ROT_DQ = OFF_G1 + LANE
ROT_DK = ROT_DQ + 256
ROT_NQ = ROT_DK + 256
ROT_KS = ROT_NQ + 512
ROT_KW = ROT_KS + 128
W_CAT = ROT_KW + 128


def _params(sem):
    return pltpu.CompilerParams(dimension_semantics=sem, vmem_limit_bytes=VMEM_LIMIT)


def _resident(shape, index_map):
    return pl.BlockSpec(shape, index_map, pipeline_mode=pl.Buffered(1))


def _dot(a, b):
    return jnp.dot(a, b, preferred_element_type=F32)


def _dot_nt(a, b):
    return lax.dot_general(a, b, (((1,), (1,)), ((), ())), preferred_element_type=F32)


def _mod_norm(x, g, shift, scale):
    ms = jnp.mean(x * x, axis=-1, keepdims=True)
    return x * lax.rsqrt(ms + EPS) * g * (1.0 + scale) + shift


def _gelu(x):
    c = math.sqrt(2.0 / math.pi)
    return 0.5 * x * (1.0 + jnp.tanh(c * (x + 0.044715 * (x * x * x))))


def _silu(x):
    return x * jax.nn.sigmoid(x)


def _adaln_kernel(c_ref, w_ref, b_ref, o_ref):
    ca = _silu(c_ref[...])
    o_ref[...] = _dot(ca, w_ref[...]) + b_ref[...]


def _adaln(c, w_ada, b_ada):
    L, D, ND = w_ada.shape
    B = c.shape[0]
    tn = ND // 8
    return pl.pallas_call(
        _adaln_kernel,
        grid=(L, ND // tn),
        in_specs=[
            pl.BlockSpec((B, D), lambda l, j: (0, 0)),
            pl.BlockSpec((None, D, tn), lambda l, j: (l, 0, j)),
            pl.BlockSpec((None, 1, tn), lambda l, j: (l, 0, j)),
        ],
        out_specs=pl.BlockSpec((None, B, tn), lambda l, j: (l, 0, j)),
        out_shape=jax.ShapeDtypeStruct((L, B, ND), F32),
        compiler_params=_params(("arbitrary", "arbitrary")),
        name="adaln",
    )(c, w_ada, b_ada.reshape(L, 1, ND))


def _ffn_kernel(*refs, rows, nchunk, final):
    if final:
        x_ref, mod_ref, g_ref, wgu_ref, wd_ref, fg_ref, o_ref, h_ref, acc_ref = refs
    else:
        x_ref, mod_ref, g_ref, wgu_ref, wd_ref, o_ref, h_ref, acc_ref = refs
    r_sh, r_sc, r_gt = rows
    x = x_ref[...]
    h_ref[...] = _mod_norm(x, g_ref[...], mod_ref[r_sh:r_sh + 1, :], mod_ref[r_sc:r_sc + 1, :]).astype(BF16)
    acc_ref[...] = jnp.zeros_like(acc_ref)

    def body(j, carry):
        gu = _dot(h_ref[...], wgu_ref[j])
        gate = gu[:, :FF_CHUNK]
        up = gu[:, FF_CHUNK:]
        a = (_silu(gate) * up).astype(BF16)
        acc_ref[...] += _dot(a, wd_ref[j])
        return carry

    lax.fori_loop(0, nchunk, body, 0)
    y = x + 0.5 * mod_ref[r_gt:r_gt + 1, :] * acc_ref[...]
    if final:
        y = y * lax.rsqrt(jnp.mean(y * y, axis=-1, keepdims=True) + EPS) * fg_ref[...]
    o_ref[...] = y


def _ffn(x, mod, g, wgu, wd, rows, final_g=None, tm=512):
    B, S, D = x.shape
    nchunk = wgu.shape[0]
    final = final_g is not None
    in_specs = [
        pl.BlockSpec((None, tm, D), lambda b, i: (b, i, 0)),
        pl.BlockSpec((None, N_MOD, D), lambda b, i: (b, 0, 0)),
        pl.BlockSpec((1, D), lambda b, i: (0, 0)),
        _resident(wgu.shape, lambda b, i: (0, 0, 0)),
        _resident(wd.shape, lambda b, i: (0, 0, 0)),
    ]
    args = [x, mod, g.reshape(1, D), wgu, wd]
    if final:
        in_specs.append(pl.BlockSpec((1, D), lambda b, i: (0, 0)))
        args.append(final_g.reshape(1, D))
    return pl.pallas_call(
        functools.partial(_ffn_kernel, rows=rows, nchunk=nchunk, final=final),
        grid=(B, S // tm),
        in_specs=in_specs,
        out_specs=pl.BlockSpec((None, tm, D), lambda b, i: (b, i, 0)),
        out_shape=jax.ShapeDtypeStruct((B, S, D), F32),
        scratch_shapes=[pltpu.VMEM((tm, D), BF16), pltpu.VMEM((tm, D), F32)],
        compiler_params=_params(("arbitrary", "arbitrary")),
        name="ffn_final" if final else "ffn",
    )(*args)


def _inproj_kernel(x_ref, mod_ref, g_ref, w_ref, tab_ref,
                   gu_ref, gv_ref, dq_ref, dk_ref, dvt_ref, nq_ref,
                   kc_ref, vc_ref, ks_ref, vst_ref, kw_ref, vwt_ref, ngt_ref, h_ref):
    h_ref[...] = _mod_norm(x_ref[...], g_ref[...], mod_ref[3:4, :], mod_ref[4:5, :]).astype(BF16)

    def mm(off, width):
        return _dot(h_ref[...], w_ref[:, off:off + width])

    def heads(ref, z):
        for hd in range(z.shape[1] // HEAD_DIM):
            ref[hd] = z[:, hd * HEAD_DIM:(hd + 1) * HEAD_DIM].astype(ref.dtype)

    def heads_t(ref, z):
        for c in range(z.shape[1] // LANE):
            zt = z[:, c * LANE:(c + 1) * LANE].T
            ref[2 * c] = zt[:HEAD_DIM].astype(ref.dtype)
            ref[2 * c + 1] = zt[HEAD_DIM:].astype(ref.dtype)

    def roped(off, rot_off, width, t_cos, t_sin):
        z = mm(off, width)
        zr = mm(rot_off, width)
        cos = tab_ref[t_cos]
        sin = tab_ref[t_sin]
        blocks = [z[:, c:c + LANE] * cos + zr[:, c:c + LANE] * sin for c in range(0, width, LANE)]
        return blocks[0] if len(blocks) == 1 else jnp.concatenate(blocks, axis=1)

    gu_ref[...] = mm(OFF_GU, GM_WIDTH)
    gv_ref[...] = mm(OFF_GV, GM_WIDTH)
    heads(dq_ref, roped(OFF_DQ, ROT_DQ, 256, 0, 1))
    heads(dk_ref, roped(OFF_DK, ROT_DK, 256, 2, 3))
    heads_t(dvt_ref, mm(OFF_DV, DA_WIDTH))
    heads(nq_ref, roped(OFF_NQ, ROT_NQ, NSA_WIDTH, 4, 5))
    heads(kc_ref, mm(OFF_KC, NSA_KV_WIDTH))
    heads(vc_ref, mm(OFF_VC, NSA_KV_WIDTH))
    heads(ks_ref, roped(OFF_KS, ROT_KS, NSA_KV_WIDTH, 6, 7))
    heads_t(vst_ref, mm(OFF_VS, NSA_KV_WIDTH))
    heads(kw_ref, roped(OFF_KW, ROT_KW, NSA_KV_WIDTH, 6, 7))
    heads_t(vwt_ref, mm(OFF_VW, NSA_KV_WIDTH))
    gates = jax.nn.sigmoid(mm(OFF_G0, 2 * LANE))
    ngt_ref[0] = gates[:, :LANE].T[:GATE_ROWS]
    ngt_ref[1] = gates[:, LANE:].T[:GATE_ROWS]


def _inproj(x, mod, g, wcat, tabs):
    B, S, D = x.shape
    tm = ATT_TILE
    nt = S // tm

    def hm(nh):
        return (jax.ShapeDtypeStruct((B, nh, S, HEAD_DIM), BF16),
                pl.BlockSpec((None, nh, tm, HEAD_DIM), lambda b, i: (b, 0, i, 0)))

    def hm_t(nh):
        return (jax.ShapeDtypeStruct((B, nh, nt, HEAD_DIM, tm), BF16),
                pl.BlockSpec((None, nh, None, HEAD_DIM, tm), lambda b, i: (b, 0, i, 0, 0)))

    def tok(w):
        return (jax.ShapeDtypeStruct((B, S, w), F32), pl.BlockSpec((None, tm, w), lambda b, i: (b, i, 0)))

    G = NSA_KV_GROUPS
    outs = [tok(GM_WIDTH), tok(GM_WIDTH), hm(DA_HEADS), hm(DA_HEADS), hm_t(DA_HEADS), hm(NSA_HEADS),
            hm(G), hm(G), hm(G), hm_t(G), hm(G), hm_t(G),
            (jax.ShapeDtypeStruct((B, G, GATE_ROWS, S), F32),
             pl.BlockSpec((None, G, GATE_ROWS, tm), lambda b, i: (b, 0, 0, i)))]
    return pl.pallas_call(
        _inproj_kernel,
        grid=(B, nt),
        in_specs=[
            pl.BlockSpec((None, tm, D), lambda b, i: (b, i, 0)),
            pl.BlockSpec((None, N_MOD, D), lambda b, i: (b, 0, 0)),
            pl.BlockSpec((1, D), lambda b, i: (0, 0)),
            _resident(wcat.shape, lambda b, i: (0, 0)),
            pl.BlockSpec((tabs.shape[0], tm, LANE), lambda b, i: (0, i, 0)),
        ],
        out_specs=[o[1] for o in outs],
        out_shape=[o[0] for o in outs],
        scratch_shapes=[pltpu.VMEM((tm, D), BF16)],
        compiler_params=_params(("arbitrary", "arbitrary")),
        name="inproj",
    )(x, mod, g.reshape(1, D), wcat, tabs)


def _gmlp_kernel(gu_ref, gv_ref, lng_ref, ws_ref, bst_ref, o_ref):
    u = _gelu(gu_ref[...])
    v = _gelu(gv_ref[...])
    nch = u.shape[0] // GM_CHUNK
    r = lax.broadcasted_iota(jnp.int32, (GM_CHUNK, GM_CHUNK), 0)
    c = lax.broadcasted_iota(jnp.int32, (GM_CHUNK, GM_CHUNK), 1)
    tril = r >= c
    for g in range(GM_GROUPS):
        lo, hi = g * HEAD_DIM, (g + 1) * HEAD_DIM
        vg = v[:, lo:hi]
        mu = jnp.mean(vg, axis=-1, keepdims=True)
        d = vg - mu
        var = jnp.mean(d * d, axis=-1, keepdims=True)
        vn = (d * lax.rsqrt(var + EPS) * lng_ref[:, lo:hi]).astype(BF16)
        w = jnp.where(tril, ws_ref[g], 0.0).astype(BF16)
        bcol = bst_ref[:, g:g + 1]
        for ch in range(nch):
            t0, t1 = ch * GM_CHUNK, (ch + 1) * GM_CHUNK
            s = _dot(w, vn[t0:t1]) + bcol
            o_ref[t0:t1, lo:hi] = u[t0:t1, lo:hi] * s


def _gmlp(gu, gv, ln_g, w_s, b_s, tm=512):
    B, S, W = gu.shape
    tok = pl.BlockSpec((None, tm, W), lambda b, i: (b, i, 0))
    return pl.pallas_call(
        _gmlp_kernel,
        grid=(B, S // tm),
        in_specs=[tok, tok,
                  pl.BlockSpec((1, W), lambda b, i: (0, 0)),
                  pl.BlockSpec(w_s.shape, lambda b, i: (0, 0, 0)),
                  pl.BlockSpec((GM_CHUNK, GM_GROUPS), lambda b, i: (0, 0))],
        out_specs=tok,
        out_shape=jax.ShapeDtypeStruct((B, S, W), F32),
        compiler_params=_params(("arbitrary", "arbitrary")),
        name="gmlp",
    )(gu, gv, ln_g.reshape(1, W), w_s, b_s.T)


def _tile_steps(q_ref, ks, vts, biases, m_ref, l_ref, acc_ref):
    n = len(ks)

    def scores(c):
        s = _dot_nt(ks[c], q_ref[c * CHUNK:(c + 1) * CHUNK, :])
        return s if biases[c] is None else s + biases[c]

    ahead = 3
    s = [scores(c) if c < ahead else None for c in range(n)]
    for c in range(n):
        m_old = m_ref[c]
        m_new = jnp.maximum(m_old, jnp.max(s[c], axis=0, keepdims=True))
        alpha = jnp.exp2(m_old - m_new)
        p = jnp.exp2(s[c] - m_new)
        l_ref[c] = alpha * l_ref[c] + jnp.sum(p, axis=0, keepdims=True)
        m_ref[c] = m_new
        if c + ahead < n:
            s[c + ahead] = scores(c + ahead)
        acc_ref[c] = alpha * acc_ref[c] + _dot(vts[c], p.astype(BF16))


def _normalized(c0, n, l_ref, acc_ref):
    parts = [acc_ref[c] * (1.0 / l_ref[c]) for c in range(c0, c0 + n)]
    return parts[0] if n == 1 else jnp.concatenate(parts, axis=1)


def _online_init(m_ref, l_ref, acc_ref):
    m_ref[...] = jnp.full_like(m_ref, NEG)
    l_ref[...] = jnp.zeros_like(l_ref)
    acc_ref[...] = jnp.zeros_like(acc_ref)


def _causal_bias(T, upper):
    key = lax.broadcasted_iota(jnp.int32, (T, T), 0)
    qry = lax.broadcasted_iota(jnp.int32, (T, T), 1)
    keep = (key > qry) if upper else (key <= qry)
    return jnp.where(keep, 0.0, NEG)


def _da_kernel(q_ref, k_ref, vt_ref, lam_ref, subg_ref, o_ref, q_sc, bias_sc, m_ref, l_ref, acc_ref, *, lam_init):
    i = pl.program_id(2)
    HP, T, Dh = q_ref.shape
    cpt = T // CHUNK
    lf = lam_ref[...]
    lam = (jnp.exp(jnp.sum(lf[0:1] * lf[1:2], axis=-1, keepdims=True))
           - jnp.exp(jnp.sum(lf[2:3] * lf[3:4], axis=-1, keepdims=True)) + lam_init)
    first = lax.broadcasted_iota(jnp.int32, (1, Dh), 1) < DA_QK_DIM
    zero = jnp.zeros((), BF16)
    for hd in range(HP):
        q = q_ref[hd]
        q_sc[(2 * hd) * T:(2 * hd + 1) * T, :] = jnp.where(first, q, zero)
        q_sc[(2 * hd + 1) * T:(2 * hd + 2) * T, :] = jnp.where(first, zero, q)
    bias_sc[...] = _causal_bias(T, upper=False)
    _online_init(m_ref, l_ref, acc_ref)

    def tile(j, masked):
        start = pl.multiple_of(j * T, T)
        ks, vts, biases = [], [], []
        for hd in range(HP):
            k = k_ref[hd, pl.ds(start, T), :]
            vt = vt_ref[hd, j]
            for cc in range(2 * cpt):
                ks.append(k)
                vts.append(vt)
                biases.append(bias_sc[:, (cc % cpt) * CHUNK:(cc % cpt + 1) * CHUNK] if masked else None)
        _tile_steps(q_sc, ks, vts, biases, m_ref, l_ref, acc_ref)

    def body(j, carry):
        tile(j, False)
        return carry

    lax.fori_loop(0, i, body, 0)
    tile(i, True)

    outs = []
    for hd in range(HP):
        o1 = _normalized(2 * hd * cpt, cpt, l_ref, acc_ref)
        o2 = _normalized((2 * hd + 1) * cpt, cpt, l_ref, acc_ref)
        o = o1 - lam * o2
        o = o * lax.rsqrt(jnp.mean(o * o, axis=0, keepdims=True) + EPS) * subg_ref[...] * (1.0 - lam_init)
        outs.append(o)
    o_ref[...] = jnp.concatenate(outs, axis=0).T


def _diff_attn(dq, dk, dvt, da_lambda, sub_g, lam_init):
    B, H, S, Dh = dq.shape
    T = ATT_TILE
    HP = 2
    rows = HP * 2 * T
    return pl.pallas_call(
        functools.partial(_da_kernel, lam_init=lam_init),
        grid=(B, H // HP, S // T),
        in_specs=[
            pl.BlockSpec((None, HP, T, Dh), lambda b, h, i: (b, h, i, 0)),
            pl.BlockSpec((None, HP, S, Dh), lambda b, h, i: (b, h, 0, 0)),
            pl.BlockSpec((None, HP, S // T, Dh, T), lambda b, h, i: (b, h, 0, 0, 0)),
            pl.BlockSpec(da_lambda.shape, lambda b, h, i: (0, 0)),
            pl.BlockSpec((Dh, T), lambda b, h, i: (0, 0)),
        ],
        out_specs=pl.BlockSpec((None, T, HP * Dh), lambda b, h, i: (b, i, h)),
        out_shape=jax.ShapeDtypeStruct((B, S, H * Dh), F32),
        scratch_shapes=[pltpu.VMEM((rows, Dh), BF16), pltpu.VMEM((T, T), F32),
                        pltpu.VMEM((rows // CHUNK, 1, CHUNK), F32), pltpu.VMEM((rows // CHUNK, 1, CHUNK), F32),
                        pltpu.VMEM((rows // CHUNK, Dh, CHUNK), F32)],
        compiler_params=_params(("arbitrary", "arbitrary", "arbitrary")),
        name="diff_attn",
    )(dq, dk, dvt, da_lambda, jnp.broadcast_to(sub_g[:, None], (Dh, T)))


def _compress_kernel(kc_ref, vc_ref, w1_ref, pe_ref, w2_ref, w2rot_ref, tab_ref, ko_ref, vo_ref):
    R = kc_ref.shape[0]
    half = w1_ref.shape[1] // 2

    def hidden(x_ref, t):
        xr = x_ref[...]
        top = _dot(xr, w1_ref[t, :half, :])
        bot = _dot(xr, w1_ref[t, half:, :])
        pe_rows = jnp.broadcast_to(pe_ref[t], (SUBLANE, 2 * half)).astype(BF16)
        pe = _dot(pe_rows, w1_ref[t])[0:1]
        return _silu(top + pltpu.roll(bot, R - 1, 0) + pe).astype(BF16)

    ak = hidden(kc_ref, 0)
    ko = _dot(ak, w2_ref[0]) * tab_ref[0] + _dot(ak, w2rot_ref[...]) * tab_ref[1]
    vo = _dot(hidden(vc_ref, 1), w2_ref[1])
    ko_ref[...] = ko.astype(BF16)
    vo_ref[...] = jnp.concatenate([ko, vo], axis=1).T[HEAD_DIM:].astype(BF16)


def _compress(kc, vc, w1, pe, w2, w2rot, tab):
    B, G, S, Dh = kc.shape
    R = S // CMP_STRIDE
    W = CMP_STRIDE * Dh
    blk = pl.BlockSpec((None, None, R, W), lambda b, g: (b, g, 0, 0))
    full = lambda a: pl.BlockSpec(a.shape, lambda b, g: (0,) * a.ndim)
    return pl.pallas_call(
        _compress_kernel,
        grid=(B, G),
        in_specs=[blk, blk, full(w1), full(pe), full(w2), full(w2rot), full(tab)],
        out_specs=[pl.BlockSpec((None, None, R, Dh), lambda b, g: (b, g, 0, 0)),
                   pl.BlockSpec((None, None, Dh, R), lambda b, g: (b, g, 0, 0))],
        out_shape=[jax.ShapeDtypeStruct((B, G, R, Dh), BF16), jax.ShapeDtypeStruct((B, G, Dh, R), BF16)],
        compiler_params=_params(("arbitrary", "arbitrary")),
        name="nsa_compress",
    )(kc.reshape(B, G, R, W), vc.reshape(B, G, R, W), w1, pe, w2, w2rot, tab)


def _cmp_kernel(q_ref, kc_ref, vct_ref, ovt_ref, o_ref, sel_ref, *, ns, topk):
    i = pl.program_id(2)
    Hg, T, Dh = q_ref.shape
    R = kc_ref.shape[0]
    NB = sel_ref.shape[0]
    cpt = T // LANE
    kc = kc_ref[...]
    vct = vct_ref[...]
    pos = i * T + lax.broadcasted_iota(jnp.int32, (R, T), 1)
    cend = lax.broadcasted_iota(jnp.int32, (R, T), 0) * CMP_STRIDE + (CMP_LEN - 1)
    mask = cend <= pos
    psum = [jnp.zeros((R, LANE), F32) for _ in range(cpt)]
    for hd in range(Hg):
        for cc in range(cpt):
            sl = slice(cc * LANE, (cc + 1) * LANE)
            mk = mask[:, sl]
            s = jnp.where(mk, _dot_nt(kc, q_ref[hd, sl, :]), NEG)
            e = jnp.where(mk, jnp.exp2(s - jnp.max(s, axis=0, keepdims=True)), 0.0)
            den = jnp.sum(e, axis=0, keepdims=True)
            p = e * (1.0 / jnp.where(den > 0.0, den, 1.0))
            o_ref[hd, :, sl] = _dot(vct, p.astype(BF16))
            psum[cc] = psum[cc] + p

    ps = psum[0] if cpt == 1 else jnp.concatenate(psum, axis=1)
    p_hi = ps.astype(BF16)
    p_lo = (ps - p_hi.astype(F32)).astype(BF16)
    ovt = ovt_ref[...]
    imp = _dot(ovt, p_hi) + _dot(ovt, p_lo)
    blk = lax.broadcasted_iota(jnp.int32, (NB, T), 0)
    cur = (i * T + lax.broadcasted_iota(jnp.int32, (NB, T), 1)) // SLC_LEN
    valid = blk <= cur
    forced = valid & ((blk == 0) | (blk >= cur - 1))
    score = jnp.where(forced, FORCE_BONUS, jnp.where(valid, imp, -1.0))
    score = jnp.where(blk < ns, score, -2.0)
    rank = jnp.zeros((NB, T), F32)
    for j in range(ns):
        row = score[j:j + 1, :]
        ahead = (row > score) | ((row == score) & (blk > j))
        rank = rank + jnp.where(ahead, 1.0, 0.0)
    sel_ref[...] = jnp.where((rank < topk) & (score >= 0.0), 1.0, 0.0).astype(sel_ref.dtype)


def _cmp_attn(nq, kcmp, vcmpt, ovt, ns, topk):
    B, H, S, Dh = nq.shape
    G = kcmp.shape[1]
    R = kcmp.shape[2]
    NB = ovt.shape[0]
    Hg = H // G
    T = ATT_TILE
    return pl.pallas_call(
        functools.partial(_cmp_kernel, ns=ns, topk=topk),
        grid=(B, G, S // T),
        in_specs=[pl.BlockSpec((None, Hg, T, Dh), lambda b, g, i: (b, g, i, 0)),
                  pl.BlockSpec((None, None, R, Dh), lambda b, g, i: (b, g, 0, 0)),
                  pl.BlockSpec((None, None, Dh, R), lambda b, g, i: (b, g, 0, 0)),
                  pl.BlockSpec(ovt.shape, lambda b, g, i: (0, 0))],
        out_specs=[pl.BlockSpec((None, Hg, Dh, T), lambda b, g, i: (b, g, 0, i)),
                   pl.BlockSpec((None, None, NB, T), lambda b, g, i: (b, g, 0, i))],
        out_shape=[jax.ShapeDtypeStruct((B, H, Dh, S), F32), jax.ShapeDtypeStruct((B, G, NB, S), BF16)],
        compiler_params=_params(("arbitrary", "arbitrary", "arbitrary")),
        name="nsa_cmp",
    )(nq, kcmp, vcmpt, ovt)


def _nsa_kernel(q_ref, ks_ref, vst_ref, kw_ref, vwt_ref, oct_ref, sel_ref, ngt_ref, ext_ref, o_ref,
                q_sc, bias_sc, m_ref, l_ref, acc_ref, oslc_sc):
    i = pl.program_id(2)
    Hg, T, Dh = q_ref.shape
    cpt = T // CHUNK
    q_sc[...] = q_ref[...].reshape(Hg * T, Dh)

    def tile(k_ref, vt_ref, j, masked):
        k = k_ref[pl.ds(pl.multiple_of(j * T, T), T), :]
        vt = vt_ref[j]
        n = Hg * cpt
        biases = [bias_sc[:, (c % cpt) * CHUNK:(c % cpt + 1) * CHUNK] if masked else None for c in range(n)]
        _tile_steps(q_sc, [k] * n, [vt] * n, biases, m_ref, l_ref, acc_ref)

    _online_init(m_ref, l_ref, acc_ref)
    sel = sel_ref[...]

    def slc_body(j, carry):
        bias_sc[...] = jnp.where(_dot(ext_ref[j], sel) > 0.5, 0.0, NEG)
        tile(ks_ref, vst_ref, j, True)
        return carry

    lax.fori_loop(0, i, slc_body, 0)
    bias_sc[...] = jnp.where(_dot(ext_ref[i], sel) > 0.5, _causal_bias(T, upper=False), NEG)
    tile(ks_ref, vst_ref, i, True)
    for hd in range(Hg):
        oslc_sc[hd] = _normalized(hd * cpt, cpt, l_ref, acc_ref)

    _online_init(m_ref, l_ref, acc_ref)
    bias_sc[...] = _causal_bias(T, upper=False)
    tile(kw_ref, vwt_ref, i, True)
    back = WIN // T
    for d in range(1, back + 1):
        @pl.when(i >= d)
        def _():
            if d == back:
                bias_sc[...] = _causal_bias(T, upper=True)
            tile(kw_ref, vwt_ref, i - d, d == back)

    gates = ngt_ref[...]
    outs = []
    for hd in range(Hg):
        o = (gates[3 * hd:3 * hd + 1, :] * oct_ref[hd]
             + gates[3 * hd + 1:3 * hd + 2, :] * oslc_sc[hd]
             + gates[3 * hd + 2:3 * hd + 3, :] * _normalized(hd * cpt, cpt, l_ref, acc_ref))
        outs.append(o)
    for pr in range(Hg // 2):
        o_ref[:, pr * LANE:(pr + 1) * LANE] = jnp.concatenate(outs[2 * pr:2 * pr + 2], axis=0).T


def _nsa_main(nq, ks, vst, kw, vwt, o_cmpt, sel, ngt, expand_t):
    B, H, S, Dh = nq.shape
    G = ks.shape[1]
    Hg = H // G
    T = ATT_TILE
    assert WIN % T == 0 and Hg % 2 == 0
    rows = Hg * T
    kspec = pl.BlockSpec((None, None, S, Dh), lambda b, g, i: (b, g, 0, 0))
    vspec = pl.BlockSpec((None, None, S // T, Dh, T), lambda b, g, i: (b, g, 0, 0, 0))
    NB = sel.shape[2]
    return pl.pallas_call(
        _nsa_kernel,
        grid=(B, G, S // T),
        in_specs=[pl.BlockSpec((None, Hg, T, Dh), lambda b, g, i: (b, g, i, 0)),
                  kspec, vspec, kspec, vspec,
                  pl.BlockSpec((None, Hg, Dh, T), lambda b, g, i: (b, g, 0, i)),
                  pl.BlockSpec((None, None, NB, T), lambda b, g, i: (b, g, 0, i)),
                  pl.BlockSpec((None, None, GATE_ROWS, T), lambda b, g, i: (b, g, 0, i)),
                  pl.BlockSpec(expand_t.shape, lambda b, g, i: (0, 0, 0))],
        out_specs=pl.BlockSpec((None, T, Hg * Dh), lambda b, g, i: (b, i, g)),
        out_shape=jax.ShapeDtypeStruct((B, S, H * Dh), F32),
        scratch_shapes=[pltpu.VMEM((rows, Dh), BF16), pltpu.VMEM((T, T), F32),
                        pltpu.VMEM((rows // CHUNK, 1, CHUNK), F32), pltpu.VMEM((rows // CHUNK, 1, CHUNK), F32),
                        pltpu.VMEM((rows // CHUNK, Dh, CHUNK), F32), pltpu.VMEM((Hg, Dh, T), F32)],
        compiler_params=_params(("arbitrary", "arbitrary", "arbitrary")),
        name="nsa_main",
    )(nq, ks, vst, kw, vwt, o_cmpt, sel, ngt, expand_t)


def _outproj_kernel(x_ref, mod_ref, ya_ref, yb_ref, yc_ref, w_ref, o_ref):
    y = _dot(ya_ref[...].astype(BF16), w_ref[0:GM_WIDTH, :])
    y += _dot(yb_ref[...].astype(BF16), w_ref[GM_WIDTH:GM_WIDTH + DA_WIDTH, :])
    y += _dot(yc_ref[...].astype(BF16), w_ref[GM_WIDTH + DA_WIDTH:, :])
    o_ref[...] = x_ref[...] + mod_ref[5:6, :] * y


def _outproj(x, mod, ya, yb, yc, w_out, tm=512):
    B, S, D = x.shape
    tok = lambda w: pl.BlockSpec((None, tm, w), lambda b, i: (b, i, 0))
    return pl.pallas_call(
        _outproj_kernel,
        grid=(B, S // tm),
        in_specs=[tok(D), pl.BlockSpec((None, N_MOD, D), lambda b, i: (b, 0, 0)),
                  tok(GM_WIDTH), tok(DA_WIDTH), tok(NSA_WIDTH),
                  _resident(w_out.shape, lambda b, i: (0, 0))],
        out_specs=tok(D),
        out_shape=jax.ShapeDtypeStruct((B, S, D), F32),
        compiler_params=_params(("arbitrary", "arbitrary")),
        name="outproj",
    )(x, mod, ya, yb, yc, w_out)


def _rope_table(pos, period, width, scale=1.0):
    half = period // 2
    lane = np.arange(width)
    inv = ROPE_THETA ** (-((lane % period) % half).astype(np.float32) / half)
    ang = pos.astype(jnp.float32)[:, None] * jnp.asarray(inv, F32)[None, :]
    return jnp.cos(ang) * scale, jnp.sin(ang) * scale


def _rope_tables(S):
    pos = jnp.arange(S)
    tabs = []
    tabs += _rope_table(pos, DA_QK_DIM, LANE, DA_QK_DIM ** -0.5 * LOG2E)
    tabs += _rope_table(pos, DA_QK_DIM, LANE)
    tabs += _rope_table(pos, HEAD_DIM, LANE, HEAD_DIM ** -0.5 * LOG2E)
    tabs += _rope_table(pos, HEAD_DIM, LANE)
    cmp_end = jnp.arange(S // CMP_STRIDE) * CMP_STRIDE + CMP_LEN - 1
    return jnp.stack(tabs), jnp.stack(_rope_table(cmp_end, HEAD_DIM, HEAD_DIM))


def _rot_cols(w, period):
    d, n = w.shape
    half = period // 2
    c = w.reshape(d, n // period, period)
    return jnp.concatenate([-c[..., half:], c[..., :half]], axis=-1).reshape(d, n)


def _layout_w_in(w):
    d = w.shape[0]
    pad = jnp.zeros((d, LANE - N_GATES // 2), w.dtype)
    parts = [
        w[:, :OFF_NG],
        w[:, OFF_NG:OFF_NG + N_GATES // 2], pad,
        w[:, OFF_NG + N_GATES // 2:OFF_NG + N_GATES], pad,
        _rot_cols(w[:, OFF_DQ:OFF_DQ + 256], DA_QK_DIM),
        _rot_cols(w[:, OFF_DK:OFF_DK + 256], DA_QK_DIM),
        _rot_cols(w[:, OFF_NQ:OFF_NQ + NSA_WIDTH], HEAD_DIM),
        _rot_cols(w[:, OFF_KS:OFF_KS + NSA_KV_WIDTH], HEAD_DIM),
        _rot_cols(w[:, OFF_KW:OFF_KW + NSA_KV_WIDTH], HEAD_DIM),
    ]
    return jnp.concatenate(parts, axis=1).astype(BF16)


def _layout_ffn(w_gate, w_up, w_down):
    d, f = w_gate.shape
    n = f // FF_CHUNK
    wg = w_gate.reshape(d, n, FF_CHUNK).transpose(1, 0, 2)
    wu = w_up.reshape(d, n, FF_CHUNK).transpose(1, 0, 2)
    return jnp.concatenate([wg, wu], axis=-1).astype(BF16), w_down.reshape(n, FF_CHUNK, d).astype(BF16)


def _overlap_table_t(S, nb):
    nc = (S - CMP_LEN) // CMP_STRIDE + 1
    ns = S // SLC_LEN
    cs = np.arange(nc) * CMP_STRIDE
    bs = np.arange(ns) * SLC_LEN
    ov = np.clip(np.minimum(cs[:, None] + CMP_LEN, bs[None, :] + SLC_LEN) - np.maximum(cs[:, None], bs[None, :]), 0, None)
    out = np.zeros((nb, S // CMP_STRIDE), np.float32)
    out[:ns, :nc] = (ov / CMP_STRIDE).T
    return jnp.asarray(out, BF16)


def _expand_table_t(S, T, nb):
    key_block = np.arange(S).reshape(S // T, T, 1) // SLC_LEN
    return jnp.asarray(key_block == np.arange(nb).reshape(1, 1, nb), BF16)


def kernel(x, c, w_ada, b_ada, norm_g, ffn_w_gate, ffn_w_up, ffn_w_down, w_in, w_out,
           gm_ln_g, gm_w_s, gm_b_s, da_lambda, da_sub_g, nsa_cmp_pe, nsa_cmp_w1, nsa_cmp_w2, final_g):
    B, S, D = x.shape
    depth = w_in.shape[0]
    ns = S // SLC_LEN
    topk = min(SLC_TOPK, ns)
    nb = -(-ns // 16) * 16
    assert S % 512 == 0 and ATT_TILE % CHUNK == 0 and CHUNK % LANE == 0 and ffn_w_gate.shape[-1] % FF_CHUNK == 0

    mod_all = _adaln(c, w_ada, b_ada).reshape(depth, B, N_MOD, D)
    tabs, cmp_tab = _rope_tables(S)
    ovt = _overlap_table_t(S, nb)
    expand_t = _expand_table_t(S, ATT_TILE, nb)

    for l in range(depth):
        mod = mod_all[l]
        lam_init = 0.8 - 0.6 * math.exp(-0.3 * l)
        wgu, wd = _layout_ffn(ffn_w_gate[l, 0], ffn_w_up[l, 0], ffn_w_down[l, 0])
        x = _ffn(x, mod, norm_g[l, 0], wgu, wd, rows=(0, 1, 2))

        (gu, gv, dq, dk, dvt, nq, kc, vc, ks, vst, kw, vwt, ngt) = _inproj(
            x, mod, norm_g[l, 1], _layout_w_in(w_in[l]), tabs)
        y_a = _gmlp(gu, gv, gm_ln_g[l], gm_w_s[l], gm_b_s[l])
        y_b = _diff_attn(dq, dk, dvt, da_lambda[l], da_sub_g[l], lam_init)
        kcmp, vcmpt = _compress(
            kc, vc, nsa_cmp_w1[l].astype(BF16), nsa_cmp_pe[l].reshape(2, 1, CMP_LEN * HEAD_DIM),
            nsa_cmp_w2[l].astype(BF16), _rot_cols(nsa_cmp_w2[l, 0], HEAD_DIM).astype(BF16), cmp_tab)
        o_cmpt, sel = _cmp_attn(nq, kcmp, vcmpt, ovt, ns, topk)
        y_c = _nsa_main(nq, ks, vst, kw, vwt, o_cmpt, sel, ngt, expand_t)
        x = _outproj(x, mod, y_a, y_b, y_c, w_out[l].astype(BF16))

        wgu, wd = _layout_ffn(ffn_w_gate[l, 1], ffn_w_up[l, 1], ffn_w_down[l, 1])
        x = _ffn(x, mod, norm_g[l, 2], wgu, wd, rows=(6, 7, 8), final_g=final_g if l == depth - 1 else None)
    return x
```

```python
import functools
import math

import numpy as np
import jax
import jax.numpy as jnp
from jax import lax
from jax.experimental import pallas as pl
from jax.experimental.pallas import tpu as pltpu

F32 = jnp.float32
BF16 = jnp.bfloat16

HEAD_DIM = 64
GM_GROUPS = 4
GM_CHUNK = 128
GM_WIDTH = GM_GROUPS * HEAD_DIM
DA_HEADS = 4
DA_QK_DIM = HEAD_DIM // 2
DA_WIDTH = DA_HEADS * HEAD_DIM
NSA_HEADS = 8
NSA_KV_GROUPS = 2
NSA_HG = NSA_HEADS // NSA_KV_GROUPS
NSA_WIDTH = NSA_HEADS * HEAD_DIM
NSA_KV_WIDTH = NSA_KV_GROUPS * HEAD_DIM
N_GATES = 3 * NSA_HEADS
GATE_ROWS = 16
CMP_LEN = 32
CMP_STRIDE = 16
SLC_LEN = 64
SLC_TOPK = 16
WIN = 512
FORCE_BONUS = 1000.0
ROPE_THETA = 10000.0
EPS = 1e-6
N_MOD = 9
NEG = -1e30
LOG2E = math.log2(math.e)

LANE = 128
SUBLANE = 8
FF_CHUNK = 256
ATT_TILE = 256
CHUNK = 256
VMEM_LIMIT = 56 * 1024 * 1024

OFF_GU, OFF_GV, OFF_DQ, OFF_DK, OFF_DV = 0, 256, 512, 768, 1024
OFF_NQ, OFF_KC, OFF_VC, OFF_KS, OFF_VS, OFF_KW, OFF_VW, OFF_NG = 1280, 1792, 1920, 2048, 2176, 2304, 2432, 2560
OFF_G0 = 2560
OFF_G1 = OFF_G0 + LANE
ROT_DQ = OFF_G1 + LANE
ROT_DK = ROT_DQ + 256
ROT_NQ = ROT_DK + 256
ROT_KS = ROT_NQ + 512
ROT_KW = ROT_KS + 128
W_CAT = ROT_KW + 128


def _params(sem):
    return pltpu.CompilerParams(dimension_semantics=sem, vmem_limit_bytes=VMEM_LIMIT)


def _resident(shape, index_map):
    return pl.BlockSpec(shape, index_map, pipeline_mode=pl.Buffered(1))


def _dot(a, b):
    return jnp.dot(a, b, preferred_element_type=F32)


def _dot_nt(a, b):
    return lax.dot_general(a, b, (((1,), (1,)), ((), ())), preferred_element_type=F32)


def _mod_norm(x, g, shift, scale):
    ms = jnp.mean(x * x, axis=-1, keepdims=True)
    return x * lax.rsqrt(ms + EPS) * g * (1.0 + scale) + shift


def _gelu(x):
    c = math.sqrt(2.0 / math.pi)
    return 0.5 * x * (1.0 + jnp.tanh(c * (x + 0.044715 * (x * x * x))))


def _silu(x):
    return x * jax.nn.sigmoid(x)


def _adaln_kernel(c_ref, w_ref, b_ref, o_ref):
    ca = _silu(c_ref[...])
    o_ref[...] = _dot(ca, w_ref[...]) + b_ref[...]


def _adaln(c, w_ada, b_ada):
    L, D, ND = w_ada.shape
    B = c.shape[0]
    tn = ND // 8
    return pl.pallas_call(
        _adaln_kernel,
        grid=(L, ND // tn),
        in_specs=[
            pl.BlockSpec((B, D), lambda l, j: (0, 0)),
            pl.BlockSpec((None, D, tn), lambda l, j: (l, 0, j)),
            pl.BlockSpec((None, 1, tn), lambda l, j: (l, 0, j)),
        ],
        out_specs=pl.BlockSpec((None, B, tn), lambda l, j: (l, 0, j)),
        out_shape=jax.ShapeDtypeStruct((L, B, ND), F32),
        compiler_params=_params(("arbitrary", "arbitrary")),
        name="adaln",
    )(c, w_ada, b_ada.reshape(L, 1, ND))


def _ffn_kernel(*refs, rows, nchunk, final):
    if final:
        x_ref, mod_ref, g_ref, wgu_ref, wd_ref, fg_ref, o_ref, h_ref, a_ref = refs
    else:
        x_ref, mod_ref, g_ref, wgu_ref, wd_ref, o_ref, h_ref, a_ref = refs
    r_sh, r_sc, r_gt = rows
    x = x_ref[...]
    h_ref[...] = _mod_norm(x, g_ref[...], mod_ref[r_sh:r_sh + 1, :], mod_ref[r_sc:r_sc + 1, :]).astype(BF16)
    for j in range(nchunk):
        gu = _dot(h_ref[...], wgu_ref[j])
        gate = gu[:, :FF_CHUNK]
        up = gu[:, FF_CHUNK:]
        a_ref[:, j * FF_CHUNK:(j + 1) * FF_CHUNK] = (_silu(gate) * up).astype(BF16)
    y = x + 0.5 * mod_ref[r_gt:r_gt + 1, :] * _dot(a_ref[...], wd_ref[...])
    if final:
        y = y * lax.rsqrt(jnp.mean(y * y, axis=-1, keepdims=True) + EPS) * fg_ref[...]
    o_ref[...] = y


def _ffn(x, mod, g, wgu, wd, rows, final_g=None, tm=512):
    B, S, D = x.shape
    nchunk = wgu.shape[0]
    final = final_g is not None
    in_specs = [
        pl.BlockSpec((None, tm, D), lambda b, i: (b, i, 0)),
        pl.BlockSpec((None, N_MOD, D), lambda b, i: (b, 0, 0)),
        pl.BlockSpec((1, D), lambda b, i: (0, 0)),
        _resident(wgu.shape, lambda b, i: (0, 0, 0)),
        _resident(wd.shape, lambda b, i: (0, 0)),
    ]
    args = [x, mod, g.reshape(1, D), wgu, wd]
    if final:
        in_specs.append(pl.BlockSpec((1, D), lambda b, i: (0, 0)))
        args.append(final_g.reshape(1, D))
    return pl.pallas_call(
        functools.partial(_ffn_kernel, rows=rows, nchunk=nchunk, final=final),
        grid=(B, S // tm),
        in_specs=in_specs,
        out_specs=pl.BlockSpec((None, tm, D), lambda b, i: (b, i, 0)),
        out_shape=jax.ShapeDtypeStruct((B, S, D), F32),
        scratch_shapes=[pltpu.VMEM((tm, D), BF16), pltpu.VMEM((tm, wd.shape[0]), BF16)],
        compiler_params=_params(("arbitrary", "arbitrary")),
        name="ffn_final" if final else "ffn",
    )(*args)


def _inproj_kernel(x_ref, mod_ref, g_ref, w_ref, tab_ref,
                   gu_ref, gv_ref, dq_ref, dk_ref, dvt_ref, nq_ref,
                   kc_ref, vc_ref, ks_ref, vst_ref, kw_ref, vwt_ref, ngt_ref, h_ref):
    h_ref[...] = _mod_norm(x_ref[...], g_ref[...], mod_ref[3:4, :], mod_ref[4:5, :]).astype(BF16)

    def mm(off, width):
        return _dot(h_ref[...], w_ref[:, off:off + width])

    def heads(ref, z):
        for hd in range(z.shape[1] // HEAD_DIM):
            ref[hd] = z[:, hd * HEAD_DIM:(hd + 1) * HEAD_DIM].astype(ref.dtype)

    def heads_t(ref, z):
        for c in range(z.shape[1] // LANE):
            zt = z[:, c * LANE:(c + 1) * LANE].T
            ref[2 * c] = zt[:HEAD_DIM].astype(ref.dtype)
            ref[2 * c + 1] = zt[HEAD_DIM:].astype(ref.dtype)

    def roped(off, rot_off, width, t_cos, t_sin):
        z = mm(off, width)
        zr = mm(rot_off, width)
        cos = tab_ref[t_cos]
        sin = tab_ref[t_sin]
        blocks = [z[:, c:c + LANE] * cos + zr[:, c:c + LANE] * sin for c in range(0, width, LANE)]
        return blocks[0] if len(blocks) == 1 else jnp.concatenate(blocks, axis=1)

    gu_ref[...] = mm(OFF_GU, GM_WIDTH)
    gv_ref[...] = mm(OFF_GV, GM_WIDTH)
    heads(dq_ref, roped(OFF_DQ, ROT_DQ, 256, 0, 1))
    heads(dk_ref, roped(OFF_DK, ROT_DK, 256, 2, 3))
    heads_t(dvt_ref, mm(OFF_DV, DA_WIDTH))
    heads(nq_ref, roped(OFF_NQ, ROT_NQ, NSA_WIDTH, 4, 5))
    heads(kc_ref, mm(OFF_KC, NSA_KV_WIDTH))
    heads(vc_ref, mm(OFF_VC, NSA_KV_WIDTH))
    heads(ks_ref, roped(OFF_KS, ROT_KS, NSA_KV_WIDTH, 6, 7))
    heads_t(vst_ref, mm(OFF_VS, NSA_KV_WIDTH))
    heads(kw_ref, roped(OFF_KW, ROT_KW, NSA_KV_WIDTH, 6, 7))
    heads_t(vwt_ref, mm(OFF_VW, NSA_KV_WIDTH))
    gates = jax.nn.sigmoid(mm(OFF_G0, 2 * LANE))
    ngt_ref[0] = gates[:, :LANE].T[:GATE_ROWS]
    ngt_ref[1] = gates[:, LANE:].T[:GATE_ROWS]


def _inproj(x, mod, g, wcat, tabs):
    B, S, D = x.shape
    tm = ATT_TILE
    nt = S // tm

    def hm(nh):
        return (jax.ShapeDtypeStruct((B, nh, S, HEAD_DIM), BF16),
                pl.BlockSpec((None, nh, tm, HEAD_DIM), lambda b, i: (b, 0, i, 0)))

    def hm_t(nh):
        return (jax.ShapeDtypeStruct((B, nh, nt, HEAD_DIM, tm), BF16),
                pl.BlockSpec((None, nh, None, HEAD_DIM, tm), lambda b, i: (b, 0, i, 0, 0)))

    def tok(w):
        return (jax.ShapeDtypeStruct((B, S, w), F32), pl.BlockSpec((None, tm, w), lambda b, i: (b, i, 0)))

    G = NSA_KV_GROUPS
    outs = [tok(GM_WIDTH), tok(GM_WIDTH), hm(DA_HEADS), hm(DA_HEADS), hm_t(DA_HEADS), hm(NSA_HEADS),
            hm(G), hm(G), hm(G), hm_t(G), hm(G), hm_t(G),
            (jax.ShapeDtypeStruct((B, G, GATE_ROWS, S), F32),
             pl.BlockSpec((None, G, GATE_ROWS, tm), lambda b, i: (b, 0, 0, i)))]
    return pl.pallas_call(
        _inproj_kernel,
        grid=(B, nt),
        in_specs=[
            pl.BlockSpec((None, tm, D), lambda b, i: (b, i, 0)),
            pl.BlockSpec((None, N_MOD, D), lambda b, i: (b, 0, 0)),
            pl.BlockSpec((1, D), lambda b, i: (0, 0)),
            _resident(wcat.shape, lambda b, i: (0, 0)),
            pl.BlockSpec((tabs.shape[0], tm, LANE), lambda b, i: (0, i, 0)),
        ],
        out_specs=[o[1] for o in outs],
        out_shape=[o[0] for o in outs],
        scratch_shapes=[pltpu.VMEM((tm, D), BF16)],
        compiler_params=_params(("arbitrary", "arbitrary")),
        name="inproj",
    )(x, mod, g.reshape(1, D), wcat, tabs)


def _gmlp_kernel(gu_ref, gv_ref, lng_ref, ws_ref, bst_ref, o_ref):
    u = _gelu(gu_ref[...])
    v = _gelu(gv_ref[...])
    nch = u.shape[0] // GM_CHUNK
    r = lax.broadcasted_iota(jnp.int32, (GM_CHUNK, GM_CHUNK), 0)
    c = lax.broadcasted_iota(jnp.int32, (GM_CHUNK, GM_CHUNK), 1)
    tril = r >= c
    for g in range(GM_GROUPS):
        lo, hi = g * HEAD_DIM, (g + 1) * HEAD_DIM
        vg = v[:, lo:hi]
        mu = jnp.mean(vg, axis=-1, keepdims=True)
        d = vg - mu
        var = jnp.mean(d * d, axis=-1, keepdims=True)
        vn = (d * lax.rsqrt(var + EPS) * lng_ref[:, lo:hi]).astype(BF16)
        w = jnp.where(tril, ws_ref[g], 0.0).astype(BF16)
        bcol = bst_ref[:, g:g + 1]
        for ch in range(nch):
            t0, t1 = ch * GM_CHUNK, (ch + 1) * GM_CHUNK
            s = _dot(w, vn[t0:t1]) + bcol
            o_ref[t0:t1, lo:hi] = u[t0:t1, lo:hi] * s


def _gmlp(gu, gv, ln_g, w_s, b_s, tm=512):
    B, S, W = gu.shape
    tok = pl.BlockSpec((None, tm, W), lambda b, i: (b, i, 0))
    return pl.pallas_call(
        _gmlp_kernel,
        grid=(B, S // tm),
        in_specs=[tok, tok,
                  pl.BlockSpec((1, W), lambda b, i: (0, 0)),
                  pl.BlockSpec(w_s.shape, lambda b, i: (0, 0, 0)),
                  pl.BlockSpec((GM_CHUNK, GM_GROUPS), lambda b, i: (0, 0))],
        out_specs=tok,
        out_shape=jax.ShapeDtypeStruct((B, S, W), F32),
        compiler_params=_params(("arbitrary", "arbitrary")),
        name="gmlp",
    )(gu, gv, ln_g.reshape(1, W), w_s, b_s.T)


AHEAD = 3


def _online_steps(q_ref, steps, m_ref, l_ref, acc_ref):
    n = len(steps)

    def scores(t):
        c, k, _, bias = steps[t]
        s = _dot_nt(k, q_ref[c * CHUNK:(c + 1) * CHUNK, :])
        return s if bias is None else s + bias()

    s = [scores(t) if t < AHEAD else None for t in range(n)]
    for t in range(n):
        c, _, vt, _ = steps[t]
        m_old = m_ref[c]
        m_new = jnp.maximum(m_old, jnp.max(s[t], axis=0, keepdims=True))
        alpha = jnp.exp2(m_old - m_new)
        p = jnp.exp2(s[t] - m_new)
        s[t] = None
        l_ref[c] = alpha * l_ref[c] + jnp.sum(p, axis=0, keepdims=True)
        m_ref[c] = m_new
        if t + AHEAD < n:
            s[t + AHEAD] = scores(t + AHEAD)
        acc_ref[c] = alpha * acc_ref[c] + _dot(vt, p.astype(BF16))


def _normalized(c0, n, l_ref, acc_ref):
    parts = [acc_ref[c] * (1.0 / l_ref[c]) for c in range(c0, c0 + n)]
    return parts[0] if n == 1 else jnp.concatenate(parts, axis=1)


def _online_init(m_ref, l_ref, acc_ref):
    m_ref[...] = jnp.full_like(m_ref, NEG)
    l_ref[...] = jnp.zeros_like(l_ref)
    acc_ref[...] = jnp.zeros_like(acc_ref)


def _causal_bias(T, upper):
    key = lax.broadcasted_iota(jnp.int32, (T, T), 0)
    qry = lax.broadcasted_iota(jnp.int32, (T, T), 1)
    keep = (key > qry) if upper else (key <= qry)
    return jnp.where(keep, 0.0, NEG)


def _da_kernel(q_ref, k_ref, vt_ref, lam_ref, subg_ref, o_ref, q_sc, bias_sc, m_ref, l_ref, acc_ref, *, lam_init):
    i = pl.program_id(2)
    HP, T, Dh = q_ref.shape
    cpt = T // CHUNK
    lf = lam_ref[...]
    lam = (jnp.exp(jnp.sum(lf[0:1] * lf[1:2], axis=-1, keepdims=True))
           - jnp.exp(jnp.sum(lf[2:3] * lf[3:4], axis=-1, keepdims=True)) + lam_init)
    first = lax.broadcasted_iota(jnp.int32, (1, Dh), 1) < DA_QK_DIM
    zero = jnp.zeros((), BF16)
    for hd in range(HP):
        q = q_ref[hd]
        q_sc[(2 * hd) * T:(2 * hd + 1) * T, :] = jnp.where(first, q, zero)
        q_sc[(2 * hd + 1) * T:(2 * hd + 2) * T, :] = jnp.where(first, zero, q)
    bias_sc[...] = _causal_bias(T, upper=False)
    _online_init(m_ref, l_ref, acc_ref)

    def tile(j, masked):
        start = pl.multiple_of(j * T, T)
        steps = []
        for hd in range(HP):
            k = k_ref[hd, pl.ds(start, T), :]
            vt = vt_ref[hd, j]
            for cc in range(2 * cpt):
                lo = (cc % cpt) * CHUNK
                bias = (lambda lo=lo: bias_sc[:, lo:lo + CHUNK]) if masked else None
                steps.append((hd * 2 * cpt + cc, k, vt, bias))
        return steps

    def run(steps):
        _online_steps(q_sc, steps, m_ref, l_ref, acc_ref)

    def body(jj, carry):
        run(tile(2 * jj, False) + tile(2 * jj + 1, False))
        return carry

    lax.fori_loop(0, i // 2, body, 0)

    @pl.when(i % 2 == 0)
    def _():
        run(tile(i, True))

    @pl.when(i % 2 == 1)
    def _():
        run(tile(i - 1, False) + tile(i, True))

    outs = []
    for hd in range(HP):
        o1 = _normalized(2 * hd * cpt, cpt, l_ref, acc_ref)
        o2 = _normalized((2 * hd + 1) * cpt, cpt, l_ref, acc_ref)
        o = o1 - lam * o2
        o = o * lax.rsqrt(jnp.mean(o * o, axis=0, keepdims=True) + EPS) * subg_ref[...] * (1.0 - lam_init)
        outs.append(o)
    o_ref[...] = jnp.concatenate(outs, axis=0).T


def _diff_attn(dq, dk, dvt, da_lambda, sub_g, lam_init):
    B, H, S, Dh = dq.shape
    T = ATT_TILE
    HP = 2
    rows = HP * 2 * T
    return pl.pallas_call(
        functools.partial(_da_kernel, lam_init=lam_init),
        grid=(B, H // HP, S // T),
        in_specs=[
            pl.BlockSpec((None, HP, T, Dh), lambda b, h, i: (b, h, i, 0)),
            pl.BlockSpec((None, HP, S, Dh), lambda b, h, i: (b, h, 0, 0)),
            pl.BlockSpec((None, HP, S // T, Dh, T), lambda b, h, i: (b, h, 0, 0, 0)),
            pl.BlockSpec(da_lambda.shape, lambda b, h, i: (0, 0)),
            pl.BlockSpec((Dh, T), lambda b, h, i: (0, 0)),
        ],
        out_specs=pl.BlockSpec((None, T, HP * Dh), lambda b, h, i: (b, i, h)),
        out_shape=jax.ShapeDtypeStruct((B, S, H * Dh), F32),
        scratch_shapes=[pltpu.VMEM((rows, Dh), BF16), pltpu.VMEM((T, T), F32),
                        pltpu.VMEM((rows // CHUNK, 1, CHUNK), F32), pltpu.VMEM((rows // CHUNK, 1, CHUNK), F32),
                        pltpu.VMEM((rows // CHUNK, Dh, CHUNK), F32)],
        compiler_params=_params(("arbitrary", "arbitrary", "arbitrary")),
        name="diff_attn",
    )(dq, dk, dvt, da_lambda, jnp.broadcast_to(sub_g[:, None], (Dh, T)))


def _compress_kernel(kc_ref, vc_ref, w1_ref, pe_ref, w2_ref, w2rot_ref, tab_ref, ko_ref, vo_ref):
    R = kc_ref.shape[0]
    half = w1_ref.shape[1] // 2

    def hidden(x_ref, t):
        xr = x_ref[...]
        top = _dot(xr, w1_ref[t, :half, :])
        bot = _dot(xr, w1_ref[t, half:, :])
        pe_rows = jnp.broadcast_to(pe_ref[t], (SUBLANE, 2 * half)).astype(BF16)
        pe = _dot(pe_rows, w1_ref[t])[0:1]
        return _silu(top + pltpu.roll(bot, R - 1, 0) + pe).astype(BF16)

    ak = hidden(kc_ref, 0)
    ko = _dot(ak, w2_ref[0]) * tab_ref[0] + _dot(ak, w2rot_ref[...]) * tab_ref[1]
    vo = _dot(hidden(vc_ref, 1), w2_ref[1])
    ko_ref[...] = ko.astype(BF16)
    vo_ref[...] = jnp.concatenate([ko, vo], axis=1).T[HEAD_DIM:].astype(BF16)


def _compress(kc, vc, w1, pe, w2, w2rot, tab):
    B, G, S, Dh = kc.shape
    R = S // CMP_STRIDE
    W = CMP_STRIDE * Dh
    blk = pl.BlockSpec((None, None, R, W), lambda b, g: (b, g, 0, 0))
    full = lambda a: pl.BlockSpec(a.shape, lambda b, g: (0,) * a.ndim)
    return pl.pallas_call(
        _compress_kernel,
        grid=(B, G),
        in_specs=[blk, blk, full(w1), full(pe), full(w2), full(w2rot), full(tab)],
        out_specs=[pl.BlockSpec((None, None, R, Dh), lambda b, g: (b, g, 0, 0)),
                   pl.BlockSpec((None, None, Dh, R), lambda b, g: (b, g, 0, 0))],
        out_shape=[jax.ShapeDtypeStruct((B, G, R, Dh), BF16), jax.ShapeDtypeStruct((B, G, Dh, R), BF16)],
        compiler_params=_params(("arbitrary", "arbitrary")),
        name="nsa_compress",
    )(kc.reshape(B, G, R, W), vc.reshape(B, G, R, W), w1, pe, w2, w2rot, tab)


def _cmp_kernel(q_ref, kc_ref, vct_ref, ovt_ref, o_ref, sel_ref, *, ns, topk):
    i = pl.program_id(2)
    Hg, T, Dh = q_ref.shape
    R = kc_ref.shape[0]
    NB = sel_ref.shape[0]
    kc = kc_ref[...]
    vct = vct_ref[...]
    pos = i * T + lax.broadcasted_iota(jnp.int32, (R, T), 1)
    cend = lax.broadcasted_iota(jnp.int32, (R, T), 0) * CMP_STRIDE + (CMP_LEN - 1)
    mask = cend <= pos
    scores = [_dot_nt(kc, q_ref[hd]) for hd in range(Hg)]
    ps = jnp.zeros((R, T), F32)
    for hd in range(Hg):
        s = jnp.where(mask, scores[hd], NEG)
        e = jnp.where(mask, jnp.exp2(s - jnp.max(s, axis=0, keepdims=True)), 0.0)
        den = jnp.sum(e, axis=0, keepdims=True)
        p = e * (1.0 / jnp.where(den > 0.0, den, 1.0))
        o_ref[hd] = _dot(vct, p.astype(BF16))
        ps = ps + p

    p_hi = ps.astype(BF16)
    p_lo = (ps - p_hi.astype(F32)).astype(BF16)
    ovt = ovt_ref[...]
    imp = _dot(ovt, p_hi) + _dot(ovt, p_lo)
    blk = lax.broadcasted_iota(jnp.int32, (NB, T), 0)
    cur = (i * T + lax.broadcasted_iota(jnp.int32, (NB, T), 1)) // SLC_LEN
    valid = blk <= cur
    forced = valid & ((blk == 0) | (blk >= cur - 1))
    score = jnp.where(forced, FORCE_BONUS, jnp.where(valid, imp, -1.0))
    score = jnp.where(blk < ns, score, -2.0)
    rows = [score[j:j + 1, :] for j in range(ns)]
    ranks = []
    for lo in range(0, NB, SUBLANE):
        sc = score[lo:lo + SUBLANE, :]
        blk_g = lo + lax.broadcasted_iota(jnp.int32, (SUBLANE, T), 0)
        rank = jnp.zeros((SUBLANE, T), F32)
        for j in range(ns):
            if j < lo:
                ahead = rows[j] >= sc
            elif j >= lo + SUBLANE:
                ahead = rows[j] > sc
            else:
                ahead = (rows[j] > sc) | ((rows[j] == sc) & (blk_g > j))
            rank = rank + jnp.where(ahead, 1.0, 0.0)
        ranks.append(rank)
    rank = jnp.concatenate(ranks, axis=0)
    sel_ref[...] = jnp.where((rank < topk) & (score >= 0.0), 1.0, 0.0).astype(sel_ref.dtype)


def _cmp_attn(nq, kcmp, vcmpt, ovt, ns, topk):
    B, H, S, Dh = nq.shape
    G = kcmp.shape[1]
    R = kcmp.shape[2]
    NB = ovt.shape[0]
    Hg = H // G
    T = ATT_TILE
    return pl.pallas_call(
        functools.partial(_cmp_kernel, ns=ns, topk=topk),
        grid=(B, G, S // T),
        in_specs=[pl.BlockSpec((None, Hg, T, Dh), lambda b, g, i: (b, g, i, 0)),
                  pl.BlockSpec((None, None, R, Dh), lambda b, g, i: (b, g, 0, 0)),
                  pl.BlockSpec((None, None, Dh, R), lambda b, g, i: (b, g, 0, 0)),
                  pl.BlockSpec(ovt.shape, lambda b, g, i: (0, 0))],
        out_specs=[pl.BlockSpec((None, Hg, Dh, T), lambda b, g, i: (b, g, 0, i)),
                   pl.BlockSpec((None, None, NB, T), lambda b, g, i: (b, g, 0, i))],
        out_shape=[jax.ShapeDtypeStruct((B, H, Dh, S), F32), jax.ShapeDtypeStruct((B, G, NB, S), BF16)],
        compiler_params=_params(("arbitrary", "arbitrary", "arbitrary")),
        name="nsa_cmp",
    )(nq, kcmp, vcmpt, ovt)


def _nsa_kernel(q_ref, ks_ref, vst_ref, kw_ref, vwt_ref, oct_ref, sel_ref, ngt_ref, ext_ref, o_ref,
                q_sc, bias_sc, m_ref, l_ref, acc_ref, oslc_sc):
    i = pl.program_id(2)
    Hg, T, Dh = q_ref.shape
    cpt = T // CHUNK
    q_sc[...] = q_ref[...].reshape(Hg * T, Dh)

    def tile(k_ref, vt_ref, j, slot):
        k = k_ref[pl.ds(pl.multiple_of(j * T, T), T), :]
        vt = vt_ref[j]
        steps = []
        for c in range(Hg * cpt):
            lo = (c % cpt) * CHUNK
            bias = None if slot is None else (lambda lo=lo: bias_sc[slot, :, lo:lo + CHUNK])
            steps.append((c, k, vt, bias))
        return steps

    def run(steps):
        _online_steps(q_sc, steps, m_ref, l_ref, acc_ref)

    def slc_bias(slot, j, diagonal):
        chosen = _dot(ext_ref[j], sel) > 0.5
        bias_sc[slot] = jnp.where(chosen, _causal_bias(T, upper=False) if diagonal else 0.0, NEG)

    _online_init(m_ref, l_ref, acc_ref)
    sel = sel_ref[...]

    def slc_body(jj, carry):
        slc_bias(0, 2 * jj, False)
        slc_bias(1, 2 * jj + 1, False)
        run(tile(ks_ref, vst_ref, 2 * jj, 0) + tile(ks_ref, vst_ref, 2 * jj + 1, 1))
        return carry

    lax.fori_loop(0, i // 2, slc_body, 0)

    @pl.when(i % 2 == 0)
    def _():
        slc_bias(0, i, True)
        run(tile(ks_ref, vst_ref, i, 0))

    @pl.when(i % 2 == 1)
    def _():
        slc_bias(0, i - 1, False)
        slc_bias(1, i, True)
        run(tile(ks_ref, vst_ref, i - 1, 0) + tile(ks_ref, vst_ref, i, 1))

    for hd in range(Hg):
        oslc_sc[hd] = _normalized(hd * cpt, cpt, l_ref, acc_ref)

    _online_init(m_ref, l_ref, acc_ref)
    bias_sc[0] = _causal_bias(T, upper=False)
    bias_sc[1] = _causal_bias(T, upper=True)
    back = WIN // T

    def win_steps(n_back):
        steps = tile(kw_ref, vwt_ref, i, 0)
        for d in range(1, n_back + 1):
            steps = steps + tile(kw_ref, vwt_ref, i - d, 1 if d == back else None)
        return steps

    for n_back in range(back):
        @pl.when(i == n_back)
        def _():
            run(win_steps(n_back))

    @pl.when(i >= back)
    def _():
        run(win_steps(back))

    gates = ngt_ref[...]
    outs = []
    for hd in range(Hg):
        o = (gates[3 * hd:3 * hd + 1, :] * oct_ref[hd]
             + gates[3 * hd + 1:3 * hd + 2, :] * oslc_sc[hd]
             + gates[3 * hd + 2:3 * hd + 3, :] * _normalized(hd * cpt, cpt, l_ref, acc_ref))
        outs.append(o)
    for pr in range(Hg // 2):
        o_ref[:, pr * LANE:(pr + 1) * LANE] = jnp.concatenate(outs[2 * pr:2 * pr + 2], axis=0).T


def _nsa_main(nq, ks, vst, kw, vwt, o_cmpt, sel, ngt, expand_t):
    B, H, S, Dh = nq.shape
    G = ks.shape[1]
    Hg = H // G
    T = ATT_TILE
    assert WIN % T == 0 and Hg % 2 == 0
    rows = Hg * T
    kspec = pl.BlockSpec((None, None, S, Dh), lambda b, g, i: (b, g, 0, 0))
    vspec = pl.BlockSpec((None, None, S // T, Dh, T), lambda b, g, i: (b, g, 0, 0, 0))
    NB = sel.shape[2]
    return pl.pallas_call(
        _nsa_kernel,
        grid=(B, G, S // T),
        in_specs=[pl.BlockSpec((None, Hg, T, Dh), lambda b, g, i: (b, g, i, 0)),
                  kspec, vspec, kspec, vspec,
                  pl.BlockSpec((None, Hg, Dh, T), lambda b, g, i: (b, g, 0, i)),
                  pl.BlockSpec((None, None, NB, T), lambda b, g, i: (b, g, 0, i)),
                  pl.BlockSpec((None, None, GATE_ROWS, T), lambda b, g, i: (b, g, 0, i)),
                  pl.BlockSpec(expand_t.shape, lambda b, g, i: (0, 0, 0))],
        out_specs=pl.BlockSpec((None, T, Hg * Dh), lambda b, g, i: (b, i, g)),
        out_shape=jax.ShapeDtypeStruct((B, S, H * Dh), F32),
        scratch_shapes=[pltpu.VMEM((rows, Dh), BF16), pltpu.VMEM((2, T, T), F32),
                        pltpu.VMEM((rows // CHUNK, 1, CHUNK), F32), pltpu.VMEM((rows // CHUNK, 1, CHUNK), F32),
                        pltpu.VMEM((rows // CHUNK, Dh, CHUNK), F32), pltpu.VMEM((Hg, Dh, T), F32)],
        compiler_params=_params(("arbitrary", "arbitrary", "arbitrary")),
        name="nsa_main",
    )(nq, ks, vst, kw, vwt, o_cmpt, sel, ngt, expand_t)


def _outproj_kernel(x_ref, mod_ref, ya_ref, yb_ref, yc_ref, w_ref, o_ref):
    y = _dot(ya_ref[...].astype(BF16), w_ref[0:GM_WIDTH, :])
    y += _dot(yb_ref[...].astype(BF16), w_ref[GM_WIDTH:GM_WIDTH + DA_WIDTH, :])
    y += _dot(yc_ref[...].astype(BF16), w_ref[GM_WIDTH + DA_WIDTH:, :])
    o_ref[...] = x_ref[...] + mod_ref[5:6, :] * y


def _outproj(x, mod, ya, yb, yc, w_out, tm=512):
    B, S, D = x.shape
    tok = lambda w: pl.BlockSpec((None, tm, w), lambda b, i: (b, i, 0))
    return pl.pallas_call(
        _outproj_kernel,
        grid=(B, S // tm),
        in_specs=[tok(D), pl.BlockSpec((None, N_MOD, D), lambda b, i: (b, 0, 0)),
                  tok(GM_WIDTH), tok(DA_WIDTH), tok(NSA_WIDTH),
                  _resident(w_out.shape, lambda b, i: (0, 0))],
        out_specs=tok(D),
        out_shape=jax.ShapeDtypeStruct((B, S, D), F32),
        compiler_params=_params(("arbitrary", "arbitrary")),
        name="outproj",
    )(x, mod, ya, yb, yc, w_out)


def _rope_table(pos, period, width, scale=1.0):
    half = period // 2
    lane = np.arange(width)
    inv = ROPE_THETA ** (-((lane % period) % half).astype(np.float32) / half)
    ang = pos.astype(jnp.float32)[:, None] * jnp.asarray(inv, F32)[None, :]
    return jnp.cos(ang) * scale, jnp.sin(ang) * scale


def _rope_tables(S):
    pos = jnp.arange(S)
    tabs = []
    tabs += _rope_table(pos, DA_QK_DIM, LANE, DA_QK_DIM ** -0.5 * LOG2E)
    tabs += _rope_table(pos, DA_QK_DIM, LANE)
    tabs += _rope_table(pos, HEAD_DIM, LANE, HEAD_DIM ** -0.5 * LOG2E)
    tabs += _rope_table(pos, HEAD_DIM, LANE)
    cmp_end = jnp.arange(S // CMP_STRIDE) * CMP_STRIDE + CMP_LEN - 1
    return jnp.stack(tabs), jnp.stack(_rope_table(cmp_end, HEAD_DIM, HEAD_DIM))


def _rot_cols(w, period):
    d, n = w.shape
    half = period // 2
    c = w.reshape(d, n // period, period)
    return jnp.concatenate([-c[..., half:], c[..., :half]], axis=-1).reshape(d, n)


def _layout_w_in(w):
    d = w.shape[0]
    pad = jnp.zeros((d, LANE - N_GATES // 2), w.dtype)
    parts = [
        w[:, :OFF_NG],
        w[:, OFF_NG:OFF_NG + N_GATES // 2], pad,
        w[:, OFF_NG + N_GATES // 2:OFF_NG + N_GATES], pad,
        _rot_cols(w[:, OFF_DQ:OFF_DQ + 256], DA_QK_DIM),
        _rot_cols(w[:, OFF_DK:OFF_DK + 256], DA_QK_DIM),
        _rot_cols(w[:, OFF_NQ:OFF_NQ + NSA_WIDTH], HEAD_DIM),
        _rot_cols(w[:, OFF_KS:OFF_KS + NSA_KV_WIDTH], HEAD_DIM),
        _rot_cols(w[:, OFF_KW:OFF_KW + NSA_KV_WIDTH], HEAD_DIM),
    ]
    return jnp.concatenate(parts, axis=1).astype(BF16)


def _layout_ffn(w_gate, w_up, w_down):
    d, f = w_gate.shape
    n = f // FF_CHUNK
    wg = w_gate.reshape(d, n, FF_CHUNK).transpose(1, 0, 2)
    wu = w_up.reshape(d, n, FF_CHUNK).transpose(1, 0, 2)
    return jnp.concatenate([wg, wu], axis=-1).astype(BF16), w_down.astype(BF16)


def _overlap_table_t(S, nb):
    nc = (S - CMP_LEN) // CMP_STRIDE + 1
    ns = S // SLC_LEN
    cs = np.arange(nc) * CMP_STRIDE
    bs = np.arange(ns) * SLC_LEN
    ov = np.clip(np.minimum(cs[:, None] + CMP_LEN, bs[None, :] + SLC_LEN) - np.maximum(cs[:, None], bs[None, :]), 0, None)
    out = np.zeros((nb, S // CMP_STRIDE), np.float32)
    out[:ns, :nc] = (ov / CMP_STRIDE).T
    return jnp.asarray(out, BF16)


def _expand_table_t(S, T, nb):
    key_block = np.arange(S).reshape(S // T, T, 1) // SLC_LEN
    return jnp.asarray(key_block == np.arange(nb).reshape(1, 1, nb), BF16)


def kernel(x, c, w_ada, b_ada, norm_g, ffn_w_gate, ffn_w_up, ffn_w_down, w_in, w_out,
           gm_ln_g, gm_w_s, gm_b_s, da_lambda, da_sub_g, nsa_cmp_pe, nsa_cmp_w1, nsa_cmp_w2, final_g):
    B, S, D = x.shape
    depth = w_in.shape[0]
    ns = S // SLC_LEN
    topk = min(SLC_TOPK, ns)
    nb = -(-ns // 16) * 16
    assert S % 512 == 0 and ATT_TILE % CHUNK == 0 and CHUNK % LANE == 0 and ffn_w_gate.shape[-1] % FF_CHUNK == 0

    mod_all = _adaln(c, w_ada, b_ada).reshape(depth, B, N_MOD, D)
    tabs, cmp_tab = _rope_tables(S)
    ovt = _overlap_table_t(S, nb)
    expand_t = _expand_table_t(S, ATT_TILE, nb)

    for l in range(depth):
        mod = mod_all[l]
        lam_init = 0.8 - 0.6 * math.exp(-0.3 * l)
        wgu, wd = _layout_ffn(ffn_w_gate[l, 0], ffn_w_up[l, 0], ffn_w_down[l, 0])
        x = _ffn(x, mod, norm_g[l, 0], wgu, wd, rows=(0, 1, 2))

        (gu, gv, dq, dk, dvt, nq, kc, vc, ks, vst, kw, vwt, ngt) = _inproj(
            x, mod, norm_g[l, 1], _layout_w_in(w_in[l]), tabs)
        y_a = _gmlp(gu, gv, gm_ln_g[l], gm_w_s[l], gm_b_s[l])
        y_b = _diff_attn(dq, dk, dvt, da_lambda[l], da_sub_g[l], lam_init)
        kcmp, vcmpt = _compress(
            kc, vc, nsa_cmp_w1[l].astype(BF16), nsa_cmp_pe[l].reshape(2, 1, CMP_LEN * HEAD_DIM),
            nsa_cmp_w2[l].astype(BF16), _rot_cols(nsa_cmp_w2[l, 0], HEAD_DIM).astype(BF16), cmp_tab)
        o_cmpt, sel = _cmp_attn(nq, kcmp, vcmpt, ovt, ns, topk)
        y_c = _nsa_main(nq, ks, vst, kw, vwt, o_cmpt, sel, ngt, expand_t)
        x = _outproj(x, mod, y_a, y_b, y_c, w_out[l].astype(BF16))

        wgu, wd = _layout_ffn(ffn_w_gate[l, 1], ffn_w_up[l, 1], ffn_w_down[l, 1])
        x = _ffn(x, mod, norm_g[l, 2], wgu, wd, rows=(6, 7, 8), final_g=final_g if l == depth - 1 else None)
    return x
```

```python
import functools
import math

import numpy as np
import jax
import jax.numpy as jnp
from jax import lax
from jax.experimental import pallas as pl
from jax.experimental.pallas import tpu as pltpu

F32 = jnp.float32
BF16 = jnp.bfloat16

HEAD_DIM = 64
GM_GROUPS = 4
GM_CHUNK = 128
GM_WIDTH = GM_GROUPS * HEAD_DIM
DA_HEADS = 4
DA_QK_DIM = HEAD_DIM // 2
DA_WIDTH = DA_HEADS * HEAD_DIM
NSA_HEADS = 8
NSA_KV_GROUPS = 2
NSA_HG = NSA_HEADS // NSA_KV_GROUPS
NSA_WIDTH = NSA_HEADS * HEAD_DIM
NSA_KV_WIDTH = NSA_KV_GROUPS * HEAD_DIM
N_GATES = 3 * NSA_HEADS
GATE_ROWS = 16
CMP_LEN = 32
CMP_STRIDE = 16
SLC_LEN = 64
SLC_TOPK = 16
WIN = 512
FORCE_BONUS = 1000.0
ROPE_THETA = 10000.0
EPS = 1e-6
N_MOD = 9
NEG = -1e30
LOG2E = math.log2(math.e)

LANE = 128
SUBLANE = 8
FF_CHUNK = 256
ATT_TILE = 256
CHUNK = 256
VMEM_LIMIT = 56 * 1024 * 1024

OFF_GU, OFF_GV, OFF_DQ, OFF_DK, OFF_DV = 0, 256, 512, 768, 1024
OFF_NQ, OFF_KC, OFF_VC, OFF_KS, OFF_VS, OFF_KW, OFF_VW, OFF_NG = 1280, 1792, 1920, 2048, 2176, 2304, 2432, 2560
OFF_G0 = 2560
OFF_G1 = OFF_G0 + LANE
W_CAT = OFF_G1 + LANE


def _params(sem):
    return pltpu.CompilerParams(dimension_semantics=sem, vmem_limit_bytes=VMEM_LIMIT)


def _resident(shape, index_map):
    return pl.BlockSpec(shape, index_map, pipeline_mode=pl.Buffered(1))


def _dot(a, b):
    return jnp.dot(a, b, preferred_element_type=F32)


def _dot_nt(a, b):
    return lax.dot_general(a, b, (((1,), (1,)), ((), ())), preferred_element_type=F32)


def _mod_norm(x, g, shift, scale):
    ms = jnp.mean(x * x, axis=-1, keepdims=True)
    return x * lax.rsqrt(ms + EPS) * g * (1.0 + scale) + shift


def _gelu(x):
    c = math.sqrt(2.0 / math.pi)
    return 0.5 * x * (1.0 + jnp.tanh(c * (x + 0.044715 * (x * x * x))))


def _silu(x):
    return x * jax.nn.sigmoid(x)


def _adaln_kernel(c_ref, w_ref, b_ref, o_ref):
    ca = _silu(c_ref[...])
    o_ref[...] = _dot(ca, w_ref[...]) + b_ref[...]


def _adaln(c, w_ada, b_ada):
    L, D, ND = w_ada.shape
    B = c.shape[0]
    tn = ND // 8
    return pl.pallas_call(
        _adaln_kernel,
        grid=(L, ND // tn),
        in_specs=[
            pl.BlockSpec((B, D), lambda l, j: (0, 0)),
            pl.BlockSpec((None, D, tn), lambda l, j: (l, 0, j)),
            pl.BlockSpec((None, 1, tn), lambda l, j: (l, 0, j)),
        ],
        out_specs=pl.BlockSpec((None, B, tn), lambda l, j: (l, 0, j)),
        out_shape=jax.ShapeDtypeStruct((L, B, ND), F32),
        compiler_params=_params(("arbitrary", "arbitrary")),
        name="adaln",
    )(c, w_ada, b_ada.reshape(L, 1, ND))


def _ffn_kernel(*refs, rows, nchunk, mixed, final):
    refs = list(refs)
    x_ref, mod_ref, g_ref, wg_ref, wu_ref, wd_ref = refs[:6]
    rest = refs[6:]
    if mixed:
        ya_ref, yb_ref, yc_ref, wo_ref = rest[:4]
        rest = rest[4:]
    if final:
        fg_ref = rest[0]
        rest = rest[1:]
    o_ref, h_ref, a_ref = rest[:3]
    r_sh, r_sc, r_gt = rows
    if mixed:
        x_sc = rest[3]
        y = _dot(ya_ref[...], wo_ref[0:GM_WIDTH, :])
        y += _dot(yb_ref[...], wo_ref[GM_WIDTH:GM_WIDTH + DA_WIDTH, :])
        y += _dot(yc_ref[...], wo_ref[GM_WIDTH + DA_WIDTH:, :])
        x_sc[...] = x_ref[...] + mod_ref[5:6, :] * y
        x_ref = x_sc
    h_ref[...] = _mod_norm(x_ref[...], g_ref[...], mod_ref[r_sh:r_sh + 1, :], mod_ref[r_sc:r_sc + 1, :]).astype(BF16)
    for j in range(nchunk):
        cols = slice(j * FF_CHUNK, (j + 1) * FF_CHUNK)
        gate = _dot(h_ref[...], wg_ref[:, cols])
        up = _dot(h_ref[...], wu_ref[:, cols])
        a_ref[:, cols] = (_silu(gate) * up).astype(BF16)
    y = x_ref[...] + 0.5 * mod_ref[r_gt:r_gt + 1, :] * _dot(a_ref[...], wd_ref[...])
    if final:
        y = y * lax.rsqrt(jnp.mean(y * y, axis=-1, keepdims=True) + EPS) * fg_ref[...]
    o_ref[...] = y


def _ffn(x, mod, g, wg, wu, wd, rows, mix=None, final_g=None, tm=512):
    B, S, D = x.shape
    F = wd.shape[0]
    mixed = mix is not None
    final = final_g is not None
    tok = lambda w: pl.BlockSpec((None, tm, w), lambda b, i: (b, i, 0))
    row = pl.BlockSpec((1, D), lambda b, i: (0, 0))
    in_specs = [tok(D), pl.BlockSpec((None, N_MOD, D), lambda b, i: (b, 0, 0)), row,
                _resident(wg.shape, lambda b, i: (0, 0)), _resident(wu.shape, lambda b, i: (0, 0)),
                _resident(wd.shape, lambda b, i: (0, 0))]
    args = [x, mod, g.reshape(1, D), wg, wu, wd]
    scratch = [pltpu.VMEM((tm, D), BF16), pltpu.VMEM((tm, F), BF16)]
    if mixed:
        ya, yb, yc, w_out = mix
        in_specs += [tok(ya.shape[-1]), tok(yb.shape[-1]), tok(yc.shape[-1]), _resident(w_out.shape, lambda b, i: (0, 0))]
        args += [ya, yb, yc, w_out]
        scratch.append(pltpu.VMEM((tm, D), F32))
    if final:
        in_specs.append(row)
        args.append(final_g.reshape(1, D))
    return pl.pallas_call(
        functools.partial(_ffn_kernel, rows=rows, nchunk=F // FF_CHUNK, mixed=mixed, final=final),
        grid=(B, S // tm),
        in_specs=in_specs,
        out_specs=tok(D),
        out_shape=jax.ShapeDtypeStruct((B, S, D), F32),
        scratch_shapes=scratch,
        compiler_params=_params(("arbitrary", "arbitrary")),
        name="ffn" + ("_mix" if mixed else "") + ("_final" if final else ""),
    )(*args)


def _inproj_kernel(x_ref, mod_ref, g_ref, w_ref, tab_ref, lng_ref, ws_ref, bs_ref, avg_ref,
                   ya_ref, dq_ref, dk_ref, dvt_ref, nq_ref,
                   kc_ref, vc_ref, ks_ref, vst_ref, kw_ref, vwt_ref, ngt_ref, h_ref):
    h_ref[...] = _mod_norm(x_ref[...], g_ref[...], mod_ref[3:4, :], mod_ref[4:5, :]).astype(BF16)
    tm = h_ref.shape[0]

    def mm(off, width):
        return _dot(h_ref[...], w_ref[:, off:off + width])

    def heads(ref, z):
        for hd in range(z.shape[1] // HEAD_DIM):
            ref[hd] = z[:, hd * HEAD_DIM:(hd + 1) * HEAD_DIM].astype(ref.dtype)

    def heads_t(ref, z):
        for c in range(z.shape[1] // LANE):
            zt = z[:, c * LANE:(c + 1) * LANE].T
            ref[2 * c] = zt[:HEAD_DIM].astype(ref.dtype)
            ref[2 * c + 1] = zt[HEAD_DIM:].astype(ref.dtype)

    def rope(z, half, t):
        blocks = []
        for c in range(0, z.shape[1], LANE):
            zb = z[:, c:c + LANE]
            blocks.append(zb * tab_ref[t] + pltpu.roll(zb, LANE - half, 1) * tab_ref[t + 1]
                          + pltpu.roll(zb, half, 1) * tab_ref[t + 2])
        return blocks[0] if len(blocks) == 1 else jnp.concatenate(blocks, axis=1)

    avg = avg_ref[...]

    def group_mean(t):
        hi = t.astype(BF16)
        lo = (t - hi.astype(F32)).astype(BF16)
        return _dot(hi, avg) + _dot(lo, avg)

    u_raw = mm(OFF_GU, GM_WIDTH)
    v_raw = mm(OFF_GV, GM_WIDTH)
    heads(dq_ref, rope(mm(OFF_DQ, 256), DA_QK_DIM // 2, 0))
    v = _gelu(v_raw)
    d = v - group_mean(v)
    heads(dk_ref, rope(mm(OFF_DK, 256), DA_QK_DIM // 2, 3))
    var = group_mean(d * d)
    heads_t(dvt_ref, mm(OFF_DV, DA_WIDTH))
    vn = (d * lax.rsqrt(var + EPS) * lng_ref[...]).astype(BF16)
    heads(nq_ref, rope(mm(OFF_NQ, NSA_WIDTH), HEAD_DIM // 2, 6))

    u = _gelu(u_raw)
    r = lax.broadcasted_iota(jnp.int32, (GM_CHUNK, GM_CHUNK), 0)
    c = lax.broadcasted_iota(jnp.int32, (GM_CHUNK, GM_CHUNK), 1)
    w_sp = [jnp.where(r >= c, ws_ref[g], 0.0).astype(BF16) for g in range(GM_GROUPS)]
    group = lax.broadcasted_iota(jnp.int32, (GM_CHUNK, GM_WIDTH), 1) // HEAD_DIM
    for ch in range(tm // GM_CHUNK):
        t0, t1 = ch * GM_CHUNK, (ch + 1) * GM_CHUNK
        s = bs_ref[...]
        for g in range(GM_GROUPS):
            s = s + jnp.where(group == g, _dot(w_sp[g], vn[t0:t1]), 0.0)
        ya_ref[t0:t1, :] = (u[t0:t1] * s).astype(ya_ref.dtype)

    kv = mm(OFF_KC, 2 * NSA_KV_WIDTH)
    heads(kc_ref, kv[:, :NSA_KV_WIDTH])
    heads(vc_ref, kv[:, NSA_KV_WIDTH:])
    kv = mm(OFF_KS, 2 * NSA_KV_WIDTH)
    heads(ks_ref, rope(kv[:, :NSA_KV_WIDTH], HEAD_DIM // 2, 9))
    heads_t(vst_ref, kv[:, NSA_KV_WIDTH:])
    kv = mm(OFF_KW, 2 * NSA_KV_WIDTH)
    heads(kw_ref, rope(kv[:, :NSA_KV_WIDTH], HEAD_DIM // 2, 9))
    heads_t(vwt_ref, kv[:, NSA_KV_WIDTH:])
    gates = jax.nn.sigmoid(mm(OFF_G0, 2 * LANE))
    ngt_ref[0] = gates[:, :LANE].T[:GATE_ROWS]
    ngt_ref[1] = gates[:, LANE:].T[:GATE_ROWS]


def _inproj(x, mod, g, wcat, tabs, ln_g, w_s, b_s):
    B, S, D = x.shape
    tm = ATT_TILE
    nt = S // tm

    def hm(nh):
        return (jax.ShapeDtypeStruct((B, nh, S, HEAD_DIM), BF16),
                pl.BlockSpec((None, nh, tm, HEAD_DIM), lambda b, i: (b, 0, i, 0)))

    def hm_t(nh):
        return (jax.ShapeDtypeStruct((B, nh, nt, HEAD_DIM, tm), BF16),
                pl.BlockSpec((None, nh, None, HEAD_DIM, tm), lambda b, i: (b, 0, i, 0, 0)))

    G = NSA_KV_GROUPS
    outs = [(jax.ShapeDtypeStruct((B, S, GM_WIDTH), BF16), pl.BlockSpec((None, tm, GM_WIDTH), lambda b, i: (b, i, 0))),
            hm(DA_HEADS), hm(DA_HEADS), hm_t(DA_HEADS), hm(NSA_HEADS),
            hm(G), hm(G), hm(G), hm_t(G), hm(G), hm_t(G),
            (jax.ShapeDtypeStruct((B, G, GATE_ROWS, S), F32),
             pl.BlockSpec((None, G, GATE_ROWS, tm), lambda b, i: (b, 0, 0, i)))]
    bias = jnp.repeat(b_s.T, HEAD_DIM, axis=1)
    lane_group = np.arange(GM_WIDTH) // HEAD_DIM
    avg = jnp.asarray((lane_group[:, None] == lane_group[None, :]) / HEAD_DIM, BF16)
    full = lambda a: pl.BlockSpec(a.shape, lambda b, i: (0,) * a.ndim)
    return pl.pallas_call(
        _inproj_kernel,
        grid=(B, nt),
        in_specs=[
            pl.BlockSpec((None, tm, D), lambda b, i: (b, i, 0)),
            pl.BlockSpec((None, N_MOD, D), lambda b, i: (b, 0, 0)),
            pl.BlockSpec((1, D), lambda b, i: (0, 0)),
            _resident(wcat.shape, lambda b, i: (0, 0)),
            pl.BlockSpec((tabs.shape[0], tm, LANE), lambda b, i: (0, i, 0)),
            pl.BlockSpec((1, GM_WIDTH), lambda b, i: (0, 0)), full(w_s), full(bias), full(avg),
        ],
        out_specs=[o[1] for o in outs],
        out_shape=[o[0] for o in outs],
        scratch_shapes=[pltpu.VMEM((tm, D), BF16)],
        compiler_params=_params(("arbitrary", "arbitrary")),
        name="inproj",
    )(x, mod, g.reshape(1, D), wcat, tabs, ln_g.reshape(1, GM_WIDTH), w_s, bias, avg)


AHEAD = 4


def _online_steps(q_ref, steps, m_ref, l_ref, acc_ref):
    n = len(steps)

    def scores(t):
        c, k, _, bias = steps[t]
        s = _dot_nt(k, q_ref[c * CHUNK:(c + 1) * CHUNK, :])
        return s if bias is None else s + bias()

    s = [scores(t) if t < AHEAD else None for t in range(n)]
    for t in range(n):
        c, _, vt, _ = steps[t]
        m_old = m_ref[c]
        m_new = jnp.maximum(m_old, jnp.max(s[t], axis=0, keepdims=True))
        alpha = jnp.exp2(m_old - m_new)
        p = jnp.exp2(s[t] - m_new)
        s[t] = None
        l_ref[c] = alpha * l_ref[c] + jnp.sum(p, axis=0, keepdims=True)
        m_ref[c] = m_new
        if t + AHEAD < n:
            s[t + AHEAD] = scores(t + AHEAD)
        acc_ref[c] = alpha * acc_ref[c] + _dot(vt, p.astype(BF16))


def _normalized(c0, n, l_ref, acc_ref):
    parts = [acc_ref[c] * (1.0 / l_ref[c]) for c in range(c0, c0 + n)]
    return parts[0] if n == 1 else jnp.concatenate(parts, axis=1)


def _online_init(m_ref, l_ref, acc_ref):
    m_ref[...] = jnp.full_like(m_ref, NEG)
    l_ref[...] = jnp.zeros_like(l_ref)
    acc_ref[...] = jnp.zeros_like(acc_ref)


def _causal_bias(T, upper):
    key = lax.broadcasted_iota(jnp.int32, (T, T), 0)
    qry = lax.broadcasted_iota(jnp.int32, (T, T), 1)
    keep = (key > qry) if upper else (key <= qry)
    return jnp.where(keep, 0.0, NEG)


def _da_kernel(q_ref, k_ref, vt_ref, lam_ref, subg_ref, o_ref, q_sc, bias_sc, m_ref, l_ref, acc_ref, *, lam_init):
    i = pl.program_id(2)
    HP, T, Dh = q_ref.shape
    cpt = T // CHUNK
    lf = lam_ref[...]
    lam = (jnp.exp(jnp.sum(lf[0:1] * lf[1:2], axis=-1, keepdims=True))
           - jnp.exp(jnp.sum(lf[2:3] * lf[3:4], axis=-1, keepdims=True)) + lam_init)
    first = lax.broadcasted_iota(jnp.int32, (1, Dh), 1) < DA_QK_DIM
    zero = jnp.zeros((), BF16)
    for hd in range(HP):
        q = q_ref[hd]
        q_sc[(2 * hd) * T:(2 * hd + 1) * T, :] = jnp.where(first, q, zero)
        q_sc[(2 * hd + 1) * T:(2 * hd + 2) * T, :] = jnp.where(first, zero, q)
    bias_sc[...] = _causal_bias(T, upper=False)
    _online_init(m_ref, l_ref, acc_ref)

    def tile(j, masked):
        start = pl.multiple_of(j * T, T)
        steps = []
        for hd in range(HP):
            k = k_ref[hd, pl.ds(start, T), :]
            vt = vt_ref[hd, j]
            for cc in range(2 * cpt):
                lo = (cc % cpt) * CHUNK
                bias = (lambda lo=lo: bias_sc[:, lo:lo + CHUNK]) if masked else None
                steps.append((hd * 2 * cpt + cc, k, vt, bias))
        return steps

    def run(steps):
        _online_steps(q_sc, steps, m_ref, l_ref, acc_ref)

    def body(jj, carry):
        run(tile(2 * jj, False) + tile(2 * jj + 1, False))
        return carry

    lax.fori_loop(0, i // 2, body, 0)

    @pl.when(i % 2 == 0)
    def _():
        run(tile(i, True))

    @pl.when(i % 2 == 1)
    def _():
        run(tile(i - 1, False) + tile(i, True))

    outs = []
    for hd in range(HP):
        o1 = _normalized(2 * hd * cpt, cpt, l_ref, acc_ref)
        o2 = _normalized((2 * hd + 1) * cpt, cpt, l_ref, acc_ref)
        o = o1 - lam * o2
        o = o * lax.rsqrt(jnp.mean(o * o, axis=0, keepdims=True) + EPS) * subg_ref[...] * (1.0 - lam_init)
        outs.append(o)
    o_ref[...] = jnp.concatenate(outs, axis=0).T.astype(o_ref.dtype)


def _diff_attn(dq, dk, dvt, da_lambda, sub_g, lam_init):
    B, H, S, Dh = dq.shape
    T = ATT_TILE
    HP = 2
    rows = HP * 2 * T
    return pl.pallas_call(
        functools.partial(_da_kernel, lam_init=lam_init),
        grid=(B, H // HP, S // T),
        in_specs=[
            pl.BlockSpec((None, HP, T, Dh), lambda b, h, i: (b, h, i, 0)),
            pl.BlockSpec((None, HP, S, Dh), lambda b, h, i: (b, h, 0, 0)),
            pl.BlockSpec((None, HP, S // T, Dh, T), lambda b, h, i: (b, h, 0, 0, 0)),
            pl.BlockSpec(da_lambda.shape, lambda b, h, i: (0, 0)),
            pl.BlockSpec((Dh, T), lambda b, h, i: (0, 0)),
        ],
        out_specs=pl.BlockSpec((None, T, HP * Dh), lambda b, h, i: (b, i, h)),
        out_shape=jax.ShapeDtypeStruct((B, S, H * Dh), BF16),
        scratch_shapes=[pltpu.VMEM((rows, Dh), BF16), pltpu.VMEM((T, T), F32),
                        pltpu.VMEM((rows // CHUNK, 1, CHUNK), F32), pltpu.VMEM((rows // CHUNK, 1, CHUNK), F32),
                        pltpu.VMEM((rows // CHUNK, Dh, CHUNK), F32)],
        compiler_params=_params(("arbitrary", "arbitrary", "arbitrary")),
        name="diff_attn",
    )(dq, dk, dvt, da_lambda, jnp.broadcast_to(sub_g[:, None], (Dh, T)))


def _compress_kernel(kc_ref, vc_ref, w1_ref, pe_ref, w2_ref, w2rot_ref, tab_ref, ko_ref, vo_ref):
    R = kc_ref.shape[0]
    half = w1_ref.shape[1] // 2

    def hidden(x_ref, t):
        xr = x_ref[...]
        top = _dot(xr, w1_ref[t, :half, :])
        bot = _dot(xr, w1_ref[t, half:, :])
        pe_rows = jnp.broadcast_to(pe_ref[t], (SUBLANE, 2 * half)).astype(BF16)
        pe = _dot(pe_rows, w1_ref[t])[0:1]
        return _silu(top + pltpu.roll(bot, R - 1, 0) + pe).astype(BF16)

    ak = hidden(kc_ref, 0)
    ko = _dot(ak, w2_ref[0]) * tab_ref[0] + _dot(ak, w2rot_ref[...]) * tab_ref[1]
    vo = _dot(hidden(vc_ref, 1), w2_ref[1])
    ko_ref[...] = ko.astype(BF16)
    vo_ref[...] = jnp.concatenate([ko, vo], axis=1).T[HEAD_DIM:].astype(BF16)


def _compress(kc, vc, w1, pe, w2, w2rot, tab):
    B, G, S, Dh = kc.shape
    R = S // CMP_STRIDE
    W = CMP_STRIDE * Dh
    blk = pl.BlockSpec((None, None, R, W), lambda b, g: (b, g, 0, 0))
    full = lambda a: pl.BlockSpec(a.shape, lambda b, g: (0,) * a.ndim)
    return pl.pallas_call(
        _compress_kernel,
        grid=(B, G),
        in_specs=[blk, blk, full(w1), full(pe), full(w2), full(w2rot), full(tab)],
        out_specs=[pl.BlockSpec((None, None, R, Dh), lambda b, g: (b, g, 0, 0)),
                   pl.BlockSpec((None, None, Dh, R), lambda b, g: (b, g, 0, 0))],
        out_shape=[jax.ShapeDtypeStruct((B, G, R, Dh), BF16), jax.ShapeDtypeStruct((B, G, Dh, R), BF16)],
        compiler_params=_params(("arbitrary", "arbitrary")),
        name="nsa_compress",
    )(kc.reshape(B, G, R, W), vc.reshape(B, G, R, W), w1, pe, w2, w2rot, tab)


def _cmp_kernel(q_ref, kc_ref, vct_ref, ovt_ref, o_ref, sel_ref, *, ns, topk):
    i = pl.program_id(2)
    Hg, T, Dh = q_ref.shape
    R = kc_ref.shape[0]
    NB = sel_ref.shape[0]
    kc = kc_ref[...]
    vct = vct_ref[...]
    pos = i * T + lax.broadcasted_iota(jnp.int32, (R, T), 1)
    cend = lax.broadcasted_iota(jnp.int32, (R, T), 0) * CMP_STRIDE + (CMP_LEN - 1)
    mask = cend <= pos
    scores = [_dot_nt(kc, q_ref[hd]) for hd in range(Hg)]
    ps = jnp.zeros((R, T), F32)
    for hd in range(Hg):
        s = jnp.where(mask, scores[hd], NEG)
        e = jnp.where(mask, jnp.exp2(s - jnp.max(s, axis=0, keepdims=True)), 0.0)
        den = jnp.sum(e, axis=0, keepdims=True)
        p = e * (1.0 / jnp.where(den > 0.0, den, 1.0))
        o_ref[hd] = _dot(vct, p.astype(BF16))
        ps = ps + p

    p_hi = ps.astype(BF16)
    p_lo = (ps - p_hi.astype(F32)).astype(BF16)
    ovt = ovt_ref[...]
    imp = _dot(ovt, p_hi) + _dot(ovt, p_lo)
    blk = lax.broadcasted_iota(jnp.int32, (NB, T), 0)
    cur = (i * T + lax.broadcasted_iota(jnp.int32, (NB, T), 1)) // SLC_LEN
    valid = blk <= cur
    forced = valid & ((blk == 0) | (blk >= cur - 1))
    score = jnp.where(forced, FORCE_BONUS, jnp.where(valid, imp, -1.0))
    score = jnp.where(blk < ns, score, -2.0)
    rows = [score[j:j + 1, :] for j in range(ns)]
    ranks = []
    for lo in range(0, NB, SUBLANE):
        sc = score[lo:lo + SUBLANE, :]
        blk_g = lo + lax.broadcasted_iota(jnp.int32, (SUBLANE, T), 0)
        rank = jnp.zeros((SUBLANE, T), F32)
        for j in range(ns):
            if j < lo:
                ahead = rows[j] >= sc
            elif j >= lo + SUBLANE:
                ahead = rows[j] > sc
            else:
                ahead = (rows[j] > sc) | ((rows[j] == sc) & (blk_g > j))
            rank = rank + jnp.where(ahead, 1.0, 0.0)
        ranks.append(rank)
    rank = jnp.concatenate(ranks, axis=0)
    sel_ref[...] = jnp.where((rank < topk) & (score >= 0.0), 1.0, 0.0).astype(sel_ref.dtype)


def _cmp_attn(nq, kcmp, vcmpt, ovt, ns, topk):
    B, H, S, Dh = nq.shape
    G = kcmp.shape[1]
    R = kcmp.shape[2]
    NB = ovt.shape[0]
    Hg = H // G
    T = ATT_TILE
    return pl.pallas_call(
        functools.partial(_cmp_kernel, ns=ns, topk=topk),
        grid=(B, G, S // T),
        in_specs=[pl.BlockSpec((None, Hg, T, Dh), lambda b, g, i: (b, g, i, 0)),
                  pl.BlockSpec((None, None, R, Dh), lambda b, g, i: (b, g, 0, 0)),
                  pl.BlockSpec((None, None, Dh, R), lambda b, g, i: (b, g, 0, 0)),
                  pl.BlockSpec(ovt.shape, lambda b, g, i: (0, 0))],
        out_specs=[pl.BlockSpec((None, Hg, Dh, T), lambda b, g, i: (b, g, 0, i)),
                   pl.BlockSpec((None, None, NB, T), lambda b, g, i: (b, g, 0, i))],
        out_shape=[jax.ShapeDtypeStruct((B, H, Dh, S), F32), jax.ShapeDtypeStruct((B, G, NB, S), BF16)],
        compiler_params=_params(("arbitrary", "arbitrary", "arbitrary")),
        name="nsa_cmp",
    )(nq, kcmp, vcmpt, ovt)


def _nsa_kernel(q_ref, ks_ref, vst_ref, kw_ref, vwt_ref, oct_ref, sel_ref, ngt_ref, ext_ref, o_ref,
                q_sc, bias_sc, m_ref, l_ref, acc_ref, oslc_sc):
    i = pl.program_id(2)
    Hg, T, Dh = q_ref.shape
    cpt = T // CHUNK
    q_sc[...] = q_ref[...].reshape(Hg * T, Dh)

    def tile(k_ref, vt_ref, j, slot):
        k = k_ref[pl.ds(pl.multiple_of(j * T, T), T), :]
        vt = vt_ref[j]
        steps = []
        for c in range(Hg * cpt):
            lo = (c % cpt) * CHUNK
            bias = None if slot is None else (lambda lo=lo: bias_sc[slot, :, lo:lo + CHUNK])
            steps.append((c, k, vt, bias))
        return steps

    def run(steps):
        _online_steps(q_sc, steps, m_ref, l_ref, acc_ref)

    def slc_bias(slot, j, diagonal):
        chosen = _dot(ext_ref[j], sel) > 0.5
        bias_sc[slot] = jnp.where(chosen, _causal_bias(T, upper=False) if diagonal else 0.0, NEG)

    _online_init(m_ref, l_ref, acc_ref)
    sel = sel_ref[...]

    def slc_body(jj, carry):
        slc_bias(0, 2 * jj, False)
        slc_bias(1, 2 * jj + 1, False)
        run(tile(ks_ref, vst_ref, 2 * jj, 0) + tile(ks_ref, vst_ref, 2 * jj + 1, 1))
        return carry

    lax.fori_loop(0, i // 2, slc_body, 0)

    @pl.when(i % 2 == 0)
    def _():
        slc_bias(0, i, True)
        run(tile(ks_ref, vst_ref, i, 0))

    @pl.when(i % 2 == 1)
    def _():
        slc_bias(0, i - 1, False)
        slc_bias(1, i, True)
        run(tile(ks_ref, vst_ref, i - 1, 0) + tile(ks_ref, vst_ref, i, 1))

    for hd in range(Hg):
        oslc_sc[hd] = _normalized(hd * cpt, cpt, l_ref, acc_ref)

    _online_init(m_ref, l_ref, acc_ref)
    bias_sc[0] = _causal_bias(T, upper=False)
    bias_sc[1] = _causal_bias(T, upper=True)
    back = WIN // T

    def win_steps(n_back):
        steps = tile(kw_ref, vwt_ref, i, 0)
        for d in range(1, n_back + 1):
            steps = steps + tile(kw_ref, vwt_ref, i - d, 1 if d == back else None)
        return steps

    for n_back in range(back):
        @pl.when(i == n_back)
        def _():
            run(win_steps(n_back))

    @pl.when(i >= back)
    def _():
        run(win_steps(back))

    gates = ngt_ref[...]
    outs = []
    for hd in range(Hg):
        o = (gates[3 * hd:3 * hd + 1, :] * oct_ref[hd]
             + gates[3 * hd + 1:3 * hd + 2, :] * oslc_sc[hd]
             + gates[3 * hd + 2:3 * hd + 3, :] * _normalized(hd * cpt, cpt, l_ref, acc_ref))
        outs.append(o)
    for pr in range(Hg // 2):
        o_ref[:, pr * LANE:(pr + 1) * LANE] = jnp.concatenate(outs[2 * pr:2 * pr + 2], axis=0).T.astype(o_ref.dtype)


def _nsa_main(nq, ks, vst, kw, vwt, o_cmpt, sel, ngt, expand_t):
    B, H, S, Dh = nq.shape
    G = ks.shape[1]
    Hg = H // G
    T = ATT_TILE
    assert WIN % T == 0 and Hg % 2 == 0
    rows = Hg * T
    kspec = pl.BlockSpec((None, None, S, Dh), lambda b, g, i: (b, g, 0, 0))
    vspec = pl.BlockSpec((None, None, S // T, Dh, T), lambda b, g, i: (b, g, 0, 0, 0))
    NB = sel.shape[2]
    return pl.pallas_call(
        _nsa_kernel,
        grid=(B, G, S // T),
        in_specs=[pl.BlockSpec((None, Hg, T, Dh), lambda b, g, i: (b, g, i, 0)),
                  kspec, vspec, kspec, vspec,
                  pl.BlockSpec((None, Hg, Dh, T), lambda b, g, i: (b, g, 0, i)),
                  pl.BlockSpec((None, None, NB, T), lambda b, g, i: (b, g, 0, i)),
                  pl.BlockSpec((None, None, GATE_ROWS, T), lambda b, g, i: (b, g, 0, i)),
                  pl.BlockSpec(expand_t.shape, lambda b, g, i: (0, 0, 0))],
        out_specs=pl.BlockSpec((None, T, Hg * Dh), lambda b, g, i: (b, i, g)),
        out_shape=jax.ShapeDtypeStruct((B, S, H * Dh), BF16),
        scratch_shapes=[pltpu.VMEM((rows, Dh), BF16), pltpu.VMEM((2, T, T), F32),
                        pltpu.VMEM((rows // CHUNK, 1, CHUNK), F32), pltpu.VMEM((rows // CHUNK, 1, CHUNK), F32),
                        pltpu.VMEM((rows // CHUNK, Dh, CHUNK), F32), pltpu.VMEM((Hg, Dh, T), F32)],
        compiler_params=_params(("arbitrary", "arbitrary", "arbitrary")),
        name="nsa_main",
    )(nq, ks, vst, kw, vwt, o_cmpt, sel, ngt, expand_t)


def _rope_angles(pos, period, width):
    half = period // 2
    local = np.arange(width) % period
    inv = ROPE_THETA ** (-(local % half).astype(np.float32) / half)
    return pos.astype(jnp.float32)[:, None] * jnp.asarray(inv, F32)[None, :], jnp.asarray(local < half)


def _rope_table(pos, period, width):
    ang, _ = _rope_angles(pos, period, width)
    return [jnp.cos(ang), jnp.sin(ang)]


def _rope_table3(pos, period, width, scale=1.0):
    ang, first = _rope_angles(pos, period, width)
    sin = jnp.sin(ang) * scale
    return [jnp.cos(ang) * scale, jnp.where(first, -sin, 0.0), jnp.where(first, 0.0, sin)]


def _rope_tables(S):
    pos = jnp.arange(S)
    tabs = []
    tabs += _rope_table3(pos, DA_QK_DIM, LANE, DA_QK_DIM ** -0.5 * LOG2E)
    tabs += _rope_table3(pos, DA_QK_DIM, LANE)
    tabs += _rope_table3(pos, HEAD_DIM, LANE, HEAD_DIM ** -0.5 * LOG2E)
    tabs += _rope_table3(pos, HEAD_DIM, LANE)
    cmp_end = jnp.arange(S // CMP_STRIDE) * CMP_STRIDE + CMP_LEN - 1
    return jnp.stack(tabs), jnp.stack(_rope_table(cmp_end, HEAD_DIM, HEAD_DIM))


def _rot_cols(w, period):
    d, n = w.shape
    half = period // 2
    c = w.reshape(d, n // period, period)
    return jnp.concatenate([-c[..., half:], c[..., :half]], axis=-1).reshape(d, n)


def _layout_w_in(w):
    d = w.shape[0]
    pad = jnp.zeros((d, LANE - N_GATES // 2), w.dtype)
    parts = [w[:, :OFF_NG],
             w[:, OFF_NG:OFF_NG + N_GATES // 2], pad,
             w[:, OFF_NG + N_GATES // 2:OFF_NG + N_GATES], pad]
    return jnp.concatenate(parts, axis=1).astype(BF16)


def _overlap_table_t(S, nb):
    nc = (S - CMP_LEN) // CMP_STRIDE + 1
    ns = S // SLC_LEN
    cs = np.arange(nc) * CMP_STRIDE
    bs = np.arange(ns) * SLC_LEN
    ov = np.clip(np.minimum(cs[:, None] + CMP_LEN, bs[None, :] + SLC_LEN) - np.maximum(cs[:, None], bs[None, :]), 0, None)
    out = np.zeros((nb, S // CMP_STRIDE), np.float32)
    out[:ns, :nc] = (ov / CMP_STRIDE).T
    return jnp.asarray(out, BF16)


def _expand_table_t(S, T, nb):
    key_block = np.arange(S).reshape(S // T, T, 1) // SLC_LEN
    return jnp.asarray(key_block == np.arange(nb).reshape(1, 1, nb), BF16)


def kernel(x, c, w_ada, b_ada, norm_g, ffn_w_gate, ffn_w_up, ffn_w_down, w_in, w_out,
           gm_ln_g, gm_w_s, gm_b_s, da_lambda, da_sub_g, nsa_cmp_pe, nsa_cmp_w1, nsa_cmp_w2, final_g):
    B, S, D = x.shape
    depth = w_in.shape[0]
    ns = S // SLC_LEN
    topk = min(SLC_TOPK, ns)
    nb = -(-ns // 16) * 16
    assert S % 512 == 0 and ATT_TILE % CHUNK == 0 and CHUNK % LANE == 0 and ffn_w_gate.shape[-1] % FF_CHUNK == 0

    mod_all = _adaln(c, w_ada, b_ada).reshape(depth, B, N_MOD, D)
    tabs, cmp_tab = _rope_tables(S)
    ovt = _overlap_table_t(S, nb)
    expand_t = _expand_table_t(S, ATT_TILE, nb)

    for l in range(depth):
        mod = mod_all[l]
        lam_init = 0.8 - 0.6 * math.exp(-0.3 * l)
        bf = lambda w: w.astype(BF16)
        x = _ffn(x, mod, norm_g[l, 0], bf(ffn_w_gate[l, 0]), bf(ffn_w_up[l, 0]), bf(ffn_w_down[l, 0]), rows=(0, 1, 2))

        (y_a, dq, dk, dvt, nq, kc, vc, ks, vst, kw, vwt, ngt) = _inproj(
            x, mod, norm_g[l, 1], _layout_w_in(w_in[l]), tabs, gm_ln_g[l], gm_w_s[l], gm_b_s[l])
        y_b = _diff_attn(dq, dk, dvt, da_lambda[l], da_sub_g[l], lam_init)
        kcmp, vcmpt = _compress(
            kc, vc, bf(nsa_cmp_w1[l]), nsa_cmp_pe[l].reshape(2, 1, CMP_LEN * HEAD_DIM),
            bf(nsa_cmp_w2[l]), bf(_rot_cols(nsa_cmp_w2[l, 0], HEAD_DIM)), cmp_tab)
        o_cmpt, sel = _cmp_attn(nq, kcmp, vcmpt, ovt, ns, topk)
        y_c = _nsa_main(nq, ks, vst, kw, vwt, o_cmpt, sel, ngt, expand_t)

        x = _ffn(x, mod, norm_g[l, 2], bf(ffn_w_gate[l, 1]), bf(ffn_w_up[l, 1]), bf(ffn_w_down[l, 1]), rows=(6, 7, 8),
                 mix=(y_a, y_b, y_c, bf(w_out[l])), final_g=final_g if l == depth - 1 else None)
    return x
```

```python
import functools
import math

import numpy as np
import jax
import jax.numpy as jnp
from jax import lax
from jax.experimental import pallas as pl
from jax.experimental.pallas import tpu as pltpu

F32 = jnp.float32
BF16 = jnp.bfloat16

HEAD_DIM = 64
GM_GROUPS = 4
GM_CHUNK = 128
GM_WIDTH = GM_GROUPS * HEAD_DIM
DA_HEADS = 4
DA_QK_DIM = HEAD_DIM // 2
DA_WIDTH = DA_HEADS * HEAD_DIM
NSA_HEADS = 8
NSA_KV_GROUPS = 2
NSA_HG = NSA_HEADS // NSA_KV_GROUPS
NSA_WIDTH = NSA_HEADS * HEAD_DIM
NSA_KV_WIDTH = NSA_KV_GROUPS * HEAD_DIM
N_GATES = 3 * NSA_HEADS
GATE_ROWS = 16
CMP_LEN = 32
CMP_STRIDE = 16
SLC_LEN = 64
SLC_TOPK = 16
WIN = 512
FORCE_BONUS = 1000.0
ROPE_THETA = 10000.0
EPS = 1e-6
N_MOD = 9
NEG = -1e30
LOG2E = math.log2(math.e)

LANE = 128
SUBLANE = 8
FF_CHUNK = 256
ATT_TILE = 256
CHUNK = 256
VMEM_LIMIT = 56 * 1024 * 1024

OFF_GU, OFF_GV, OFF_DQ, OFF_DK, OFF_DV = 0, 256, 512, 768, 1024
OFF_NQ, OFF_KC, OFF_VC, OFF_KS, OFF_VS, OFF_KW, OFF_VW, OFF_NG = 1280, 1792, 1920, 2048, 2176, 2304, 2432, 2560
OFF_G0 = 2560
OFF_G1 = OFF_G0 + LANE
W_CAT = OFF_G1 + LANE


def _params(sem):
    return pltpu.CompilerParams(dimension_semantics=sem, vmem_limit_bytes=VMEM_LIMIT)


def _resident(shape, index_map):
    return pl.BlockSpec(shape, index_map, pipeline_mode=pl.Buffered(1))


def _dot(a, b):
    return jnp.dot(a, b, preferred_element_type=F32)


def _dot_nt(a, b):
    return lax.dot_general(a, b, (((1,), (1,)), ((), ())), preferred_element_type=F32)


def _mod_norm(x, g, shift, scale):
    ms = jnp.mean(x * x, axis=-1, keepdims=True)
    return x * lax.rsqrt(ms + EPS) * g * (1.0 + scale) + shift


def _gelu(x):
    c = math.sqrt(2.0 / math.pi)
    return 0.5 * x * (1.0 + jnp.tanh(c * (x + 0.044715 * (x * x * x))))


def _silu(x):
    return x * jax.nn.sigmoid(x)


def _adaln_kernel(c_ref, w_ref, b_ref, o_ref):
    ca = _silu(c_ref[...])
    o_ref[...] = _dot(ca, w_ref[...]) + b_ref[...]


def _adaln(c, w_ada, b_ada):
    L, D, ND = w_ada.shape
    B = c.shape[0]
    tn = ND // 8
    return pl.pallas_call(
        _adaln_kernel,
        grid=(L, ND // tn),
        in_specs=[
            pl.BlockSpec((B, D), lambda l, j: (0, 0)),
            pl.BlockSpec((None, D, tn), lambda l, j: (l, 0, j)),
            pl.BlockSpec((None, 1, tn), lambda l, j: (l, 0, j)),
        ],
        out_specs=pl.BlockSpec((None, B, tn), lambda l, j: (l, 0, j)),
        out_shape=jax.ShapeDtypeStruct((L, B, ND), F32),
        compiler_params=_params(("arbitrary", "arbitrary")),
        name="adaln",
    )(c, w_ada, b_ada.reshape(L, 1, ND))


def _ffn_kernel(*refs, rows, nchunk, mixed, final):
    refs = list(refs)
    x_ref, mod_ref, g_ref, wg_ref, wu_ref, wd_ref = refs[:6]
    rest = refs[6:]
    if mixed:
        ya_ref, yb_ref, yc_ref, wo_ref = rest[:4]
        rest = rest[4:]
    if final:
        fg_ref = rest[0]
        rest = rest[1:]
    o_ref, h_ref, a_ref = rest[:3]
    r_sh, r_sc, r_gt = rows
    if mixed:
        x_sc = rest[3]
        y = _dot(ya_ref[...], wo_ref[0:GM_WIDTH, :])
        y += _dot(yb_ref[...], wo_ref[GM_WIDTH:GM_WIDTH + DA_WIDTH, :])
        y += _dot(yc_ref[...], wo_ref[GM_WIDTH + DA_WIDTH:, :])
        x_sc[...] = x_ref[...] + mod_ref[5:6, :] * y
        x_ref = x_sc
    h_ref[...] = _mod_norm(x_ref[...], g_ref[...], mod_ref[r_sh:r_sh + 1, :], mod_ref[r_sc:r_sc + 1, :]).astype(BF16)
    for j in range(nchunk):
        cols = slice(j * FF_CHUNK, (j + 1) * FF_CHUNK)
        gate = _dot(h_ref[...], wg_ref[:, cols])
        up = _dot(h_ref[...], wu_ref[:, cols])
        a_ref[:, cols] = (_silu(gate) * up).astype(BF16)
    y = x_ref[...] + 0.5 * mod_ref[r_gt:r_gt + 1, :] * _dot(a_ref[...], wd_ref[...])
    if final:
        y = y * lax.rsqrt(jnp.mean(y * y, axis=-1, keepdims=True) + EPS) * fg_ref[...]
    o_ref[...] = y


def _ffn(x, mod, g, wg, wu, wd, which, rows, mix=None, final_g=None, tm=512):
    B, S, D = x.shape
    F = wd.shape[-2]
    mixed = mix is not None
    final = final_g is not None
    tok = lambda w: pl.BlockSpec((None, tm, w), lambda b, i: (b, i, 0))
    row = pl.BlockSpec((1, D), lambda b, i: (0, 0))
    stacked = lambda w: _resident((None, None) + w.shape[2:], lambda b, i: which + (0, 0))
    in_specs = [tok(D), pl.BlockSpec((None, N_MOD, D), lambda b, i: (b, 0, 0)), row,
                stacked(wg), stacked(wu), stacked(wd)]
    args = [x, mod, g.reshape(1, D), wg, wu, wd]
    scratch = [pltpu.VMEM((tm, D), BF16), pltpu.VMEM((tm, F), BF16)]
    if mixed:
        ya, yb, yc, w_out = mix
        in_specs += [tok(ya.shape[-1]), tok(yb.shape[-1]), tok(yc.shape[-1]),
                     _resident((None,) + w_out.shape[1:], lambda b, i: (which[0], 0, 0))]
        args += [ya, yb, yc, w_out]
        scratch.append(pltpu.VMEM((tm, D), F32))
    if final:
        in_specs.append(row)
        args.append(final_g.reshape(1, D))
    return pl.pallas_call(
        functools.partial(_ffn_kernel, rows=rows, nchunk=F // FF_CHUNK, mixed=mixed, final=final),
        grid=(B, S // tm),
        in_specs=in_specs,
        out_specs=tok(D),
        out_shape=jax.ShapeDtypeStruct((B, S, D), F32),
        scratch_shapes=scratch,
        compiler_params=_params(("arbitrary", "arbitrary")),
        name="ffn" + ("_mix" if mixed else "") + ("_final" if final else ""),
    )(*args)


def _inproj_kernel(x_ref, mod_ref, g_ref, w_ref, tab_ref, lng_ref, ws_ref, bs_ref, avg_ref,
                   ya_ref, dq_ref, dk_ref, dvt_ref, nq_ref,
                   kc_ref, vc_ref, ks_ref, vst_ref, kw_ref, vwt_ref, ngt_ref, h_ref):
    h_ref[...] = _mod_norm(x_ref[...], g_ref[...], mod_ref[3:4, :], mod_ref[4:5, :]).astype(BF16)
    tm = h_ref.shape[0]

    def mm(off, width):
        return _dot(h_ref[...], w_ref[:, off:off + width])

    def heads(ref, z):
        for hd in range(z.shape[1] // HEAD_DIM):
            ref[hd] = z[:, hd * HEAD_DIM:(hd + 1) * HEAD_DIM].astype(ref.dtype)

    def heads_t(ref, z):
        for c in range(z.shape[1] // LANE):
            zt = z[:, c * LANE:(c + 1) * LANE].T
            ref[2 * c] = zt[:HEAD_DIM].astype(ref.dtype)
            ref[2 * c + 1] = zt[HEAD_DIM:].astype(ref.dtype)

    def rope(z, half, t):
        blocks = []
        for c in range(0, z.shape[1], LANE):
            zb = z[:, c:c + LANE]
            blocks.append(zb * tab_ref[t] + pltpu.roll(zb, LANE - half, 1) * tab_ref[t + 1]
                          + pltpu.roll(zb, half, 1) * tab_ref[t + 2])
        return blocks[0] if len(blocks) == 1 else jnp.concatenate(blocks, axis=1)

    avg = avg_ref[...]

    def group_mean(t):
        hi = t.astype(BF16)
        lo = (t - hi.astype(F32)).astype(BF16)
        return _dot(hi, avg) + _dot(lo, avg)

    u_raw = mm(OFF_GU, GM_WIDTH)
    v_raw = mm(OFF_GV, GM_WIDTH)
    heads(dq_ref, rope(mm(OFF_DQ, 256), DA_QK_DIM // 2, 0))
    v = _gelu(v_raw)
    d = v - group_mean(v)
    heads(dk_ref, rope(mm(OFF_DK, 256), DA_QK_DIM // 2, 3))
    var = group_mean(d * d)
    heads_t(dvt_ref, mm(OFF_DV, DA_WIDTH))
    vn = (d * lax.rsqrt(var + EPS) * lng_ref[...]).astype(BF16)
    heads(nq_ref, rope(mm(OFF_NQ, NSA_WIDTH), HEAD_DIM // 2, 6))

    u = _gelu(u_raw)
    r = lax.broadcasted_iota(jnp.int32, (GM_CHUNK, GM_CHUNK), 0)
    c = lax.broadcasted_iota(jnp.int32, (GM_CHUNK, GM_CHUNK), 1)
    w_sp = [jnp.where(r >= c, ws_ref[g], 0.0).astype(BF16) for g in range(GM_GROUPS)]
    group = lax.broadcasted_iota(jnp.int32, (GM_CHUNK, GM_WIDTH), 1) // HEAD_DIM
    for ch in range(tm // GM_CHUNK):
        t0, t1 = ch * GM_CHUNK, (ch + 1) * GM_CHUNK
        s = bs_ref[...]
        for g in range(GM_GROUPS):
            s = s + jnp.where(group == g, _dot(w_sp[g], vn[t0:t1]), 0.0)
        ya_ref[t0:t1, :] = (u[t0:t1] * s).astype(ya_ref.dtype)

    kv = mm(OFF_KC, 2 * NSA_KV_WIDTH)
    heads(kc_ref, kv[:, :NSA_KV_WIDTH])
    heads(vc_ref, kv[:, NSA_KV_WIDTH:])
    kv = mm(OFF_KS, 2 * NSA_KV_WIDTH)
    heads(ks_ref, rope(kv[:, :NSA_KV_WIDTH], HEAD_DIM // 2, 9))
    heads_t(vst_ref, kv[:, NSA_KV_WIDTH:])
    kv = mm(OFF_KW, 2 * NSA_KV_WIDTH)
    heads(kw_ref, rope(kv[:, :NSA_KV_WIDTH], HEAD_DIM // 2, 9))
    heads_t(vwt_ref, kv[:, NSA_KV_WIDTH:])
    gates = jax.nn.sigmoid(mm(OFF_G0, 2 * LANE))
    ngt_ref[0] = gates[:, :LANE].T[:GATE_ROWS]
    ngt_ref[1] = gates[:, LANE:].T[:GATE_ROWS]


def _inproj(x, mod, g, wcat, tabs, ln_g, w_s, b_s):
    B, S, D = x.shape
    tm = ATT_TILE
    nt = S // tm

    def hm(nh):
        return (jax.ShapeDtypeStruct((B, nh, S, HEAD_DIM), BF16),
                pl.BlockSpec((None, nh, tm, HEAD_DIM), lambda b, i: (b, 0, i, 0)))

    def hm_t(nh):
        return (jax.ShapeDtypeStruct((B, nh, nt, HEAD_DIM, tm), BF16),
                pl.BlockSpec((None, nh, None, HEAD_DIM, tm), lambda b, i: (b, 0, i, 0, 0)))

    G = NSA_KV_GROUPS
    outs = [(jax.ShapeDtypeStruct((B, S, GM_WIDTH), BF16), pl.BlockSpec((None, tm, GM_WIDTH), lambda b, i: (b, i, 0))),
            hm(DA_HEADS), hm(DA_HEADS), hm_t(DA_HEADS), hm(NSA_HEADS),
            hm(G), hm(G), hm(G), hm_t(G), hm(G), hm_t(G),
            (jax.ShapeDtypeStruct((B, G, GATE_ROWS, S), F32),
             pl.BlockSpec((None, G, GATE_ROWS, tm), lambda b, i: (b, 0, 0, i)))]
    bias = jnp.repeat(b_s.T, HEAD_DIM, axis=1)
    lane_group = np.arange(GM_WIDTH) // HEAD_DIM
    avg = jnp.asarray((lane_group[:, None] == lane_group[None, :]) / HEAD_DIM, BF16)
    full = lambda a: pl.BlockSpec(a.shape, lambda b, i: (0,) * a.ndim)
    return pl.pallas_call(
        _inproj_kernel,
        grid=(B, nt),
        in_specs=[
            pl.BlockSpec((None, tm, D), lambda b, i: (b, i, 0)),
            pl.BlockSpec((None, N_MOD, D), lambda b, i: (b, 0, 0)),
            pl.BlockSpec((1, D), lambda b, i: (0, 0)),
            _resident(wcat.shape, lambda b, i: (0, 0)),
            pl.BlockSpec((tabs.shape[0], tm, LANE), lambda b, i: (0, i, 0)),
            pl.BlockSpec((1, GM_WIDTH), lambda b, i: (0, 0)), full(w_s), full(bias), full(avg),
        ],
        out_specs=[o[1] for o in outs],
        out_shape=[o[0] for o in outs],
        scratch_shapes=[pltpu.VMEM((tm, D), BF16)],
        compiler_params=_params(("arbitrary", "arbitrary")),
        name="inproj",
    )(x, mod, g.reshape(1, D), wcat, tabs, ln_g.reshape(1, GM_WIDTH), w_s, bias, avg)


AHEAD = 4


def _online_steps(q_ref, steps, m_ref, l_ref, acc_ref):
    n = len(steps)

    def scores(t):
        _, qc, k, _, bias = steps[t]
        s = _dot_nt(k, q_ref[qc * CHUNK:(qc + 1) * CHUNK, :])
        return s if bias is None else s + bias()

    s = [scores(t) if t < AHEAD else None for t in range(n)]
    for t in range(n):
        c, _, _, vt, _ = steps[t]
        m_old = m_ref[c]
        m_new = jnp.maximum(m_old, jnp.max(s[t], axis=0, keepdims=True))
        alpha = jnp.exp2(m_old - m_new)
        p = jnp.exp2(s[t] - m_new)
        s[t] = None
        l_ref[c] = alpha * l_ref[c] + jnp.sum(p, axis=0, keepdims=True)
        m_ref[c] = m_new
        if t + AHEAD < n:
            s[t + AHEAD] = scores(t + AHEAD)
        acc_ref[c] = alpha * acc_ref[c] + _dot(vt, p.astype(BF16))


def _normalized(c0, n, l_ref, acc_ref):
    parts = [acc_ref[c] * (1.0 / l_ref[c]) for c in range(c0, c0 + n)]
    return parts[0] if n == 1 else jnp.concatenate(parts, axis=1)


def _online_init(m_ref, l_ref, acc_ref):
    m_ref[...] = jnp.full_like(m_ref, NEG)
    l_ref[...] = jnp.zeros_like(l_ref)
    acc_ref[...] = jnp.zeros_like(acc_ref)


def _causal_bias(T, upper):
    key = lax.broadcasted_iota(jnp.int32, (T, T), 0)
    qry = lax.broadcasted_iota(jnp.int32, (T, T), 1)
    keep = (key > qry) if upper else (key <= qry)
    return jnp.where(keep, 0.0, NEG)


def _da_kernel(q_ref, k_ref, vt_ref, lam_ref, subg_ref, o_ref, q_sc, bias_sc, m_ref, l_ref, acc_ref, *, lam_init):
    i = pl.program_id(2)
    HP, T, Dh = q_ref.shape
    cpt = T // CHUNK
    lf = lam_ref[...]
    lam = (jnp.exp(jnp.sum(lf[0:1] * lf[1:2], axis=-1, keepdims=True))
           - jnp.exp(jnp.sum(lf[2:3] * lf[3:4], axis=-1, keepdims=True)) + lam_init)
    first = lax.broadcasted_iota(jnp.int32, (1, Dh), 1) < DA_QK_DIM
    zero = jnp.zeros((), BF16)
    for hd in range(HP):
        q = q_ref[hd]
        q_sc[(2 * hd) * T:(2 * hd + 1) * T, :] = jnp.where(first, q, zero)
        q_sc[(2 * hd + 1) * T:(2 * hd + 2) * T, :] = jnp.where(first, zero, q)
    bias_sc[...] = _causal_bias(T, upper=False)
    _online_init(m_ref, l_ref, acc_ref)

    def tile(j, masked):
        start = pl.multiple_of(j * T, T)
        steps = []
        for hd in range(HP):
            k = k_ref[hd, pl.ds(start, T), :]
            vt = vt_ref[hd, j]
            for cc in range(2 * cpt):
                lo = (cc % cpt) * CHUNK
                bias = (lambda lo=lo: bias_sc[:, lo:lo + CHUNK]) if masked else None
                c = hd * 2 * cpt + cc
                steps.append((c, c, k, vt, bias))
        return steps

    def run(steps):
        _online_steps(q_sc, steps, m_ref, l_ref, acc_ref)

    def body(jj, carry):
        run(tile(2 * jj, False) + tile(2 * jj + 1, False))
        return carry

    lax.fori_loop(0, i // 2, body, 0)

    @pl.when(i % 2 == 0)
    def _():
        run(tile(i, True))

    @pl.when(i % 2 == 1)
    def _():
        run(tile(i - 1, False) + tile(i, True))

    outs = []
    for hd in range(HP):
        o1 = _normalized(2 * hd * cpt, cpt, l_ref, acc_ref)
        o2 = _normalized((2 * hd + 1) * cpt, cpt, l_ref, acc_ref)
        o = o1 - lam * o2
        o = o * lax.rsqrt(jnp.mean(o * o, axis=0, keepdims=True) + EPS) * subg_ref[...] * (1.0 - lam_init)
        outs.append(o)
    o_ref[...] = jnp.concatenate(outs, axis=0).T.astype(o_ref.dtype)


def _diff_attn(dq, dk, dvt, da_lambda, sub_g, lam_init):
    B, H, S, Dh = dq.shape
    T = ATT_TILE
    HP = H
    rows = HP * 2 * T
    return pl.pallas_call(
        functools.partial(_da_kernel, lam_init=lam_init),
        grid=(B, H // HP, S // T),
        in_specs=[
            pl.BlockSpec((None, HP, T, Dh), lambda b, h, i: (b, h, i, 0)),
            pl.BlockSpec((None, HP, S, Dh), lambda b, h, i: (b, h, 0, 0)),
            pl.BlockSpec((None, HP, S // T, Dh, T), lambda b, h, i: (b, h, 0, 0, 0)),
            pl.BlockSpec(da_lambda.shape, lambda b, h, i: (0, 0)),
            pl.BlockSpec((Dh, T), lambda b, h, i: (0, 0)),
        ],
        out_specs=pl.BlockSpec((None, T, HP * Dh), lambda b, h, i: (b, i, h)),
        out_shape=jax.ShapeDtypeStruct((B, S, H * Dh), BF16),
        scratch_shapes=[pltpu.VMEM((rows, Dh), BF16), pltpu.VMEM((T, T), F32),
                        pltpu.VMEM((rows // CHUNK, 1, CHUNK), F32), pltpu.VMEM((rows // CHUNK, 1, CHUNK), F32),
                        pltpu.VMEM((rows // CHUNK, Dh, CHUNK), F32)],
        compiler_params=_params(("arbitrary", "arbitrary", "arbitrary")),
        name="diff_attn",
    )(dq, dk, dvt, da_lambda, jnp.broadcast_to(sub_g[:, None], (Dh, T)))


def _compress_kernel(kc_ref, vc_ref, w1_ref, pe_ref, w2_ref, w2rot_ref, tab_ref, ko_ref, vo_ref):
    R = kc_ref.shape[0]
    half = w1_ref.shape[1] // 2

    def hidden(x_ref, t):
        xr = x_ref[...]
        top = _dot(xr, w1_ref[t, :half, :])
        bot = _dot(xr, w1_ref[t, half:, :])
        pe_rows = jnp.broadcast_to(pe_ref[t], (SUBLANE, 2 * half)).astype(BF16)
        pe = _dot(pe_rows, w1_ref[t])[0:1]
        return _silu(top + pltpu.roll(bot, R - 1, 0) + pe).astype(BF16)

    ak = hidden(kc_ref, 0)
    ko = _dot(ak, w2_ref[0]) * tab_ref[0] + _dot(ak, w2rot_ref[...]) * tab_ref[1]
    vo = _dot(hidden(vc_ref, 1), w2_ref[1])
    ko_ref[...] = ko.astype(BF16)
    vo_ref[...] = jnp.concatenate([ko, vo], axis=1).T[HEAD_DIM:].astype(BF16)


def _compress(kc, vc, w1, pe, w2, w2rot, tab):
    B, G, S, Dh = kc.shape
    R = S // CMP_STRIDE
    W = CMP_STRIDE * Dh
    blk = pl.BlockSpec((None, None, R, W), lambda b, g: (b, g, 0, 0))
    full = lambda a: pl.BlockSpec(a.shape, lambda b, g: (0,) * a.ndim)
    return pl.pallas_call(
        _compress_kernel,
        grid=(B, G),
        in_specs=[blk, blk, full(w1), full(pe), full(w2), full(w2rot), full(tab)],
        out_specs=[pl.BlockSpec((None, None, R, Dh), lambda b, g: (b, g, 0, 0)),
                   pl.BlockSpec((None, None, Dh, R), lambda b, g: (b, g, 0, 0))],
        out_shape=[jax.ShapeDtypeStruct((B, G, R, Dh), BF16), jax.ShapeDtypeStruct((B, G, Dh, R), BF16)],
        compiler_params=_params(("arbitrary", "arbitrary")),
        name="nsa_compress",
    )(kc.reshape(B, G, R, W), vc.reshape(B, G, R, W), w1, pe, w2, w2rot, tab)


def _cmp_kernel(q_ref, kc_ref, vct_ref, ovt_ref, o_ref, sel_ref, *, ns, topk):
    i = pl.program_id(2)
    Hg, T, Dh = q_ref.shape
    R = kc_ref.shape[0]
    NB = sel_ref.shape[0]
    kc = kc_ref[...]
    vct = vct_ref[...]
    pos = i * T + lax.broadcasted_iota(jnp.int32, (R, T), 1)
    cend = lax.broadcasted_iota(jnp.int32, (R, T), 0) * CMP_STRIDE + (CMP_LEN - 1)
    mask = cend <= pos
    scores = [_dot_nt(kc, q_ref[hd]) for hd in range(Hg)]
    ps = jnp.zeros((R, T), F32)
    for hd in range(Hg):
        s = jnp.where(mask, scores[hd], NEG)
        e = jnp.where(mask, jnp.exp2(s - jnp.max(s, axis=0, keepdims=True)), 0.0)
        den = jnp.sum(e, axis=0, keepdims=True)
        p = e * (1.0 / jnp.where(den > 0.0, den, 1.0))
        o_ref[hd] = _dot(vct, p.astype(BF16))
        ps = ps + p

    p_hi = ps.astype(BF16)
    p_lo = (ps - p_hi.astype(F32)).astype(BF16)
    ovt = ovt_ref[...]
    imp = _dot(ovt, p_hi) + _dot(ovt, p_lo)
    blk = lax.broadcasted_iota(jnp.int32, (NB, T), 0)
    cur = (i * T + lax.broadcasted_iota(jnp.int32, (NB, T), 1)) // SLC_LEN
    valid = blk <= cur
    forced = valid & ((blk == 0) | (blk >= cur - 1))
    score = jnp.where(forced, FORCE_BONUS, jnp.where(valid, imp, -1.0))
    score = jnp.where(blk < ns, score, -2.0)
    rows = [score[j:j + 1, :] for j in range(ns)]
    ranks = []
    for lo in range(0, NB, SUBLANE):
        sc = score[lo:lo + SUBLANE, :]
        blk_g = lo + lax.broadcasted_iota(jnp.int32, (SUBLANE, T), 0)
        rank = jnp.zeros((SUBLANE, T), F32)
        for j in range(ns):
            if j < lo:
                ahead = rows[j] >= sc
            elif j >= lo + SUBLANE:
                ahead = rows[j] > sc
            else:
                ahead = (rows[j] > sc) | ((rows[j] == sc) & (blk_g > j))
            rank = rank + jnp.where(ahead, 1.0, 0.0)
        ranks.append(rank)
    rank = jnp.concatenate(ranks, axis=0)
    sel_ref[...] = jnp.where((rank < topk) & (score >= 0.0), 1.0, 0.0).astype(sel_ref.dtype)


def _cmp_attn(nq, kcmp, vcmpt, ovt, ns, topk):
    B, H, S, Dh = nq.shape
    G = kcmp.shape[1]
    R = kcmp.shape[2]
    NB = ovt.shape[0]
    Hg = H // G
    T = ATT_TILE
    return pl.pallas_call(
        functools.partial(_cmp_kernel, ns=ns, topk=topk),
        grid=(B, G, S // T),
        in_specs=[pl.BlockSpec((None, Hg, T, Dh), lambda b, g, i: (b, g, i, 0)),
                  pl.BlockSpec((None, None, R, Dh), lambda b, g, i: (b, g, 0, 0)),
                  pl.BlockSpec((None, None, Dh, R), lambda b, g, i: (b, g, 0, 0)),
                  pl.BlockSpec(ovt.shape, lambda b, g, i: (0, 0))],
        out_specs=[pl.BlockSpec((None, Hg, Dh, T), lambda b, g, i: (b, g, 0, i)),
                   pl.BlockSpec((None, None, NB, T), lambda b, g, i: (b, g, 0, i))],
        out_shape=[jax.ShapeDtypeStruct((B, H, Dh, S), F32), jax.ShapeDtypeStruct((B, G, NB, S), BF16)],
        compiler_params=_params(("arbitrary", "arbitrary", "arbitrary")),
        name="nsa_cmp",
    )(nq, kcmp, vcmpt, ovt)


def _nsa_kernel(q_ref, ks_ref, vst_ref, kw_ref, vwt_ref, oct_ref, sel_ref, ngt_ref, ext_ref, o_ref,
                q_sc, bias_sc, m_ref, l_ref, acc_ref):
    i = pl.program_id(2)
    Hg, T, Dh = q_ref.shape
    cpt = T // CHUNK
    n = Hg * cpt
    back = WIN // T
    LOWER, UPPER = 2, 3
    q_sc[...] = q_ref[...].reshape(Hg * T, Dh)
    bias_sc[LOWER] = _causal_bias(T, upper=False)
    bias_sc[UPPER] = _causal_bias(T, upper=True)
    _online_init(m_ref, l_ref, acc_ref)
    sel = sel_ref[...]

    def tile(k_ref, vt_ref, j, slot, base):
        k = k_ref[pl.ds(pl.multiple_of(j * T, T), T), :]
        vt = vt_ref[j]
        steps = []
        for c in range(n):
            lo = (c % cpt) * CHUNK
            bias = None if slot is None else (lambda lo=lo: bias_sc[slot, :, lo:lo + CHUNK])
            steps.append((base + c, c, k, vt, bias))
        return steps

    def run(steps):
        _online_steps(q_sc, steps, m_ref, l_ref, acc_ref)

    def slc_tile(slot, j, diagonal):
        chosen = _dot(ext_ref[j], sel) > 0.5
        bias_sc[slot] = jnp.where(chosen, _causal_bias(T, upper=False) if diagonal else 0.0, NEG)
        return tile(ks_ref, vst_ref, j, slot, 0)

    def slc_body(jj, carry):
        run(slc_tile(0, 2 * jj, False) + slc_tile(1, 2 * jj + 1, False))
        return carry

    lax.fori_loop(0, i // 2, slc_body, 0)

    def last_body(odd, n_back):
        steps = (slc_tile(0, i - 1, False) + slc_tile(1, i, True)) if odd else slc_tile(0, i, True)
        steps = steps + tile(kw_ref, vwt_ref, i, LOWER, n)
        for d in range(1, n_back + 1):
            steps = steps + tile(kw_ref, vwt_ref, i - d, UPPER if d == back else None, n)
        run(steps)

    for n_back in range(back):
        @pl.when(i == n_back)
        def _():
            last_body(n_back % 2, n_back)

    for odd in (0, 1):
        @pl.when((i >= back) & (i % 2 == odd))
        def _():
            last_body(odd, back)

    gates = ngt_ref[...]
    outs = []
    for hd in range(Hg):
        o = (gates[3 * hd:3 * hd + 1, :] * oct_ref[hd]
             + gates[3 * hd + 1:3 * hd + 2, :] * _normalized(hd * cpt, cpt, l_ref, acc_ref)
             + gates[3 * hd + 2:3 * hd + 3, :] * _normalized(n + hd * cpt, cpt, l_ref, acc_ref))
        outs.append(o)
    for pr in range(Hg // 2):
        o_ref[:, pr * LANE:(pr + 1) * LANE] = jnp.concatenate(outs[2 * pr:2 * pr + 2], axis=0).T.astype(o_ref.dtype)


def _nsa_main(nq, ks, vst, kw, vwt, o_cmpt, sel, ngt, expand_t):
    B, H, S, Dh = nq.shape
    G = ks.shape[1]
    Hg = H // G
    T = ATT_TILE
    assert WIN % T == 0 and Hg % 2 == 0
    rows = Hg * T
    kspec = pl.BlockSpec((None, None, S, Dh), lambda b, g, i: (b, g, 0, 0))
    vspec = pl.BlockSpec((None, None, S // T, Dh, T), lambda b, g, i: (b, g, 0, 0, 0))
    NB = sel.shape[2]
    return pl.pallas_call(
        _nsa_kernel,
        grid=(B, G, S // T),
        in_specs=[pl.BlockSpec((None, Hg, T, Dh), lambda b, g, i: (b, g, i, 0)),
                  kspec, vspec, kspec, vspec,
                  pl.BlockSpec((None, Hg, Dh, T), lambda b, g, i: (b, g, 0, i)),
                  pl.BlockSpec((None, None, NB, T), lambda b, g, i: (b, g, 0, i)),
                  pl.BlockSpec((None, None, GATE_ROWS, T), lambda b, g, i: (b, g, 0, i)),
                  pl.BlockSpec(expand_t.shape, lambda b, g, i: (0, 0, 0))],
        out_specs=pl.BlockSpec((None, T, Hg * Dh), lambda b, g, i: (b, i, g)),
        out_shape=jax.ShapeDtypeStruct((B, S, H * Dh), BF16),
        scratch_shapes=[pltpu.VMEM((rows, Dh), BF16), pltpu.VMEM((4, T, T), F32),
                        pltpu.VMEM((2 * rows // CHUNK, 1, CHUNK), F32), pltpu.VMEM((2 * rows // CHUNK, 1, CHUNK), F32),
                        pltpu.VMEM((2 * rows // CHUNK, Dh, CHUNK), F32)],
        compiler_params=_params(("arbitrary", "arbitrary", "arbitrary")),
        name="nsa_main",
    )(nq, ks, vst, kw, vwt, o_cmpt, sel, ngt, expand_t)


def _rope_angles(pos, period, width):
    half = period // 2
    local = np.arange(width) % period
    inv = ROPE_THETA ** (-(local % half).astype(np.float32) / half)
    return pos.astype(jnp.float32)[:, None] * jnp.asarray(inv, F32)[None, :], jnp.asarray(local < half)


def _rope_table(pos, period, width):
    ang, _ = _rope_angles(pos, period, width)
    return [jnp.cos(ang), jnp.sin(ang)]


def _rope_table3(pos, period, width, scale=1.0):
    ang, first = _rope_angles(pos, period, width)
    sin = jnp.sin(ang) * scale
    return [jnp.cos(ang) * scale, jnp.where(first, -sin, 0.0), jnp.where(first, 0.0, sin)]


def _rope_tables(S):
    pos = jnp.arange(S)
    tabs = []
    tabs += _rope_table3(pos, DA_QK_DIM, LANE, DA_QK_DIM ** -0.5 * LOG2E)
    tabs += _rope_table3(pos, DA_QK_DIM, LANE)
    tabs += _rope_table3(pos, HEAD_DIM, LANE, HEAD_DIM ** -0.5 * LOG2E)
    tabs += _rope_table3(pos, HEAD_DIM, LANE)
    cmp_end = jnp.arange(S // CMP_STRIDE) * CMP_STRIDE + CMP_LEN - 1
    return jnp.stack(tabs), jnp.stack(_rope_table(cmp_end, HEAD_DIM, HEAD_DIM))


def _rot_cols(w, period):
    d, n = w.shape
    half = period // 2
    c = w.reshape(d, n // period, period)
    return jnp.concatenate([-c[..., half:], c[..., :half]], axis=-1).reshape(d, n)


def _layout_w_in(w):
    d = w.shape[0]
    pad = jnp.zeros((d, LANE - N_GATES // 2), w.dtype)
    parts = [w[:, :OFF_NG],
             w[:, OFF_NG:OFF_NG + N_GATES // 2], pad,
             w[:, OFF_NG + N_GATES // 2:OFF_NG + N_GATES], pad]
    return jnp.concatenate(parts, axis=1).astype(BF16)


def _overlap_table_t(S, nb):
    nc = (S - CMP_LEN) // CMP_STRIDE + 1
    ns = S // SLC_LEN
    cs = np.arange(nc) * CMP_STRIDE
    bs = np.arange(ns) * SLC_LEN
    ov = np.clip(np.minimum(cs[:, None] + CMP_LEN, bs[None, :] + SLC_LEN) - np.maximum(cs[:, None], bs[None, :]), 0, None)
    out = np.zeros((nb, S // CMP_STRIDE), np.float32)
    out[:ns, :nc] = (ov / CMP_STRIDE).T
    return jnp.asarray(out, BF16)


def _expand_table_t(S, T, nb):
    key_block = np.arange(S).reshape(S // T, T, 1) // SLC_LEN
    return jnp.asarray(key_block == np.arange(nb).reshape(1, 1, nb), BF16)


def kernel(x, c, w_ada, b_ada, norm_g, ffn_w_gate, ffn_w_up, ffn_w_down, w_in, w_out,
           gm_ln_g, gm_w_s, gm_b_s, da_lambda, da_sub_g, nsa_cmp_pe, nsa_cmp_w1, nsa_cmp_w2, final_g):
    B, S, D = x.shape
    depth = w_in.shape[0]
    ns = S // SLC_LEN
    topk = min(SLC_TOPK, ns)
    nb = -(-ns // 16) * 16
    assert S % 512 == 0 and ATT_TILE % CHUNK == 0 and CHUNK % LANE == 0 and ffn_w_gate.shape[-1] % FF_CHUNK == 0

    mod_all = _adaln(c, w_ada, b_ada).reshape(depth, B, N_MOD, D)
    tabs, cmp_tab = _rope_tables(S)
    ovt = _overlap_table_t(S, nb)
    expand_t = _expand_table_t(S, ATT_TILE, nb)

    bf = lambda w: w.astype(BF16)
    wg, wu, wd, wo = bf(ffn_w_gate), bf(ffn_w_up), bf(ffn_w_down), bf(w_out)

    for l in range(depth):
        mod = mod_all[l]
        lam_init = 0.8 - 0.6 * math.exp(-0.3 * l)
        x = _ffn(x, mod, norm_g[l, 0], wg, wu, wd, (l, 0), rows=(0, 1, 2))

        (y_a, dq, dk, dvt, nq, kc, vc, ks, vst, kw, vwt, ngt) = _inproj(
            x, mod, norm_g[l, 1], _layout_w_in(w_in[l]), tabs, gm_ln_g[l], gm_w_s[l], gm_b_s[l])
        y_b = _diff_attn(dq, dk, dvt, da_lambda[l], da_sub_g[l], lam_init)
        kcmp, vcmpt = _compress(
            kc, vc, bf(nsa_cmp_w1[l]), nsa_cmp_pe[l].reshape(2, 1, CMP_LEN * HEAD_DIM),
            bf(nsa_cmp_w2[l]), bf(_rot_cols(nsa_cmp_w2[l, 0], HEAD_DIM)), cmp_tab)
        o_cmpt, sel = _cmp_attn(nq, kcmp, vcmpt, ovt, ns, topk)
        y_c = _nsa_main(nq, ks, vst, kw, vwt, o_cmpt, sel, ngt, expand_t)

        x = _ffn(x, mod, norm_g[l, 2], wg, wu, wd, (l, 1), rows=(6, 7, 8),
                 mix=(y_a, y_b, y_c, wo), final_g=final_g if l == depth - 1 else None)
    return x
```

```python
import functools
import math

import numpy as np
import jax
import jax.numpy as jnp
from jax import lax
from jax.experimental import pallas as pl
from jax.experimental.pallas import tpu as pltpu

F32 = jnp.float32
BF16 = jnp.bfloat16

HEAD_DIM = 64
GM_GROUPS = 4
GM_CHUNK = 128
GM_WIDTH = GM_GROUPS * HEAD_DIM
DA_HEADS = 4
DA_QK_DIM = HEAD_DIM // 2
DA_WIDTH = DA_HEADS * HEAD_DIM
NSA_HEADS = 8
NSA_KV_GROUPS = 2
NSA_HG = NSA_HEADS // NSA_KV_GROUPS
NSA_WIDTH = NSA_HEADS * HEAD_DIM
NSA_KV_WIDTH = NSA_KV_GROUPS * HEAD_DIM
N_GATES = 3 * NSA_HEADS
GATE_ROWS = 16
CMP_LEN = 32
CMP_STRIDE = 16
SLC_LEN = 64
SLC_TOPK = 16
WIN = 512
FORCE_BONUS = 1000.0
ROPE_THETA = 10000.0
EPS = 1e-6
N_MOD = 9
NEG = -1e30
LOG2E = math.log2(math.e)

LANE = 128
SUBLANE = 8
FF_CHUNK = 256
ATT_TILE = 256
CHUNK = 256
VMEM_LIMIT = 56 * 1024 * 1024

OFF_GU, OFF_GV, OFF_DQ, OFF_DK, OFF_DV = 0, 256, 512, 768, 1024
OFF_NQ, OFF_KC, OFF_VC, OFF_KS, OFF_VS, OFF_KW, OFF_VW, OFF_NG = 1280, 1792, 1920, 2048, 2176, 2304, 2432, 2560
OFF_G0 = 2560
OFF_G1 = OFF_G0 + LANE
W_CAT = OFF_G1 + LANE


def _params(sem):
    return pltpu.CompilerParams(dimension_semantics=sem, vmem_limit_bytes=VMEM_LIMIT)


def _resident(shape, index_map):
    return pl.BlockSpec(shape, index_map, pipeline_mode=pl.Buffered(1))


def _dot(a, b):
    return jnp.dot(a, b, preferred_element_type=F32)


def _dot_nt(a, b):
    return lax.dot_general(a, b, (((1,), (1,)), ((), ())), preferred_element_type=F32)


def _mod_norm(x, g, shift, scale):
    ms = jnp.mean(x * x, axis=-1, keepdims=True)
    return x * lax.rsqrt(ms + EPS) * g * (1.0 + scale) + shift


def _gelu(x):
    c = math.sqrt(2.0 / math.pi)
    return 0.5 * x * (1.0 + jnp.tanh(c * (x + 0.044715 * (x * x * x))))


def _silu(x):
    return x * jax.nn.sigmoid(x)


def _adaln_kernel(c_ref, w_ref, b_ref, o_ref):
    ca = _silu(c_ref[...])
    o_ref[...] = _dot(ca, w_ref[...]) + b_ref[...]


def _adaln(c, w_ada, b_ada):
    L, D, ND = w_ada.shape
    B = c.shape[0]
    tn = ND // 8
    return pl.pallas_call(
        _adaln_kernel,
        grid=(L, ND // tn),
        in_specs=[
            pl.BlockSpec((B, D), lambda l, j: (0, 0)),
            pl.BlockSpec((None, D, tn), lambda l, j: (l, 0, j)),
            pl.BlockSpec((None, 1, tn), lambda l, j: (l, 0, j)),
        ],
        out_specs=pl.BlockSpec((None, B, tn), lambda l, j: (l, 0, j)),
        out_shape=jax.ShapeDtypeStruct((L, B, ND), F32),
        compiler_params=_params(("arbitrary", "arbitrary")),
        name="adaln",
    )(c, w_ada, b_ada.reshape(L, 1, ND))


def _ffn_kernel(*refs, rows, nchunk, mixed, final):
    refs = list(refs)
    x_ref, mod_ref, g_ref, wg_ref, wu_ref, wd_ref = refs[:6]
    rest = refs[6:]
    if mixed:
        ya_ref, yb_ref, yc_ref, wo_ref = rest[:4]
        rest = rest[4:]
    if final:
        fg_ref = rest[0]
        rest = rest[1:]
    o_ref, h_ref, a_ref = rest[:3]
    r_sh, r_sc, r_gt = rows
    if mixed:
        x_sc = rest[3]
        y = _dot(ya_ref[...], wo_ref[0:GM_WIDTH, :])
        y += _dot(yb_ref[...], wo_ref[GM_WIDTH:GM_WIDTH + DA_WIDTH, :])
        y += _dot(yc_ref[...], wo_ref[GM_WIDTH + DA_WIDTH:, :])
        x_sc[...] = x_ref[...] + mod_ref[5:6, :] * y
        x_ref = x_sc
    h_ref[...] = _mod_norm(x_ref[...], g_ref[...], mod_ref[r_sh:r_sh + 1, :], mod_ref[r_sc:r_sc + 1, :]).astype(BF16)
    for j in range(nchunk):
        cols = slice(j * FF_CHUNK, (j + 1) * FF_CHUNK)
        gate = _dot(h_ref[...], wg_ref[:, cols])
        up = _dot(h_ref[...], wu_ref[:, cols])
        a_ref[:, cols] = (_silu(gate) * up).astype(BF16)
    y = x_ref[...] + 0.5 * mod_ref[r_gt:r_gt + 1, :] * _dot(a_ref[...], wd_ref[...])
    if final:
        y = y * lax.rsqrt(jnp.mean(y * y, axis=-1, keepdims=True) + EPS) * fg_ref[...]
    o_ref[...] = y


def _ffn(x, mod, g, wg, wu, wd, which, rows, mix=None, final_g=None, tm=512):
    B, S, D = x.shape
    F = wd.shape[-2]
    mixed = mix is not None
    final = final_g is not None
    tok = lambda w: pl.BlockSpec((None, tm, w), lambda b, i: (b, i, 0))
    row = pl.BlockSpec((1, D), lambda b, i: (0, 0))
    stacked = lambda w: _resident((None, None) + w.shape[2:], lambda b, i: which + (0, 0))
    in_specs = [tok(D), pl.BlockSpec((None, N_MOD, D), lambda b, i: (b, 0, 0)), row,
                stacked(wg), stacked(wu), stacked(wd)]
    args = [x, mod, g.reshape(1, D), wg, wu, wd]
    scratch = [pltpu.VMEM((tm, D), BF16), pltpu.VMEM((tm, F), BF16)]
    if mixed:
        ya, yb, yc, w_out = mix
        in_specs += [tok(ya.shape[-1]), tok(yb.shape[-1]), tok(yc.shape[-1]),
                     _resident((None,) + w_out.shape[1:], lambda b, i: (which[0], 0, 0))]
        args += [ya, yb, yc, w_out]
        scratch.append(pltpu.VMEM((tm, D), F32))
    if final:
        in_specs.append(row)
        args.append(final_g.reshape(1, D))
    return pl.pallas_call(
        functools.partial(_ffn_kernel, rows=rows, nchunk=F // FF_CHUNK, mixed=mixed, final=final),
        grid=(B, S // tm),
        in_specs=in_specs,
        out_specs=tok(D),
        out_shape=jax.ShapeDtypeStruct((B, S, D), F32),
        scratch_shapes=scratch,
        compiler_params=_params(("arbitrary", "arbitrary")),
        name="ffn" + ("_mix" if mixed else "") + ("_final" if final else ""),
    )(*args)


def _inproj_kernel(x_ref, mod_ref, g_ref, w_ref, tab_ref, lng_ref, ws_ref, bs_ref, avg_ref,
                   ya_ref, dq_ref, dk_ref, dvt_ref, nq_ref,
                   kc_ref, vc_ref, ks_ref, vst_ref, kw_ref, vwt_ref, ngt_ref, h_ref):
    h_ref[...] = _mod_norm(x_ref[...], g_ref[...], mod_ref[3:4, :], mod_ref[4:5, :]).astype(BF16)
    tm = h_ref.shape[0]

    def mm(off, width):
        return _dot(h_ref[...], w_ref[:, off:off + width])

    def heads(ref, z):
        for hd in range(z.shape[1] // HEAD_DIM):
            ref[hd] = z[:, hd * HEAD_DIM:(hd + 1) * HEAD_DIM].astype(ref.dtype)

    def heads_t(ref, z):
        for c in range(z.shape[1] // LANE):
            zt = z[:, c * LANE:(c + 1) * LANE].T
            ref[2 * c] = zt[:HEAD_DIM].astype(ref.dtype)
            ref[2 * c + 1] = zt[HEAD_DIM:].astype(ref.dtype)

    def rope(z, half, t):
        blocks = []
        for c in range(0, z.shape[1], LANE):
            zb = z[:, c:c + LANE]
            blocks.append(zb * tab_ref[t] + pltpu.roll(zb, LANE - half, 1) * tab_ref[t + 1]
                          + pltpu.roll(zb, half, 1) * tab_ref[t + 2])
        return blocks[0] if len(blocks) == 1 else jnp.concatenate(blocks, axis=1)

    avg = avg_ref[...]

    def group_mean(t):
        hi = t.astype(BF16)
        lo = (t - hi.astype(F32)).astype(BF16)
        return _dot(hi, avg) + _dot(lo, avg)

    u_raw = mm(OFF_GU, GM_WIDTH)
    v_raw = mm(OFF_GV, GM_WIDTH)
    heads(dq_ref, rope(mm(OFF_DQ, 256), DA_QK_DIM // 2, 0))
    v = _gelu(v_raw)
    d = v - group_mean(v)
    heads(dk_ref, rope(mm(OFF_DK, 256), DA_QK_DIM // 2, 3))
    var = group_mean(d * d)
    heads_t(dvt_ref, mm(OFF_DV, DA_WIDTH))
    vn = (d * lax.rsqrt(var + EPS) * lng_ref[...]).astype(BF16)
    heads(nq_ref, rope(mm(OFF_NQ, NSA_WIDTH), HEAD_DIM // 2, 6))

    u = _gelu(u_raw)
    r = lax.broadcasted_iota(jnp.int32, (GM_CHUNK, GM_CHUNK), 0)
    c = lax.broadcasted_iota(jnp.int32, (GM_CHUNK, GM_CHUNK), 1)
    w_sp = [jnp.where(r >= c, ws_ref[g], 0.0).astype(BF16) for g in range(GM_GROUPS)]
    group = lax.broadcasted_iota(jnp.int32, (GM_CHUNK, GM_WIDTH), 1) // HEAD_DIM
    for ch in range(tm // GM_CHUNK):
        t0, t1 = ch * GM_CHUNK, (ch + 1) * GM_CHUNK
        s = bs_ref[...]
        for g in range(GM_GROUPS):
            s = s + jnp.where(group == g, _dot(w_sp[g], vn[t0:t1]), 0.0)
        ya_ref[t0:t1, :] = (u[t0:t1] * s).astype(ya_ref.dtype)

    kv = mm(OFF_KC, 2 * NSA_KV_WIDTH)
    heads(kc_ref, kv[:, :NSA_KV_WIDTH])
    heads(vc_ref, kv[:, NSA_KV_WIDTH:])
    kv = mm(OFF_KS, 2 * NSA_KV_WIDTH)
    heads(ks_ref, rope(kv[:, :NSA_KV_WIDTH], HEAD_DIM // 2, 9))
    heads_t(vst_ref, kv[:, NSA_KV_WIDTH:])
    kv = mm(OFF_KW, 2 * NSA_KV_WIDTH)
    heads(kw_ref, rope(kv[:, :NSA_KV_WIDTH], HEAD_DIM // 2, 9))
    heads_t(vwt_ref, kv[:, NSA_KV_WIDTH:])
    gates = jax.nn.sigmoid(mm(OFF_G0, 2 * LANE))
    ngt_ref[0] = gates[:, :LANE].T[:GATE_ROWS]
    ngt_ref[1] = gates[:, LANE:].T[:GATE_ROWS]


def _inproj(x, mod, g, wcat, tabs, ln_g, w_s, b_s):
    B, S, D = x.shape
    tm = ATT_TILE
    nt = S // tm

    def hm(nh):
        return (jax.ShapeDtypeStruct((B, nh, S, HEAD_DIM), BF16),
                pl.BlockSpec((None, nh, tm, HEAD_DIM), lambda b, i: (b, 0, i, 0)))

    def hm_t(nh):
        return (jax.ShapeDtypeStruct((B, nh, nt, HEAD_DIM, tm), BF16),
                pl.BlockSpec((None, nh, None, HEAD_DIM, tm), lambda b, i: (b, 0, i, 0, 0)))

    G = NSA_KV_GROUPS
    outs = [(jax.ShapeDtypeStruct((B, S, GM_WIDTH), BF16), pl.BlockSpec((None, tm, GM_WIDTH), lambda b, i: (b, i, 0))),
            hm(DA_HEADS), hm(DA_HEADS), hm_t(DA_HEADS), hm(NSA_HEADS),
            hm(G), hm(G), hm(G), hm_t(G), hm(G), hm_t(G),
            (jax.ShapeDtypeStruct((B, G, GATE_ROWS, S), F32),
             pl.BlockSpec((None, G, GATE_ROWS, tm), lambda b, i: (b, 0, 0, i)))]
    bias = jnp.repeat(b_s.T, HEAD_DIM, axis=1)
    lane_group = np.arange(GM_WIDTH) // HEAD_DIM
    avg = jnp.asarray((lane_group[:, None] == lane_group[None, :]) / HEAD_DIM, BF16)
    full = lambda a: pl.BlockSpec(a.shape, lambda b, i: (0,) * a.ndim)
    return pl.pallas_call(
        _inproj_kernel,
        grid=(B, nt),
        in_specs=[
            pl.BlockSpec((None, tm, D), lambda b, i: (b, i, 0)),
            pl.BlockSpec((None, N_MOD, D), lambda b, i: (b, 0, 0)),
            pl.BlockSpec((1, D), lambda b, i: (0, 0)),
            _resident(wcat.shape, lambda b, i: (0, 0)),
            pl.BlockSpec((tabs.shape[0], tm, LANE), lambda b, i: (0, i, 0)),
            pl.BlockSpec((1, GM_WIDTH), lambda b, i: (0, 0)), full(w_s), full(bias), full(avg),
        ],
        out_specs=[o[1] for o in outs],
        out_shape=[o[0] for o in outs],
        scratch_shapes=[pltpu.VMEM((tm, D), BF16)],
        compiler_params=_params(("arbitrary", "arbitrary")),
        name="inproj",
    )(x, mod, g.reshape(1, D), wcat, tabs, ln_g.reshape(1, GM_WIDTH), w_s, bias, avg)


AHEAD = 4


SUM_ROWS = 16


def _online_steps(q_ref, steps, m_ref, acc_ref):
    n = len(steps)
    keys = steps[0][3].shape[1]
    ones = jnp.where(lax.broadcasted_iota(jnp.int32, (SUM_ROWS, keys), 0) == 0, 1.0, 0.0).astype(BF16)

    def scores(t):
        _, qc, k, _, bias = steps[t]
        s = _dot_nt(k, q_ref[qc * CHUNK:(qc + 1) * CHUNK, :])
        return s if bias is None else s + bias()

    s = [scores(t) if t < AHEAD else None for t in range(n)]
    for t in range(n):
        c, _, _, vt, _ = steps[t]
        m_old = m_ref[c]
        m_new = jnp.maximum(m_old, jnp.max(s[t], axis=0, keepdims=True))
        alpha = jnp.exp2(m_old - m_new)
        p = jnp.exp2(s[t] - m_new).astype(BF16)
        s[t] = None
        m_ref[c] = m_new
        if t + AHEAD < n:
            s[t + AHEAD] = scores(t + AHEAD)
        acc_ref[c] = alpha * acc_ref[c] + _dot(jnp.concatenate([vt, ones], axis=0), p)


def _normalized(c0, n, acc_ref):
    dv = acc_ref.shape[1] - SUM_ROWS
    parts = [acc_ref[c, :dv] * (1.0 / acc_ref[c, dv:dv + 1]) for c in range(c0, c0 + n)]
    return parts[0] if n == 1 else jnp.concatenate(parts, axis=1)


def _online_init(m_ref, acc_ref):
    m_ref[...] = jnp.full_like(m_ref, NEG)
    acc_ref[...] = jnp.zeros_like(acc_ref)


def _causal_bias(T, upper):
    key = lax.broadcasted_iota(jnp.int32, (T, T), 0)
    qry = lax.broadcasted_iota(jnp.int32, (T, T), 1)
    keep = (key > qry) if upper else (key <= qry)
    return jnp.where(keep, 0.0, NEG)


def _da_kernel(q_ref, k_ref, vt_ref, lam_ref, subg_ref, o_ref, q_sc, bias_sc, m_ref, acc_ref, *, lam_init):
    i = pl.program_id(2)
    HP, T, Dh = q_ref.shape
    cpt = T // CHUNK
    lf = lam_ref[...]
    lam = (jnp.exp(jnp.sum(lf[0:1] * lf[1:2], axis=-1, keepdims=True))
           - jnp.exp(jnp.sum(lf[2:3] * lf[3:4], axis=-1, keepdims=True)) + lam_init)
    first = lax.broadcasted_iota(jnp.int32, (1, Dh), 1) < DA_QK_DIM
    zero = jnp.zeros((), BF16)
    for hd in range(HP):
        q = q_ref[hd]
        q_sc[(2 * hd) * T:(2 * hd + 1) * T, :] = jnp.where(first, q, zero)
        q_sc[(2 * hd + 1) * T:(2 * hd + 2) * T, :] = jnp.where(first, zero, q)
    bias_sc[...] = _causal_bias(T, upper=False)
    _online_init(m_ref, acc_ref)

    def tile(j, masked):
        start = pl.multiple_of(j * T, T)
        steps = []
        for hd in range(HP):
            k = k_ref[hd, pl.ds(start, T), :]
            vt = vt_ref[hd, j]
            for cc in range(2 * cpt):
                lo = (cc % cpt) * CHUNK
                bias = (lambda lo=lo: bias_sc[:, lo:lo + CHUNK]) if masked else None
                c = hd * 2 * cpt + cc
                steps.append((c, c, k, vt, bias))
        return steps

    def run(steps):
        _online_steps(q_sc, steps, m_ref, acc_ref)

    def body(jj, carry):
        run(tile(2 * jj, False) + tile(2 * jj + 1, False))
        return carry

    lax.fori_loop(0, i // 2, body, 0)

    @pl.when(i % 2 == 0)
    def _():
        run(tile(i, True))

    @pl.when(i % 2 == 1)
    def _():
        run(tile(i - 1, False) + tile(i, True))

    outs = []
    for hd in range(HP):
        o1 = _normalized(2 * hd * cpt, cpt, acc_ref)
        o2 = _normalized((2 * hd + 1) * cpt, cpt, acc_ref)
        o = o1 - lam * o2
        o = o * lax.rsqrt(jnp.mean(o * o, axis=0, keepdims=True) + EPS) * subg_ref[...] * (1.0 - lam_init)
        outs.append(o)
    o_ref[...] = jnp.concatenate(outs, axis=0).T.astype(o_ref.dtype)


def _diff_attn(dq, dk, dvt, da_lambda, sub_g, lam_init):
    B, H, S, Dh = dq.shape
    T = ATT_TILE
    HP = H
    rows = HP * 2 * T
    return pl.pallas_call(
        functools.partial(_da_kernel, lam_init=lam_init),
        grid=(B, H // HP, S // T),
        in_specs=[
            pl.BlockSpec((None, HP, T, Dh), lambda b, h, i: (b, h, i, 0)),
            pl.BlockSpec((None, HP, S, Dh), lambda b, h, i: (b, h, 0, 0)),
            pl.BlockSpec((None, HP, S // T, Dh, T), lambda b, h, i: (b, h, 0, 0, 0)),
            pl.BlockSpec(da_lambda.shape, lambda b, h, i: (0, 0)),
            pl.BlockSpec((Dh, T), lambda b, h, i: (0, 0)),
        ],
        out_specs=pl.BlockSpec((None, T, HP * Dh), lambda b, h, i: (b, i, h)),
        out_shape=jax.ShapeDtypeStruct((B, S, H * Dh), BF16),
        scratch_shapes=[pltpu.VMEM((rows, Dh), BF16), pltpu.VMEM((T, T), F32),
                        pltpu.VMEM((rows // CHUNK, 1, CHUNK), F32),
                        pltpu.VMEM((rows // CHUNK, Dh + SUM_ROWS, CHUNK), F32)],
        compiler_params=_params(("arbitrary", "arbitrary", "arbitrary")),
        name="diff_attn",
    )(dq, dk, dvt, da_lambda, jnp.broadcast_to(sub_g[:, None], (Dh, T)))


def _compress_kernel(kc_ref, vc_ref, w1_ref, pe_ref, w2_ref, w2rot_ref, tab_ref, ko_ref, vo_ref):
    R = kc_ref.shape[0]
    half = w1_ref.shape[1] // 2

    def hidden(x_ref, t):
        xr = x_ref[...]
        top = _dot(xr, w1_ref[t, :half, :])
        bot = _dot(xr, w1_ref[t, half:, :])
        pe_rows = jnp.broadcast_to(pe_ref[t], (SUBLANE, 2 * half)).astype(BF16)
        pe = _dot(pe_rows, w1_ref[t])[0:1]
        return _silu(top + pltpu.roll(bot, R - 1, 0) + pe).astype(BF16)

    ak = hidden(kc_ref, 0)
    ko = _dot(ak, w2_ref[0]) * tab_ref[0] + _dot(ak, w2rot_ref[...]) * tab_ref[1]
    vo = _dot(hidden(vc_ref, 1), w2_ref[1])
    ko_ref[...] = ko.astype(BF16)
    vo_ref[...] = jnp.concatenate([ko, vo], axis=1).T[HEAD_DIM:].astype(BF16)


def _compress(kc, vc, w1, pe, w2, w2rot, tab):
    B, G, S, Dh = kc.shape
    R = S // CMP_STRIDE
    W = CMP_STRIDE * Dh
    blk = pl.BlockSpec((None, None, R, W), lambda b, g: (b, g, 0, 0))
    full = lambda a: pl.BlockSpec(a.shape, lambda b, g: (0,) * a.ndim)
    return pl.pallas_call(
        _compress_kernel,
        grid=(B, G),
        in_specs=[blk, blk, full(w1), full(pe), full(w2), full(w2rot), full(tab)],
        out_specs=[pl.BlockSpec((None, None, R, Dh), lambda b, g: (b, g, 0, 0)),
                   pl.BlockSpec((None, None, Dh, R), lambda b, g: (b, g, 0, 0))],
        out_shape=[jax.ShapeDtypeStruct((B, G, R, Dh), BF16), jax.ShapeDtypeStruct((B, G, Dh, R), BF16)],
        compiler_params=_params(("arbitrary", "arbitrary")),
        name="nsa_compress",
    )(kc.reshape(B, G, R, W), vc.reshape(B, G, R, W), w1, pe, w2, w2rot, tab)


def _cmp_kernel(q_ref, kc_ref, vct_ref, ovt_ref, o_ref, sel_ref, *, ns, topk):
    i = pl.program_id(1)
    H, T, Dh = q_ref.shape
    G, R, _ = kc_ref.shape
    Hg = H // G
    NB = sel_ref.shape[1]
    pos = i * T + lax.broadcasted_iota(jnp.int32, (R, T), 1)
    cend = lax.broadcasted_iota(jnp.int32, (R, T), 0) * CMP_STRIDE + (CMP_LEN - 1)
    mask = cend <= pos
    scores = [_dot_nt(kc_ref[hd // Hg], q_ref[hd]) for hd in range(H)]
    blk = lax.broadcasted_iota(jnp.int32, (NB, T), 0)
    cur = (i * T + lax.broadcasted_iota(jnp.int32, (NB, T), 1)) // SLC_LEN
    valid = blk <= cur
    forced = valid & ((blk == 0) | (blk >= cur - 1))
    ovt = ovt_ref[...]

    for g in range(G):
        ps = jnp.zeros((R, T), F32)
        for hd in range(g * Hg, (g + 1) * Hg):
            s = jnp.where(mask, scores[hd], NEG)
            e = jnp.where(mask, jnp.exp2(s - jnp.max(s, axis=0, keepdims=True)), 0.0)
            den = jnp.sum(e, axis=0, keepdims=True)
            p = e * (1.0 / jnp.where(den > 0.0, den, 1.0))
            o_ref[hd] = _dot(vct_ref[g], p.astype(BF16))
            ps = ps + p

        p_hi = ps.astype(BF16)
        p_lo = (ps - p_hi.astype(F32)).astype(BF16)
        imp = _dot(ovt, p_hi) + _dot(ovt, p_lo)
        score = jnp.where(forced, FORCE_BONUS, jnp.where(valid, imp, -1.0))
        score = jnp.where(blk < ns, score, -2.0)
        rows = [score[j:j + 1, :] for j in range(ns)]
        ranks = []
        for lo in range(0, NB, SUBLANE):
            sc = score[lo:lo + SUBLANE, :]
            blk_g = lo + lax.broadcasted_iota(jnp.int32, (SUBLANE, T), 0)
            rank = jnp.zeros((SUBLANE, T), F32)
            for j in range(ns):
                if j < lo:
                    ahead = rows[j] >= sc
                elif j >= lo + SUBLANE:
                    ahead = rows[j] > sc
                else:
                    ahead = (rows[j] > sc) | ((rows[j] == sc) & (blk_g > j))
                rank = rank + jnp.where(ahead, 1.0, 0.0)
            ranks.append(rank)
        rank = jnp.concatenate(ranks, axis=0)
        sel_ref[g] = jnp.where((rank < topk) & (score >= 0.0), 1.0, 0.0).astype(sel_ref.dtype)


def _cmp_attn(nq, kcmp, vcmpt, ovt, ns, topk):
    B, H, S, Dh = nq.shape
    G = kcmp.shape[1]
    R = kcmp.shape[2]
    NB = ovt.shape[0]
    T = ATT_TILE
    return pl.pallas_call(
        functools.partial(_cmp_kernel, ns=ns, topk=topk),
        grid=(B, S // T),
        in_specs=[pl.BlockSpec((None, H, T, Dh), lambda b, i: (b, 0, i, 0)),
                  pl.BlockSpec((None, G, R, Dh), lambda b, i: (b, 0, 0, 0)),
                  pl.BlockSpec((None, G, Dh, R), lambda b, i: (b, 0, 0, 0)),
                  pl.BlockSpec(ovt.shape, lambda b, i: (0, 0))],
        out_specs=[pl.BlockSpec((None, H, Dh, T), lambda b, i: (b, 0, 0, i)),
                   pl.BlockSpec((None, G, NB, T), lambda b, i: (b, 0, 0, i))],
        out_shape=[jax.ShapeDtypeStruct((B, H, Dh, S), F32), jax.ShapeDtypeStruct((B, G, NB, S), BF16)],
        compiler_params=_params(("arbitrary", "arbitrary")),
        name="nsa_cmp",
    )(nq, kcmp, vcmpt, ovt)


def _nsa_kernel(q_ref, ks_ref, vst_ref, kw_ref, vwt_ref, oct_ref, sel_ref, ngt_ref, ext_ref, o_ref,
                q_sc, bias_sc, m_ref, acc_ref):
    i = pl.program_id(1)
    H, T, Dh = q_ref.shape
    G = ks_ref.shape[0]
    Hg = H // G
    cpt = T // CHUNK
    n = Hg * cpt
    back = WIN // T
    LOWER, UPPER = 2 * G, 2 * G + 1
    q_sc[...] = q_ref[...].reshape(H * T, Dh)
    bias_sc[LOWER] = _causal_bias(T, upper=False)
    bias_sc[UPPER] = _causal_bias(T, upper=True)
    _online_init(m_ref, acc_ref)

    def tile(g, k_ref, vt_ref, j, slot, window):
        k = k_ref[g, pl.ds(pl.multiple_of(j * T, T), T), :]
        vt = vt_ref[g, j]
        steps = []
        for c in range(n):
            lo = (c % cpt) * CHUNK
            bias = None if slot is None else (lambda lo=lo: bias_sc[slot, :, lo:lo + CHUNK])
            steps.append((2 * n * g + (n if window else 0) + c, g * n + c, k, vt, bias))
        return steps

    def run(steps):
        _online_steps(q_sc, steps, m_ref, acc_ref)

    def slc_tile(g, which, j, diagonal):
        slot = 2 * g + which
        chosen = _dot(ext_ref[j], sel_ref[g]) > 0.5
        bias_sc[slot] = jnp.where(chosen, _causal_bias(T, upper=False) if diagonal else 0.0, NEG)
        return tile(g, ks_ref, vst_ref, j, slot, False)

    def both(fn):
        steps = []
        for g in range(G):
            steps = steps + fn(g)
        return steps

    def slc_body(jj, carry):
        run(both(lambda g: slc_tile(g, 0, 2 * jj, False) + slc_tile(g, 1, 2 * jj + 1, False)))
        return carry

    lax.fori_loop(0, i // 2, slc_body, 0)

    def last_steps(g, odd, n_back):
        steps = (slc_tile(g, 0, i - 1, False) + slc_tile(g, 1, i, True)) if odd else slc_tile(g, 0, i, True)
        steps = steps + tile(g, kw_ref, vwt_ref, i, LOWER, True)
        for d in range(1, n_back + 1):
            steps = steps + tile(g, kw_ref, vwt_ref, i - d, UPPER if d == back else None, True)
        return steps

    for n_back in range(back):
        @pl.when(i == n_back)
        def _():
            run(both(lambda g: last_steps(g, n_back % 2, n_back)))

    for odd in (0, 1):
        @pl.when((i >= back) & (i % 2 == odd))
        def _():
            run(both(lambda g: last_steps(g, odd, back)))

    outs = []
    for hd in range(H):
        g, c0 = hd // Hg, 2 * n * (hd // Hg) + (hd % Hg) * cpt
        gates = ngt_ref[g]
        r = 3 * (hd % Hg)
        o = (gates[r:r + 1, :] * oct_ref[hd]
             + gates[r + 1:r + 2, :] * _normalized(c0, cpt, acc_ref)
             + gates[r + 2:r + 3, :] * _normalized(c0 + n, cpt, acc_ref))
        outs.append(o)
    for pr in range(H // 2):
        o_ref[:, pr * LANE:(pr + 1) * LANE] = jnp.concatenate(outs[2 * pr:2 * pr + 2], axis=0).T.astype(o_ref.dtype)


def _nsa_main(nq, ks, vst, kw, vwt, o_cmpt, sel, ngt, expand_t):
    B, H, S, Dh = nq.shape
    G = ks.shape[1]
    T = ATT_TILE
    assert WIN % T == 0 and (H // G) % 2 == 0
    rows = H * T
    kspec = pl.BlockSpec((None, G, S, Dh), lambda b, i: (b, 0, 0, 0))
    vspec = pl.BlockSpec((None, G, S // T, Dh, T), lambda b, i: (b, 0, 0, 0, 0))
    NB = sel.shape[2]
    return pl.pallas_call(
        _nsa_kernel,
        grid=(B, S // T),
        in_specs=[pl.BlockSpec((None, H, T, Dh), lambda b, i: (b, 0, i, 0)),
                  kspec, vspec, kspec, vspec,
                  pl.BlockSpec((None, H, Dh, T), lambda b, i: (b, 0, 0, i)),
                  pl.BlockSpec((None, G, NB, T), lambda b, i: (b, 0, 0, i)),
                  pl.BlockSpec((None, G, GATE_ROWS, T), lambda b, i: (b, 0, 0, i)),
                  pl.BlockSpec(expand_t.shape, lambda b, i: (0, 0, 0))],
        out_specs=pl.BlockSpec((None, T, H * Dh), lambda b, i: (b, i, 0)),
        out_shape=jax.ShapeDtypeStruct((B, S, H * Dh), BF16),
        scratch_shapes=[pltpu.VMEM((rows, Dh), BF16), pltpu.VMEM((2 * G + 2, T, T), F32),
                        pltpu.VMEM((2 * rows // CHUNK, 1, CHUNK), F32),
                        pltpu.VMEM((2 * rows // CHUNK, Dh + SUM_ROWS, CHUNK), F32)],
        compiler_params=_params(("arbitrary", "arbitrary")),
        name="nsa_main",
    )(nq, ks, vst, kw, vwt, o_cmpt, sel, ngt, expand_t)


def _rope_angles(pos, period, width):
    half = period // 2
    local = np.arange(width) % period
    inv = ROPE_THETA ** (-(local % half).astype(np.float32) / half)
    return pos.astype(jnp.float32)[:, None] * jnp.asarray(inv, F32)[None, :], jnp.asarray(local < half)


def _rope_table(pos, period, width):
    ang, _ = _rope_angles(pos, period, width)
    return [jnp.cos(ang), jnp.sin(ang)]


def _rope_table3(pos, period, width, scale=1.0):
    ang, first = _rope_angles(pos, period, width)
    sin = jnp.sin(ang) * scale
    return [jnp.cos(ang) * scale, jnp.where(first, -sin, 0.0), jnp.where(first, 0.0, sin)]


def _rope_tables(S):
    pos = jnp.arange(S)
    tabs = []
    tabs += _rope_table3(pos, DA_QK_DIM, LANE, DA_QK_DIM ** -0.5 * LOG2E)
    tabs += _rope_table3(pos, DA_QK_DIM, LANE)
    tabs += _rope_table3(pos, HEAD_DIM, LANE, HEAD_DIM ** -0.5 * LOG2E)
    tabs += _rope_table3(pos, HEAD_DIM, LANE)
    cmp_end = jnp.arange(S // CMP_STRIDE) * CMP_STRIDE + CMP_LEN - 1
    return jnp.stack(tabs), jnp.stack(_rope_table(cmp_end, HEAD_DIM, HEAD_DIM))


def _rot_cols(w, period):
    d, n = w.shape
    half = period // 2
    c = w.reshape(d, n // period, period)
    return jnp.concatenate([-c[..., half:], c[..., :half]], axis=-1).reshape(d, n)


def _layout_w_in(w):
    d = w.shape[0]
    pad = jnp.zeros((d, LANE - N_GATES // 2), w.dtype)
    parts = [w[:, :OFF_NG],
             w[:, OFF_NG:OFF_NG + N_GATES // 2], pad,
             w[:, OFF_NG + N_GATES // 2:OFF_NG + N_GATES], pad]
    return jnp.concatenate(parts, axis=1).astype(BF16)


def _overlap_table_t(S, nb):
    nc = (S - CMP_LEN) // CMP_STRIDE + 1
    ns = S // SLC_LEN
    cs = np.arange(nc) * CMP_STRIDE
    bs = np.arange(ns) * SLC_LEN
    ov = np.clip(np.minimum(cs[:, None] + CMP_LEN, bs[None, :] + SLC_LEN) - np.maximum(cs[:, None], bs[None, :]), 0, None)
    out = np.zeros((nb, S // CMP_STRIDE), np.float32)
    out[:ns, :nc] = (ov / CMP_STRIDE).T
    return jnp.asarray(out, BF16)


def _expand_table_t(S, T, nb):
    key_block = np.arange(S).reshape(S // T, T, 1) // SLC_LEN
    return jnp.asarray(key_block == np.arange(nb).reshape(1, 1, nb), BF16)


def kernel(x, c, w_ada, b_ada, norm_g, ffn_w_gate, ffn_w_up, ffn_w_down, w_in, w_out,
           gm_ln_g, gm_w_s, gm_b_s, da_lambda, da_sub_g, nsa_cmp_pe, nsa_cmp_w1, nsa_cmp_w2, final_g):
    B, S, D = x.shape
    depth = w_in.shape[0]
    ns = S // SLC_LEN
    topk = min(SLC_TOPK, ns)
    nb = -(-ns // 16) * 16
    assert S % 512 == 0 and ATT_TILE % CHUNK == 0 and CHUNK % LANE == 0 and ffn_w_gate.shape[-1] % FF_CHUNK == 0

    mod_all = _adaln(c, w_ada, b_ada).reshape(depth, B, N_MOD, D)
    tabs, cmp_tab = _rope_tables(S)
    ovt = _overlap_table_t(S, nb)
    expand_t = _expand_table_t(S, ATT_TILE, nb)

    bf = lambda w: w.astype(BF16)
    wg, wu, wd, wo = bf(ffn_w_gate), bf(ffn_w_up), bf(ffn_w_down), bf(w_out)

    for l in range(depth):
        mod = mod_all[l]
        lam_init = 0.8 - 0.6 * math.exp(-0.3 * l)
        x = _ffn(x, mod, norm_g[l, 0], wg, wu, wd, (l, 0), rows=(0, 1, 2))

        (y_a, dq, dk, dvt, nq, kc, vc, ks, vst, kw, vwt, ngt) = _inproj(
            x, mod, norm_g[l, 1], _layout_w_in(w_in[l]), tabs, gm_ln_g[l], gm_w_s[l], gm_b_s[l])
        y_b = _diff_attn(dq, dk, dvt, da_lambda[l], da_sub_g[l], lam_init)
        kcmp, vcmpt = _compress(
            kc, vc, bf(nsa_cmp_w1[l]), nsa_cmp_pe[l].reshape(2, 1, CMP_LEN * HEAD_DIM),
            bf(nsa_cmp_w2[l]), bf(_rot_cols(nsa_cmp_w2[l, 0], HEAD_DIM)), cmp_tab)
        o_cmpt, sel = _cmp_attn(nq, kcmp, vcmpt, ovt, ns, topk)
        y_c = _nsa_main(nq, ks, vst, kw, vwt, o_cmpt, sel, ngt, expand_t)

        x = _ffn(x, mod, norm_g[l, 2], wg, wu, wd, (l, 1), rows=(6, 7, 8),
                 mix=(y_a, y_b, y_c, wo), final_g=final_g if l == depth - 1 else None)
    return x
```

```python
import functools
import math

import numpy as np
import jax
import jax.numpy as jnp
from jax import lax
from jax.experimental import pallas as pl
from jax.experimental.pallas import tpu as pltpu

F32 = jnp.float32
BF16 = jnp.bfloat16

HEAD_DIM = 64
GM_GROUPS = 4
GM_CHUNK = 128
GM_WIDTH = GM_GROUPS * HEAD_DIM
DA_HEADS = 4
DA_QK_DIM = HEAD_DIM // 2
DA_WIDTH = DA_HEADS * HEAD_DIM
NSA_HEADS = 8
NSA_KV_GROUPS = 2
NSA_HG = NSA_HEADS // NSA_KV_GROUPS
NSA_WIDTH = NSA_HEADS * HEAD_DIM
NSA_KV_WIDTH = NSA_KV_GROUPS * HEAD_DIM
N_GATES = 3 * NSA_HEADS
GATE_ROWS = 16
CMP_LEN = 32
CMP_STRIDE = 16
SLC_LEN = 64
SLC_TOPK = 16
WIN = 512
FORCE_BONUS = 1000.0
ROPE_THETA = 10000.0
EPS = 1e-6
N_MOD = 9
NEG = -1e30
LOG2E = math.log2(math.e)

LANE = 128
SUBLANE = 8
FF_CHUNK = 256
ATT_TILE = 256
CHUNK = 256
VMEM_LIMIT = 56 * 1024 * 1024

OFF_GU, OFF_GV, OFF_DQ, OFF_DK, OFF_DV = 0, 256, 512, 768, 1024
OFF_NQ, OFF_KC, OFF_VC, OFF_KS, OFF_VS, OFF_KW, OFF_VW, OFF_NG = 1280, 1792, 1920, 2048, 2176, 2304, 2432, 2560
TOK_GU, TOK_GV, TOK_DQ, TOK_DK, TOK_NQ, TOK_KC, TOK_KS, TOK_WIDTH = 0, 256, 512, 768, 1024, 1536, 1792, 2048
FEAT_DV, FEAT_VS, FEAT_VW, FEAT_NG = 0, 256, 384, 512
FEAT_ROWS = FEAT_NG + NSA_KV_GROUPS * GATE_ROWS


def _params(sem):
    return pltpu.CompilerParams(dimension_semantics=sem, vmem_limit_bytes=VMEM_LIMIT)


def _resident(shape, index_map):
    return pl.BlockSpec(shape, index_map, pipeline_mode=pl.Buffered(1))


def _dot(a, b):
    return jnp.dot(a, b, preferred_element_type=F32)


def _dot_nt(a, b):
    return lax.dot_general(a, b, (((1,), (1,)), ((), ())), preferred_element_type=F32)


def _mod_norm(x, g, shift, scale):
    ms = jnp.mean(x * x, axis=-1, keepdims=True)
    return x * lax.rsqrt(ms + EPS) * g * (1.0 + scale) + shift


def _gelu(x):
    c = math.sqrt(2.0 / math.pi)
    return 0.5 * x * (1.0 + jnp.tanh(c * (x + 0.044715 * (x * x * x))))


def _silu(x):
    return x * jax.nn.sigmoid(x)


def _adaln_kernel(c_ref, w_ref, b_ref, o_ref):
    ca = _silu(c_ref[...])
    o_ref[...] = _dot(ca, w_ref[...]) + b_ref[...]


def _adaln(c, w_ada, b_ada):
    L, D, ND = w_ada.shape
    B = c.shape[0]
    tn = ND // 8
    return pl.pallas_call(
        _adaln_kernel,
        grid=(L, ND // tn),
        in_specs=[
            pl.BlockSpec((B, D), lambda l, j: (0, 0)),
            pl.BlockSpec((None, D, tn), lambda l, j: (l, 0, j)),
            pl.BlockSpec((None, 1, tn), lambda l, j: (l, 0, j)),
        ],
        out_specs=pl.BlockSpec((None, B, tn), lambda l, j: (l, 0, j)),
        out_shape=jax.ShapeDtypeStruct((L, B, ND), F32),
        compiler_params=_params(("arbitrary", "arbitrary")),
        name="adaln",
    )(c, w_ada, b_ada.reshape(L, 1, ND))


def _ffn_kernel(*refs, rows, nchunk, mixed, final):
    refs = list(refs)
    x_ref, mod_ref, g_ref, wg_ref, wu_ref, wd_ref = refs[:6]
    rest = refs[6:]
    if mixed:
        ya_ref, yb_ref, yc_ref, wo_ref = rest[:4]
        rest = rest[4:]
    if final:
        fg_ref = rest[0]
        rest = rest[1:]
    o_ref, h_ref, a_ref = rest[:3]
    r_sh, r_sc, r_gt = rows
    if mixed:
        x_sc = rest[3]
        y = _dot(ya_ref[...], wo_ref[0:GM_WIDTH, :])
        y += _dot(yb_ref[...], wo_ref[GM_WIDTH:GM_WIDTH + DA_WIDTH, :])
        y += _dot(yc_ref[...], wo_ref[GM_WIDTH + DA_WIDTH:, :])
        x_sc[...] = x_ref[...] + mod_ref[5:6, :] * y
        x_ref = x_sc
    h_ref[...] = _mod_norm(x_ref[...], g_ref[...], mod_ref[r_sh:r_sh + 1, :], mod_ref[r_sc:r_sc + 1, :]).astype(BF16)
    for j in range(nchunk):
        cols = slice(j * FF_CHUNK, (j + 1) * FF_CHUNK)
        gate = _dot(h_ref[...], wg_ref[:, cols])
        up = _dot(h_ref[...], wu_ref[:, cols])
        a_ref[:, cols] = (_silu(gate) * up).astype(BF16)
    y = x_ref[...] + 0.5 * mod_ref[r_gt:r_gt + 1, :] * _dot(a_ref[...], wd_ref[...])
    if final:
        y = y * lax.rsqrt(jnp.mean(y * y, axis=-1, keepdims=True) + EPS) * fg_ref[...]
    o_ref[...] = y


def _ffn(x, mod, g, wg, wu, wd, which, rows, mix=None, final_g=None, tm=512):
    B, S, D = x.shape
    F = wd.shape[-2]
    mixed = mix is not None
    final = final_g is not None
    tok = lambda w: pl.BlockSpec((None, tm, w), lambda b, i: (b, i, 0))
    row = pl.BlockSpec((1, D), lambda b, i: (0, 0))
    stacked = lambda w: _resident((None, None) + w.shape[2:], lambda b, i: which + (0, 0))
    in_specs = [tok(D), pl.BlockSpec((None, N_MOD, D), lambda b, i: (b, 0, 0)), row,
                stacked(wg), stacked(wu), stacked(wd)]
    args = [x, mod, g.reshape(1, D), wg, wu, wd]
    scratch = [pltpu.VMEM((tm, D), BF16), pltpu.VMEM((tm, F), BF16)]
    if mixed:
        ya, yb, yc, w_out = mix
        in_specs += [tok(ya.shape[-1]), tok(yb.shape[-1]), tok(yc.shape[-1]),
                     _resident((None,) + w_out.shape[1:], lambda b, i: (which[0], 0, 0))]
        args += [ya, yb, yc, w_out]
        scratch.append(pltpu.VMEM((tm, D), F32))
    if final:
        in_specs.append(row)
        args.append(final_g.reshape(1, D))
    return pl.pallas_call(
        functools.partial(_ffn_kernel, rows=rows, nchunk=F // FF_CHUNK, mixed=mixed, final=final),
        grid=(B, S // tm),
        in_specs=in_specs,
        out_specs=tok(D),
        out_shape=jax.ShapeDtypeStruct((B, S, D), F32),
        scratch_shapes=scratch,
        compiler_params=_params(("arbitrary", "arbitrary")),
        name="ffn" + ("_mix" if mixed else "") + ("_final" if final else ""),
    )(*args)


def _inproj_kernel(x_ref, mod_ref, g_ref, w_ref, wf_ref, tab_ref, lng_ref, ws_ref, bs_ref, avg_ref,
                   ya_ref, dq_ref, dk_ref, dvt_ref, nq_ref,
                   kc_ref, vc_ref, ks_ref, vst_ref, kw_ref, vwt_ref, ngt_ref, h_ref):
    h_ref[...] = _mod_norm(x_ref[...], g_ref[...], mod_ref[3:4, :], mod_ref[4:5, :]).astype(BF16)
    tm = h_ref.shape[0]

    def mm(off, width):
        return _dot(h_ref[...], w_ref[:, off:off + width])

    def heads(ref, z):
        for hd in range(z.shape[1] // HEAD_DIM):
            ref[hd] = z[:, hd * HEAD_DIM:(hd + 1) * HEAD_DIM].astype(ref.dtype)

    def heads_t(ref, zt):
        for hd in range(zt.shape[0] // HEAD_DIM):
            ref[hd] = zt[hd * HEAD_DIM:(hd + 1) * HEAD_DIM].astype(ref.dtype)

    def rope(z, half, t):
        blocks = []
        for c in range(0, z.shape[1], LANE):
            zb = z[:, c:c + LANE]
            blocks.append(zb * tab_ref[t] + pltpu.roll(zb, LANE - half, 1) * tab_ref[t + 1]
                          + pltpu.roll(zb, half, 1) * tab_ref[t + 2])
        return blocks[0] if len(blocks) == 1 else jnp.concatenate(blocks, axis=1)

    avg = avg_ref[...]

    def group_mean(t):
        hi = t.astype(BF16)
        lo = (t - hi.astype(F32)).astype(BF16)
        return _dot(hi, avg) + _dot(lo, avg)

    u_raw = mm(TOK_GU, GM_WIDTH)
    v_raw = mm(TOK_GV, GM_WIDTH)
    heads(dq_ref, rope(mm(TOK_DQ, 256), DA_QK_DIM // 2, 0))
    v = _gelu(v_raw)
    d = v - group_mean(v)
    heads(dk_ref, rope(mm(TOK_DK, 256), DA_QK_DIM // 2, 3))
    var = group_mean(d * d)
    heads(nq_ref, rope(mm(TOK_NQ, NSA_WIDTH), HEAD_DIM // 2, 6))
    vn = (d * lax.rsqrt(var + EPS) * lng_ref[...]).astype(BF16)
    kk = rope(mm(TOK_KS, 2 * NSA_KV_WIDTH), HEAD_DIM // 2, 9)
    heads(ks_ref, kk[:, :NSA_KV_WIDTH])
    heads(kw_ref, kk[:, NSA_KV_WIDTH:])

    u = _gelu(u_raw)
    r = lax.broadcasted_iota(jnp.int32, (GM_CHUNK, GM_CHUNK), 0)
    c = lax.broadcasted_iota(jnp.int32, (GM_CHUNK, GM_CHUNK), 1)
    w_sp = [jnp.where(r >= c, ws_ref[g], 0.0).astype(BF16) for g in range(GM_GROUPS)]
    group = lax.broadcasted_iota(jnp.int32, (GM_CHUNK, GM_WIDTH), 1) // HEAD_DIM
    for ch in range(tm // GM_CHUNK):
        t0, t1 = ch * GM_CHUNK, (ch + 1) * GM_CHUNK
        s = bs_ref[...]
        for g in range(GM_GROUPS):
            s = s + jnp.where(group == g, _dot(w_sp[g], vn[t0:t1]), 0.0)
        ya_ref[t0:t1, :] = (u[t0:t1] * s).astype(ya_ref.dtype)

    zt = _dot_nt(wf_ref[...], h_ref[...])
    heads_t(dvt_ref, zt[FEAT_DV:FEAT_DV + DA_WIDTH])
    heads_t(vst_ref, zt[FEAT_VS:FEAT_VS + NSA_KV_WIDTH])
    heads_t(vwt_ref, zt[FEAT_VW:FEAT_VW + NSA_KV_WIDTH])
    for g in range(NSA_KV_GROUPS):
        ngt_ref[g] = jax.nn.sigmoid(zt[FEAT_NG + g * GATE_ROWS:FEAT_NG + (g + 1) * GATE_ROWS])
    kv = mm(TOK_KC, 2 * NSA_KV_WIDTH)
    heads(kc_ref, kv[:, :NSA_KV_WIDTH])
    heads(vc_ref, kv[:, NSA_KV_WIDTH:])


def _inproj(x, mod, g, w_tok, w_feat, tabs, ln_g, w_s, b_s):
    B, S, D = x.shape
    tm = ATT_TILE
    nt = S // tm

    def hm(nh):
        return (jax.ShapeDtypeStruct((B, nh, S, HEAD_DIM), BF16),
                pl.BlockSpec((None, nh, tm, HEAD_DIM), lambda b, i: (b, 0, i, 0)))

    def hm_t(nh):
        return (jax.ShapeDtypeStruct((B, nh, nt, HEAD_DIM, tm), BF16),
                pl.BlockSpec((None, nh, None, HEAD_DIM, tm), lambda b, i: (b, 0, i, 0, 0)))

    G = NSA_KV_GROUPS
    outs = [(jax.ShapeDtypeStruct((B, S, GM_WIDTH), BF16), pl.BlockSpec((None, tm, GM_WIDTH), lambda b, i: (b, i, 0))),
            hm(DA_HEADS), hm(DA_HEADS), hm_t(DA_HEADS), hm(NSA_HEADS),
            hm(G), hm(G), hm(G), hm_t(G), hm(G), hm_t(G),
            (jax.ShapeDtypeStruct((B, G, GATE_ROWS, S), F32),
             pl.BlockSpec((None, G, GATE_ROWS, tm), lambda b, i: (b, 0, 0, i)))]
    bias = jnp.repeat(b_s.T, HEAD_DIM, axis=1)
    lane_group = np.arange(GM_WIDTH) // HEAD_DIM
    avg = jnp.asarray((lane_group[:, None] == lane_group[None, :]) / HEAD_DIM, BF16)
    full = lambda a: pl.BlockSpec(a.shape, lambda b, i: (0,) * a.ndim)
    return pl.pallas_call(
        _inproj_kernel,
        grid=(B, nt),
        in_specs=[
            pl.BlockSpec((None, tm, D), lambda b, i: (b, i, 0)),
            pl.BlockSpec((None, N_MOD, D), lambda b, i: (b, 0, 0)),
            pl.BlockSpec((1, D), lambda b, i: (0, 0)),
            _resident(w_tok.shape, lambda b, i: (0, 0)),
            _resident(w_feat.shape, lambda b, i: (0, 0)),
            pl.BlockSpec((tabs.shape[0], tm, LANE), lambda b, i: (0, i, 0)),
            pl.BlockSpec((1, GM_WIDTH), lambda b, i: (0, 0)), full(w_s), full(bias), full(avg),
        ],
        out_specs=[o[1] for o in outs],
        out_shape=[o[0] for o in outs],
        scratch_shapes=[pltpu.VMEM((tm, D), BF16)],
        compiler_params=_params(("arbitrary", "arbitrary")),
        name="inproj",
    )(x, mod, g.reshape(1, D), w_tok, w_feat, tabs, ln_g.reshape(1, GM_WIDTH), w_s, bias, avg)


AHEAD = 4


SUM_ROWS = 16


def _online_steps(q_ref, steps, m_ref, acc_ref):
    n = len(steps)
    keys = steps[0][3].shape[1]
    ones = jnp.where(lax.broadcasted_iota(jnp.int32, (SUM_ROWS, keys), 0) == 0, 1.0, 0.0).astype(BF16)

    def scores(t):
        _, qc, k, _, bias = steps[t]
        s = _dot_nt(k, q_ref[qc * CHUNK:(qc + 1) * CHUNK, :])
        return s if bias is None else s + bias()

    s = [scores(t) if t < AHEAD else None for t in range(n)]
    for t in range(n):
        c, _, _, vt, _ = steps[t]
        m_old = m_ref[c]
        m_new = jnp.maximum(m_old, jnp.max(s[t], axis=0, keepdims=True))
        alpha = jnp.exp2(m_old - m_new)
        p = jnp.exp2(s[t] - m_new).astype(BF16)
        s[t] = None
        m_ref[c] = m_new
        if t + AHEAD < n:
            s[t + AHEAD] = scores(t + AHEAD)
        acc_ref[c] = alpha * acc_ref[c] + _dot(jnp.concatenate([vt, ones], axis=0), p)


def _normalized(c0, n, acc_ref):
    dv = acc_ref.shape[1] - SUM_ROWS
    parts = [acc_ref[c, :dv] * (1.0 / acc_ref[c, dv:dv + 1]) for c in range(c0, c0 + n)]
    return parts[0] if n == 1 else jnp.concatenate(parts, axis=1)


def _online_init(m_ref, acc_ref):
    m_ref[...] = jnp.full_like(m_ref, NEG)
    acc_ref[...] = jnp.zeros_like(acc_ref)


def _causal_bias(T, upper):
    key = lax.broadcasted_iota(jnp.int32, (T, T), 0)
    qry = lax.broadcasted_iota(jnp.int32, (T, T), 1)
    keep = (key > qry) if upper else (key <= qry)
    return jnp.where(keep, 0.0, NEG)


def _da_kernel(q_ref, k_ref, vt_ref, lam_ref, subg_ref, o_ref, q_sc, bias_sc, m_ref, acc_ref, *, lam_init):
    i = pl.program_id(2)
    HP, T, Dh = q_ref.shape
    cpt = T // CHUNK
    lf = lam_ref[...]
    lam = (jnp.exp(jnp.sum(lf[0:1] * lf[1:2], axis=-1, keepdims=True))
           - jnp.exp(jnp.sum(lf[2:3] * lf[3:4], axis=-1, keepdims=True)) + lam_init)
    first = lax.broadcasted_iota(jnp.int32, (1, Dh), 1) < DA_QK_DIM
    zero = jnp.zeros((), BF16)
    for hd in range(HP):
        q = q_ref[hd]
        q_sc[(2 * hd) * T:(2 * hd + 1) * T, :] = jnp.where(first, q, zero)
        q_sc[(2 * hd + 1) * T:(2 * hd + 2) * T, :] = jnp.where(first, zero, q)
    @pl.when(i == 0)
    def _():
        bias_sc[...] = _causal_bias(T, upper=False)

    _online_init(m_ref, acc_ref)

    def tile(j, masked):
        start = pl.multiple_of(j * T, T)
        steps = []
        for hd in range(HP):
            k = k_ref[hd, pl.ds(start, T), :]
            vt = vt_ref[hd, j]
            for cc in range(2 * cpt):
                lo = (cc % cpt) * CHUNK
                bias = (lambda lo=lo: bias_sc[:, lo:lo + CHUNK]) if masked else None
                c = hd * 2 * cpt + cc
                steps.append((c, c, k, vt, bias))
        return steps

    def run(steps):
        _online_steps(q_sc, steps, m_ref, acc_ref)

    def body(jj, carry):
        run(tile(2 * jj, False) + tile(2 * jj + 1, False))
        return carry

    lax.fori_loop(0, i // 2, body, 0)

    @pl.when(i % 2 == 0)
    def _():
        run(tile(i, True))

    @pl.when(i % 2 == 1)
    def _():
        run(tile(i - 1, False) + tile(i, True))

    outs = []
    for hd in range(HP):
        o1 = _normalized(2 * hd * cpt, cpt, acc_ref)
        o2 = _normalized((2 * hd + 1) * cpt, cpt, acc_ref)
        o = o1 - lam * o2
        o = o * lax.rsqrt(jnp.mean(o * o, axis=0, keepdims=True) + EPS) * subg_ref[...] * (1.0 - lam_init)
        outs.append(o)
    o_ref[...] = jnp.concatenate(outs, axis=0).T.astype(o_ref.dtype)


def _diff_attn(dq, dk, dvt, da_lambda, sub_g, lam_init):
    B, H, S, Dh = dq.shape
    T = ATT_TILE
    HP = H
    rows = HP * 2 * T
    return pl.pallas_call(
        functools.partial(_da_kernel, lam_init=lam_init),
        grid=(B, H // HP, S // T),
        in_specs=[
            pl.BlockSpec((None, HP, T, Dh), lambda b, h, i: (b, h, i, 0)),
            pl.BlockSpec((None, HP, S, Dh), lambda b, h, i: (b, h, 0, 0)),
            pl.BlockSpec((None, HP, S // T, Dh, T), lambda b, h, i: (b, h, 0, 0, 0)),
            pl.BlockSpec(da_lambda.shape, lambda b, h, i: (0, 0)),
            pl.BlockSpec((Dh, T), lambda b, h, i: (0, 0)),
        ],
        out_specs=pl.BlockSpec((None, T, HP * Dh), lambda b, h, i: (b, i, h)),
        out_shape=jax.ShapeDtypeStruct((B, S, H * Dh), BF16),
        scratch_shapes=[pltpu.VMEM((rows, Dh), BF16), pltpu.VMEM((T, T), F32),
                        pltpu.VMEM((rows // CHUNK, 1, CHUNK), F32),
                        pltpu.VMEM((rows // CHUNK, Dh + SUM_ROWS, CHUNK), F32)],
        compiler_params=_params(("arbitrary", "arbitrary", "arbitrary")),
        name="diff_attn",
    )(dq, dk, dvt, da_lambda, jnp.broadcast_to(sub_g[:, None], (Dh, T)))


def _compress_kernel(kc_ref, vc_ref, w1_ref, pe_ref, w2_ref, w2rot_ref, tab_ref, ko_ref, vo_ref):
    R = kc_ref.shape[0]
    half = w1_ref.shape[1] // 2

    def hidden(x_ref, t):
        xr = x_ref[...]
        top = _dot(xr, w1_ref[t, :half, :])
        bot = _dot(xr, w1_ref[t, half:, :])
        pe_rows = jnp.broadcast_to(pe_ref[t], (SUBLANE, 2 * half)).astype(BF16)
        pe = _dot(pe_rows, w1_ref[t])[0:1]
        return _silu(top + pltpu.roll(bot, R - 1, 0) + pe).astype(BF16)

    ak = hidden(kc_ref, 0)
    ko = _dot(ak, w2_ref[0]) * tab_ref[0] + _dot(ak, w2rot_ref[...]) * tab_ref[1]
    vo = _dot(hidden(vc_ref, 1), w2_ref[1])
    ko_ref[...] = ko.astype(BF16)
    vo_ref[...] = jnp.concatenate([ko, vo], axis=1).T[HEAD_DIM:].astype(BF16)


def _compress(kc, vc, w1, pe, w2, w2rot, tab):
    B, G, S, Dh = kc.shape
    R = S // CMP_STRIDE
    W = CMP_STRIDE * Dh
    blk = pl.BlockSpec((None, None, R, W), lambda b, g: (b, g, 0, 0))
    full = lambda a: pl.BlockSpec(a.shape, lambda b, g: (0,) * a.ndim)
    return pl.pallas_call(
        _compress_kernel,
        grid=(B, G),
        in_specs=[blk, blk, full(w1), full(pe), full(w2), full(w2rot), full(tab)],
        out_specs=[pl.BlockSpec((None, None, R, Dh), lambda b, g: (b, g, 0, 0)),
                   pl.BlockSpec((None, None, Dh, R), lambda b, g: (b, g, 0, 0))],
        out_shape=[jax.ShapeDtypeStruct((B, G, R, Dh), BF16), jax.ShapeDtypeStruct((B, G, Dh, R), BF16)],
        compiler_params=_params(("arbitrary", "arbitrary")),
        name="nsa_compress",
    )(kc.reshape(B, G, R, W), vc.reshape(B, G, R, W), w1, pe, w2, w2rot, tab)


def _cmp_kernel(q_ref, kc_ref, vct_ref, ovt_ref, o_ref, sel_ref, *, ns, topk):
    i = pl.program_id(1)
    H, T, Dh = q_ref.shape
    G, R, _ = kc_ref.shape
    Hg = H // G
    NB = sel_ref.shape[1]
    pos = i * T + lax.broadcasted_iota(jnp.int32, (R, T), 1)
    cend = lax.broadcasted_iota(jnp.int32, (R, T), 0) * CMP_STRIDE + (CMP_LEN - 1)
    mask = cend <= pos
    scores = [_dot_nt(kc_ref[hd // Hg], q_ref[hd]) for hd in range(H)]
    blk = lax.broadcasted_iota(jnp.int32, (NB, T), 0)
    cur = (i * T + lax.broadcasted_iota(jnp.int32, (NB, T), 1)) // SLC_LEN
    valid = blk <= cur
    forced = valid & ((blk == 0) | (blk >= cur - 1))
    ovt = ovt_ref[...]

    for g in range(G):
        ps = jnp.zeros((R, T), F32)
        for hd in range(g * Hg, (g + 1) * Hg):
            s = jnp.where(mask, scores[hd], NEG)
            e = jnp.where(mask, jnp.exp2(s - jnp.max(s, axis=0, keepdims=True)), 0.0)
            den = jnp.sum(e, axis=0, keepdims=True)
            p = e * (1.0 / jnp.where(den > 0.0, den, 1.0))
            o_ref[hd] = _dot(vct_ref[g], p.astype(BF16))
            ps = ps + p

        p_hi = ps.astype(BF16)
        p_lo = (ps - p_hi.astype(F32)).astype(BF16)
        imp = _dot(ovt, p_hi) + _dot(ovt, p_lo)
        score = jnp.where(forced, FORCE_BONUS, jnp.where(valid, imp, -1.0))
        score = jnp.where(blk < ns, score, -2.0)
        rows = [score[j:j + 1, :] for j in range(ns)]
        ranks = []
        for lo in range(0, NB, SUBLANE):
            sc = score[lo:lo + SUBLANE, :]
            blk_g = lo + lax.broadcasted_iota(jnp.int32, (SUBLANE, T), 0)
            rank = jnp.zeros((SUBLANE, T), F32)
            for j in range(ns):
                if j < lo:
                    ahead = rows[j] >= sc
                elif j >= lo + SUBLANE:
                    ahead = rows[j] > sc
                else:
                    ahead = (rows[j] > sc) | ((rows[j] == sc) & (blk_g > j))
                rank = rank + jnp.where(ahead, 1.0, 0.0)
            ranks.append(rank)
        rank = jnp.concatenate(ranks, axis=0)
        sel_ref[g] = jnp.where((rank < topk) & (score >= 0.0), 1.0, 0.0).astype(sel_ref.dtype)


def _cmp_attn(nq, kcmp, vcmpt, ovt, ns, topk):
    B, H, S, Dh = nq.shape
    G = kcmp.shape[1]
    R = kcmp.shape[2]
    NB = ovt.shape[0]
    T = ATT_TILE
    return pl.pallas_call(
        functools.partial(_cmp_kernel, ns=ns, topk=topk),
        grid=(B, S // T),
        in_specs=[pl.BlockSpec((None, H, T, Dh), lambda b, i: (b, 0, i, 0)),
                  pl.BlockSpec((None, G, R, Dh), lambda b, i: (b, 0, 0, 0)),
                  pl.BlockSpec((None, G, Dh, R), lambda b, i: (b, 0, 0, 0)),
                  pl.BlockSpec(ovt.shape, lambda b, i: (0, 0))],
        out_specs=[pl.BlockSpec((None, H, Dh, T), lambda b, i: (b, 0, 0, i)),
                   pl.BlockSpec((None, G, NB, T), lambda b, i: (b, 0, 0, i))],
        out_shape=[jax.ShapeDtypeStruct((B, H, Dh, S), F32), jax.ShapeDtypeStruct((B, G, NB, S), BF16)],
        compiler_params=_params(("arbitrary", "arbitrary")),
        name="nsa_cmp",
    )(nq, kcmp, vcmpt, ovt)


def _nsa_kernel(q_ref, ks_ref, vst_ref, kw_ref, vwt_ref, oct_ref, sel_ref, ngt_ref, ext_ref, o_ref,
                q_sc, bias_sc, m_ref, acc_ref):
    i = pl.program_id(1)
    H, T, Dh = q_ref.shape
    G = ks_ref.shape[0]
    Hg = H // G
    cpt = T // CHUNK
    n = Hg * cpt
    back = WIN // T
    LOWER, UPPER = 2 * G, 2 * G + 1
    q_sc[...] = q_ref[...].reshape(H * T, Dh)

    @pl.when(i == 0)
    def _():
        bias_sc[LOWER] = _causal_bias(T, upper=False)
        bias_sc[UPPER] = _causal_bias(T, upper=True)

    _online_init(m_ref, acc_ref)

    def tile(g, k_ref, vt_ref, j, slot, window):
        k = k_ref[g, pl.ds(pl.multiple_of(j * T, T), T), :]
        vt = vt_ref[g, j]
        steps = []
        for c in range(n):
            lo = (c % cpt) * CHUNK
            bias = None if slot is None else (lambda lo=lo: bias_sc[slot, :, lo:lo + CHUNK])
            steps.append((2 * n * g + (n if window else 0) + c, g * n + c, k, vt, bias))
        return steps

    def run(steps):
        _online_steps(q_sc, steps, m_ref, acc_ref)

    def slc_tile(g, which, j, diagonal):
        slot = 2 * g + which
        chosen = _dot(ext_ref[j], sel_ref[g]) > 0.5
        bias_sc[slot] = jnp.where(chosen, _causal_bias(T, upper=False) if diagonal else 0.0, NEG)
        return tile(g, ks_ref, vst_ref, j, slot, False)

    def both(fn):
        steps = []
        for g in range(G):
            steps = steps + fn(g)
        return steps

    def slc_body(jj, carry):
        run(both(lambda g: slc_tile(g, 0, 2 * jj, False) + slc_tile(g, 1, 2 * jj + 1, False)))
        return carry

    lax.fori_loop(0, i // 2, slc_body, 0)

    def last_steps(g, odd, n_back):
        steps = (slc_tile(g, 0, i - 1, False) + slc_tile(g, 1, i, True)) if odd else slc_tile(g, 0, i, True)
        steps = steps + tile(g, kw_ref, vwt_ref, i, LOWER, True)
        for d in range(1, n_back + 1):
            steps = steps + tile(g, kw_ref, vwt_ref, i - d, UPPER if d == back else None, True)
        return steps

    for n_back in range(back):
        @pl.when(i == n_back)
        def _():
            run(both(lambda g: last_steps(g, n_back % 2, n_back)))

    for odd in (0, 1):
        @pl.when((i >= back) & (i % 2 == odd))
        def _():
            run(both(lambda g: last_steps(g, odd, back)))

    outs = []
    for hd in range(H):
        g, c0 = hd // Hg, 2 * n * (hd // Hg) + (hd % Hg) * cpt
        gates = ngt_ref[g]
        r = 3 * (hd % Hg)
        o = (gates[r:r + 1, :] * oct_ref[hd]
             + gates[r + 1:r + 2, :] * _normalized(c0, cpt, acc_ref)
             + gates[r + 2:r + 3, :] * _normalized(c0 + n, cpt, acc_ref))
        outs.append(o)
    for pr in range(H // 2):
        o_ref[:, pr * LANE:(pr + 1) * LANE] = jnp.concatenate(outs[2 * pr:2 * pr + 2], axis=0).T.astype(o_ref.dtype)


def _nsa_main(nq, ks, vst, kw, vwt, o_cmpt, sel, ngt, expand_t):
    B, H, S, Dh = nq.shape
    G = ks.shape[1]
    T = ATT_TILE
    assert WIN % T == 0 and (H // G) % 2 == 0
    rows = H * T
    kspec = pl.BlockSpec((None, G, S, Dh), lambda b, i: (b, 0, 0, 0))
    vspec = pl.BlockSpec((None, G, S // T, Dh, T), lambda b, i: (b, 0, 0, 0, 0))
    NB = sel.shape[2]
    return pl.pallas_call(
        _nsa_kernel,
        grid=(B, S // T),
        in_specs=[pl.BlockSpec((None, H, T, Dh), lambda b, i: (b, 0, i, 0)),
                  kspec, vspec, kspec, vspec,
                  pl.BlockSpec((None, H, Dh, T), lambda b, i: (b, 0, 0, i)),
                  pl.BlockSpec((None, G, NB, T), lambda b, i: (b, 0, 0, i)),
                  pl.BlockSpec((None, G, GATE_ROWS, T), lambda b, i: (b, 0, 0, i)),
                  pl.BlockSpec(expand_t.shape, lambda b, i: (0, 0, 0))],
        out_specs=pl.BlockSpec((None, T, H * Dh), lambda b, i: (b, i, 0)),
        out_shape=jax.ShapeDtypeStruct((B, S, H * Dh), BF16),
        scratch_shapes=[pltpu.VMEM((rows, Dh), BF16), pltpu.VMEM((2 * G + 2, T, T), F32),
                        pltpu.VMEM((2 * rows // CHUNK, 1, CHUNK), F32),
                        pltpu.VMEM((2 * rows // CHUNK, Dh + SUM_ROWS, CHUNK), F32)],
        compiler_params=_params(("arbitrary", "arbitrary")),
        name="nsa_main",
    )(nq, ks, vst, kw, vwt, o_cmpt, sel, ngt, expand_t)


def _rope_angles(pos, period, width):
    half = period // 2
    local = np.arange(width) % period
    inv = ROPE_THETA ** (-(local % half).astype(np.float32) / half)
    return pos.astype(jnp.float32)[:, None] * jnp.asarray(inv, F32)[None, :], jnp.asarray(local < half)


def _rope_table(pos, period, width):
    ang, _ = _rope_angles(pos, period, width)
    return [jnp.cos(ang), jnp.sin(ang)]


def _rope_table3(pos, period, width, scale=1.0):
    ang, first = _rope_angles(pos, period, width)
    sin = jnp.sin(ang) * scale
    return [jnp.cos(ang) * scale, jnp.where(first, -sin, 0.0), jnp.where(first, 0.0, sin)]


def _rope_tables(S):
    pos = jnp.arange(S)
    tabs = []
    tabs += _rope_table3(pos, DA_QK_DIM, LANE, DA_QK_DIM ** -0.5 * LOG2E)
    tabs += _rope_table3(pos, DA_QK_DIM, LANE)
    tabs += _rope_table3(pos, HEAD_DIM, LANE, HEAD_DIM ** -0.5 * LOG2E)
    tabs += _rope_table3(pos, HEAD_DIM, LANE)
    cmp_end = jnp.arange(S // CMP_STRIDE) * CMP_STRIDE + CMP_LEN - 1
    return jnp.stack(tabs), jnp.stack(_rope_table(cmp_end, HEAD_DIM, HEAD_DIM))


def _rot_cols(w, period):
    d, n = w.shape
    half = period // 2
    c = w.reshape(d, n // period, period)
    return jnp.concatenate([-c[..., half:], c[..., :half]], axis=-1).reshape(d, n)


def _layout_w_in(w):
    d = w.shape[0]
    cols = lambda off, n: w[:, off:off + n]
    kvw = NSA_KV_WIDTH
    w_tok = jnp.concatenate(
        [cols(OFF_GU, GM_WIDTH), cols(OFF_GV, GM_WIDTH), cols(OFF_DQ, 256), cols(OFF_DK, 256), cols(OFF_NQ, NSA_WIDTH),
         cols(OFF_KC, kvw), cols(OFF_VC, kvw), cols(OFF_KS, kvw), cols(OFF_KW, kvw)], axis=1)
    per_group = N_GATES // NSA_KV_GROUPS
    pad = jnp.zeros((d, GATE_ROWS - per_group), w.dtype)
    feat = [cols(OFF_DV, DA_WIDTH), cols(OFF_VS, kvw), cols(OFF_VW, kvw)]
    for g in range(NSA_KV_GROUPS):
        feat += [cols(OFF_NG + g * per_group, per_group), pad]
    return w_tok.astype(BF16), jnp.concatenate(feat, axis=1).T.astype(BF16)


def _overlap_table_t(S, nb):
    nc = (S - CMP_LEN) // CMP_STRIDE + 1
    ns = S // SLC_LEN
    cs = np.arange(nc) * CMP_STRIDE
    bs = np.arange(ns) * SLC_LEN
    ov = np.clip(np.minimum(cs[:, None] + CMP_LEN, bs[None, :] + SLC_LEN) - np.maximum(cs[:, None], bs[None, :]), 0, None)
    out = np.zeros((nb, S // CMP_STRIDE), np.float32)
    out[:ns, :nc] = (ov / CMP_STRIDE).T
    return jnp.asarray(out, BF16)


def _expand_table_t(S, T, nb):
    key_block = np.arange(S).reshape(S // T, T, 1) // SLC_LEN
    return jnp.asarray(key_block == np.arange(nb).reshape(1, 1, nb), BF16)


def kernel(x, c, w_ada, b_ada, norm_g, ffn_w_gate, ffn_w_up, ffn_w_down, w_in, w_out,
           gm_ln_g, gm_w_s, gm_b_s, da_lambda, da_sub_g, nsa_cmp_pe, nsa_cmp_w1, nsa_cmp_w2, final_g):
    B, S, D = x.shape
    depth = w_in.shape[0]
    ns = S // SLC_LEN
    topk = min(SLC_TOPK, ns)
    nb = -(-ns // 16) * 16
    assert S % 512 == 0 and ATT_TILE % CHUNK == 0 and CHUNK % LANE == 0 and ffn_w_gate.shape[-1] % FF_CHUNK == 0

    mod_all = _adaln(c, w_ada, b_ada).reshape(depth, B, N_MOD, D)
    tabs, cmp_tab = _rope_tables(S)
    ovt = _overlap_table_t(S, nb)
    expand_t = _expand_table_t(S, ATT_TILE, nb)

    bf = lambda w: w.astype(BF16)
    wg, wu, wd, wo = bf(ffn_w_gate), bf(ffn_w_up), bf(ffn_w_down), bf(w_out)

    for l in range(depth):
        mod = mod_all[l]
        lam_init = 0.8 - 0.6 * math.exp(-0.3 * l)
        x = _ffn(x, mod, norm_g[l, 0], wg, wu, wd, (l, 0), rows=(0, 1, 2))

        (y_a, dq, dk, dvt, nq, kc, vc, ks, vst, kw, vwt, ngt) = _inproj(
            x, mod, norm_g[l, 1], *_layout_w_in(w_in[l]), tabs, gm_ln_g[l], gm_w_s[l], gm_b_s[l])
        y_b = _diff_attn(dq, dk, dvt, da_lambda[l], da_sub_g[l], lam_init)
        kcmp, vcmpt = _compress(
            kc, vc, bf(nsa_cmp_w1[l]), nsa_cmp_pe[l].reshape(2, 1, CMP_LEN * HEAD_DIM),
            bf(nsa_cmp_w2[l]), bf(_rot_cols(nsa_cmp_w2[l, 0], HEAD_DIM)), cmp_tab)
        o_cmpt, sel = _cmp_attn(nq, kcmp, vcmpt, ovt, ns, topk)
        y_c = _nsa_main(nq, ks, vst, kw, vwt, o_cmpt, sel, ngt, expand_t)

        x = _ffn(x, mod, norm_g[l, 2], wg, wu, wd, (l, 1), rows=(6, 7, 8),
                 mix=(y_a, y_b, y_c, wo), final_g=final_g if l == depth - 1 else None)
    return x
```

```python
import functools
import math

import numpy as np
import jax
import jax.numpy as jnp
from jax import lax
from jax.experimental import pallas as pl
from jax.experimental.pallas import tpu as pltpu

F32 = jnp.float32
BF16 = jnp.bfloat16

HEAD_DIM = 64
GM_GROUPS = 4
GM_CHUNK = 128
GM_WIDTH = GM_GROUPS * HEAD_DIM
DA_HEADS = 4
DA_QK_DIM = HEAD_DIM // 2
DA_WIDTH = DA_HEADS * HEAD_DIM
NSA_HEADS = 8
NSA_KV_GROUPS = 2
NSA_HG = NSA_HEADS // NSA_KV_GROUPS
NSA_WIDTH = NSA_HEADS * HEAD_DIM
NSA_KV_WIDTH = NSA_KV_GROUPS * HEAD_DIM
N_GATES = 3 * NSA_HEADS
GATE_ROWS = 16
CMP_LEN = 32
CMP_STRIDE = 16
SLC_LEN = 64
SLC_TOPK = 16
WIN = 512
FORCE_BONUS = 1000.0
ROPE_THETA = 10000.0
EPS = 1e-6
N_MOD = 9
NEG = -1e30
LOG2E = math.log2(math.e)

LANE = 128
SUBLANE = 8
FF_CHUNK = 256
ATT_TILE = 256
CHUNK = 256
VMEM_LIMIT = 56 * 1024 * 1024

OFF_GU, OFF_GV, OFF_DQ, OFF_DK, OFF_DV = 0, 256, 512, 768, 1024
OFF_NQ, OFF_KC, OFF_VC, OFF_KS, OFF_VS, OFF_KW, OFF_VW, OFF_NG = 1280, 1792, 1920, 2048, 2176, 2304, 2432, 2560
TOK_GU, TOK_GV, TOK_DQ, TOK_DK, TOK_NQ, TOK_KC, TOK_KS, TOK_WIDTH = 0, 256, 512, 768, 1024, 1536, 1792, 2048
FEAT_DV, FEAT_VS, FEAT_VW, FEAT_NG = 0, 256, 384, 512
FEAT_ROWS = FEAT_NG + NSA_KV_GROUPS * GATE_ROWS


def _params(sem):
    return pltpu.CompilerParams(dimension_semantics=sem, vmem_limit_bytes=VMEM_LIMIT)


def _resident(shape, index_map):
    return pl.BlockSpec(shape, index_map, pipeline_mode=pl.Buffered(1))


def _dot(a, b):
    return jnp.dot(a, b, preferred_element_type=F32)


def _dot_nt(a, b):
    return lax.dot_general(a, b, (((1,), (1,)), ((), ())), preferred_element_type=F32)


def _mod_norm(x, g, shift, scale):
    ms = jnp.mean(x * x, axis=-1, keepdims=True)
    return x * lax.rsqrt(ms + EPS) * g * (1.0 + scale) + shift


def _gelu(x):
    c = math.sqrt(2.0 / math.pi)
    return 0.5 * x * (1.0 + jnp.tanh(c * (x + 0.044715 * (x * x * x))))


def _silu(x):
    return x * jax.nn.sigmoid(x)


def _adaln_kernel(c_ref, w_ref, b_ref, o_ref):
    ca = _silu(c_ref[...])
    o_ref[...] = _dot(ca, w_ref[...]) + b_ref[...]


def _adaln(c, w_ada, b_ada):
    L, D, ND = w_ada.shape
    B = c.shape[0]
    tn = ND // 8
    return pl.pallas_call(
        _adaln_kernel,
        grid=(L, ND // tn),
        in_specs=[
            pl.BlockSpec((B, D), lambda l, j: (0, 0)),
            pl.BlockSpec((None, D, tn), lambda l, j: (l, 0, j)),
            pl.BlockSpec((None, 1, tn), lambda l, j: (l, 0, j)),
        ],
        out_specs=pl.BlockSpec((None, B, tn), lambda l, j: (l, 0, j)),
        out_shape=jax.ShapeDtypeStruct((L, B, ND), F32),
        compiler_params=_params(("arbitrary", "arbitrary")),
        name="adaln",
    )(c, w_ada, b_ada.reshape(L, 1, ND))


def _ffn_kernel(*refs, rows, nchunk, mixed, final):
    refs = list(refs)
    x_ref, mod_ref, g_ref, wg_ref, wu_ref, wd_ref = refs[:6]
    rest = refs[6:]
    if mixed:
        ya_ref, yb_ref, yc_ref, wo_ref = rest[:4]
        rest = rest[4:]
    if final:
        fg_ref = rest[0]
        rest = rest[1:]
    o_ref, h_ref, a_ref = rest[:3]
    r_sh, r_sc, r_gt = rows
    if mixed:
        x_sc = rest[3]
        y = _dot(ya_ref[...], wo_ref[0:GM_WIDTH, :])
        y += _dot(yb_ref[...], wo_ref[GM_WIDTH:GM_WIDTH + DA_WIDTH, :])
        y += _dot(yc_ref[...], wo_ref[GM_WIDTH + DA_WIDTH:, :])
        x_sc[...] = x_ref[...] + mod_ref[5:6, :] * y
        x_ref = x_sc
    h_ref[...] = _mod_norm(x_ref[...], g_ref[...], mod_ref[r_sh:r_sh + 1, :], mod_ref[r_sc:r_sc + 1, :]).astype(BF16)
    for j in range(nchunk):
        cols = slice(j * FF_CHUNK, (j + 1) * FF_CHUNK)
        gate = _dot(h_ref[...], wg_ref[:, cols])
        up = _dot(h_ref[...], wu_ref[:, cols])
        a_ref[:, cols] = (_silu(gate) * up).astype(BF16)
    y = x_ref[...] + 0.5 * mod_ref[r_gt:r_gt + 1, :] * _dot(a_ref[...], wd_ref[...])
    if final:
        y = y * lax.rsqrt(jnp.mean(y * y, axis=-1, keepdims=True) + EPS) * fg_ref[...]
    o_ref[...] = y


def _ffn(x, mod, g, wg, wu, wd, which, rows, mix=None, final_g=None, tm=512):
    B, S, D = x.shape
    F = wd.shape[-2]
    mixed = mix is not None
    final = final_g is not None
    tok = lambda w: pl.BlockSpec((None, tm, w), lambda b, i: (b, i, 0))
    row = pl.BlockSpec((1, D), lambda b, i: (0, 0))
    stacked = lambda w: _resident((None, None) + w.shape[2:], lambda b, i: which + (0, 0))
    in_specs = [tok(D), pl.BlockSpec((None, N_MOD, D), lambda b, i: (b, 0, 0)), row,
                stacked(wg), stacked(wu), stacked(wd)]
    args = [x, mod, g.reshape(1, D), wg, wu, wd]
    scratch = [pltpu.VMEM((tm, D), BF16), pltpu.VMEM((tm, F), BF16)]
    if mixed:
        ya, yb, yc, w_out = mix
        in_specs += [tok(ya.shape[-1]), tok(yb.shape[-1]), tok(yc.shape[-1]),
                     _resident((None,) + w_out.shape[1:], lambda b, i: (which[0], 0, 0))]
        args += [ya, yb, yc, w_out]
        scratch.append(pltpu.VMEM((tm, D), F32))
    if final:
        in_specs.append(row)
        args.append(final_g.reshape(1, D))
    return pl.pallas_call(
        functools.partial(_ffn_kernel, rows=rows, nchunk=F // FF_CHUNK, mixed=mixed, final=final),
        grid=(B, S // tm),
        in_specs=in_specs,
        out_specs=tok(D),
        out_shape=jax.ShapeDtypeStruct((B, S, D), F32),
        scratch_shapes=scratch,
        compiler_params=_params(("arbitrary", "arbitrary")),
        name="ffn" + ("_mix" if mixed else "") + ("_final" if final else ""),
    )(*args)


def _inproj_kernel(x_ref, mod_ref, g_ref, w_ref, wf_ref, tab_ref, lng_ref, ws_ref, bs_ref, avg_ref,
                   ya_ref, dq_ref, dk_ref, dvt_ref, nq_ref,
                   kc_ref, vc_ref, ks_ref, vst_ref, kw_ref, vwt_ref, ngt_ref, h_ref):
    h_ref[...] = _mod_norm(x_ref[...], g_ref[...], mod_ref[3:4, :], mod_ref[4:5, :]).astype(BF16)
    tm = h_ref.shape[0]

    def mm(off, width):
        return _dot(h_ref[...], w_ref[:, off:off + width])

    def heads(ref, z):
        for hd in range(z.shape[1] // HEAD_DIM):
            ref[hd] = z[:, hd * HEAD_DIM:(hd + 1) * HEAD_DIM].astype(ref.dtype)

    def heads_t(ref, zt):
        for hd in range(zt.shape[0] // HEAD_DIM):
            ref[hd] = zt[hd * HEAD_DIM:(hd + 1) * HEAD_DIM].astype(ref.dtype)

    def rope(z, half, t, scale=1.0):
        cos = tab_ref[t] * scale
        sin = tab_ref[t + 1] * scale
        first = lax.broadcasted_iota(jnp.int32, (1, LANE), 1) % (2 * half) < half
        sin_lo = jnp.where(first, -sin, 0.0)
        sin_hi = jnp.where(first, 0.0, sin)
        blocks = []
        for c in range(0, z.shape[1], LANE):
            zb = z[:, c:c + LANE]
            blocks.append(zb * cos + pltpu.roll(zb, LANE - half, 1) * sin_lo + pltpu.roll(zb, half, 1) * sin_hi)
        return blocks[0] if len(blocks) == 1 else jnp.concatenate(blocks, axis=1)

    avg = avg_ref[...]

    def group_mean(t):
        hi = t.astype(BF16)
        lo = (t - hi.astype(F32)).astype(BF16)
        return _dot(hi, avg) + _dot(lo, avg)

    u_raw = mm(TOK_GU, GM_WIDTH)
    v_raw = mm(TOK_GV, GM_WIDTH)
    heads(dq_ref, rope(mm(TOK_DQ, 256), DA_QK_DIM // 2, 0, DA_QK_DIM ** -0.5 * LOG2E))
    v = _gelu(v_raw)
    d = v - group_mean(v)
    heads(dk_ref, rope(mm(TOK_DK, 256), DA_QK_DIM // 2, 0))
    var = group_mean(d * d)
    heads(nq_ref, rope(mm(TOK_NQ, NSA_WIDTH), HEAD_DIM // 2, 2, HEAD_DIM ** -0.5 * LOG2E))
    vn = (d * lax.rsqrt(var + EPS) * lng_ref[...]).astype(BF16)
    kk = rope(mm(TOK_KS, 2 * NSA_KV_WIDTH), HEAD_DIM // 2, 2)
    heads(ks_ref, kk[:, :NSA_KV_WIDTH])
    heads(kw_ref, kk[:, NSA_KV_WIDTH:])

    u = _gelu(u_raw)
    r = lax.broadcasted_iota(jnp.int32, (GM_CHUNK, GM_CHUNK), 0)
    c = lax.broadcasted_iota(jnp.int32, (GM_CHUNK, GM_CHUNK), 1)
    w_sp = [jnp.where(r >= c, ws_ref[g], 0.0).astype(BF16) for g in range(GM_GROUPS)]
    group = lax.broadcasted_iota(jnp.int32, (GM_CHUNK, GM_WIDTH), 1) // HEAD_DIM
    for ch in range(tm // GM_CHUNK):
        t0, t1 = ch * GM_CHUNK, (ch + 1) * GM_CHUNK
        s = bs_ref[...]
        for g in range(GM_GROUPS):
            s = s + jnp.where(group == g, _dot(w_sp[g], vn[t0:t1]), 0.0)
        ya_ref[t0:t1, :] = (u[t0:t1] * s).astype(ya_ref.dtype)

    zt = _dot_nt(wf_ref[...], h_ref[...])
    heads_t(dvt_ref, zt[FEAT_DV:FEAT_DV + DA_WIDTH])
    heads_t(vst_ref, zt[FEAT_VS:FEAT_VS + NSA_KV_WIDTH])
    heads_t(vwt_ref, zt[FEAT_VW:FEAT_VW + NSA_KV_WIDTH])
    for g in range(NSA_KV_GROUPS):
        ngt_ref[g] = jax.nn.sigmoid(zt[FEAT_NG + g * GATE_ROWS:FEAT_NG + (g + 1) * GATE_ROWS])
    kv = mm(TOK_KC, 2 * NSA_KV_WIDTH)
    heads(kc_ref, kv[:, :NSA_KV_WIDTH])
    heads(vc_ref, kv[:, NSA_KV_WIDTH:])


def _inproj(x, mod, g, w_tok, w_feat, layer, tabs, ln_g, w_s, b_s):
    B, S, D = x.shape
    tm = ATT_TILE
    nt = S // tm

    def hm(nh):
        return (jax.ShapeDtypeStruct((B, nh, S, HEAD_DIM), BF16),
                pl.BlockSpec((None, nh, tm, HEAD_DIM), lambda b, i: (b, 0, i, 0)))

    def hm_t(nh):
        return (jax.ShapeDtypeStruct((B, nh, nt, HEAD_DIM, tm), BF16),
                pl.BlockSpec((None, nh, None, HEAD_DIM, tm), lambda b, i: (b, 0, i, 0, 0)))

    G = NSA_KV_GROUPS
    outs = [(jax.ShapeDtypeStruct((B, S, GM_WIDTH), BF16), pl.BlockSpec((None, tm, GM_WIDTH), lambda b, i: (b, i, 0))),
            hm(DA_HEADS), hm(DA_HEADS), hm_t(DA_HEADS), hm(NSA_HEADS),
            hm(G), hm(G), hm(G), hm_t(G), hm(G), hm_t(G),
            (jax.ShapeDtypeStruct((B, G, GATE_ROWS, S), F32),
             pl.BlockSpec((None, G, GATE_ROWS, tm), lambda b, i: (b, 0, 0, i)))]
    bias = jnp.repeat(b_s.T, HEAD_DIM, axis=1)
    lane_group = np.arange(GM_WIDTH) // HEAD_DIM
    avg = jnp.asarray((lane_group[:, None] == lane_group[None, :]) / HEAD_DIM, BF16)
    full = lambda a: pl.BlockSpec(a.shape, lambda b, i: (0,) * a.ndim)
    return pl.pallas_call(
        _inproj_kernel,
        grid=(B, nt),
        in_specs=[
            pl.BlockSpec((None, tm, D), lambda b, i: (b, i, 0)),
            pl.BlockSpec((None, N_MOD, D), lambda b, i: (b, 0, 0)),
            pl.BlockSpec((1, D), lambda b, i: (0, 0)),
            _resident((None,) + w_tok.shape[1:], lambda b, i: (layer, 0, 0)),
            _resident((None,) + w_feat.shape[1:], lambda b, i: (layer, 0, 0)),
            pl.BlockSpec((tabs.shape[0], tm, LANE), lambda b, i: (0, i, 0)),
            pl.BlockSpec((1, GM_WIDTH), lambda b, i: (0, 0)), full(w_s), full(bias), full(avg),
        ],
        out_specs=[o[1] for o in outs],
        out_shape=[o[0] for o in outs],
        scratch_shapes=[pltpu.VMEM((tm, D), BF16)],
        compiler_params=_params(("arbitrary", "arbitrary")),
        name="inproj",
    )(x, mod, g.reshape(1, D), w_tok, w_feat, tabs, ln_g.reshape(1, GM_WIDTH), w_s, bias, avg)


AHEAD = 6


SUM_ROWS = 16


def _online_steps(q_ref, steps, m_ref, acc_ref):
    n = len(steps)
    keys = steps[0][3].shape[1]
    ones = jnp.where(lax.broadcasted_iota(jnp.int32, (SUM_ROWS, keys), 0) == 0, 1.0, 0.0).astype(BF16)

    def scores(t):
        _, qc, k, _, bias = steps[t]
        s = _dot_nt(k, q_ref[qc * CHUNK:(qc + 1) * CHUNK, :])
        return s if bias is None else s + bias()

    s = [scores(t) if t < AHEAD else None for t in range(n)]
    for t in range(n):
        c, _, _, vt, _ = steps[t]
        m_old = m_ref[c]
        m_new = jnp.maximum(m_old, jnp.max(s[t], axis=0, keepdims=True))
        alpha = jnp.exp2(m_old - m_new)
        p = jnp.exp2(s[t] - m_new).astype(BF16)
        s[t] = None
        m_ref[c] = m_new
        if t + AHEAD < n:
            s[t + AHEAD] = scores(t + AHEAD)
        acc_ref[c] = alpha * acc_ref[c] + _dot(jnp.concatenate([vt, ones], axis=0), p)


def _normalized(c0, n, acc_ref):
    dv = acc_ref.shape[1] - SUM_ROWS
    parts = [acc_ref[c, :dv] * (1.0 / acc_ref[c, dv:dv + 1]) for c in range(c0, c0 + n)]
    return parts[0] if n == 1 else jnp.concatenate(parts, axis=1)


def _online_init(m_ref, acc_ref):
    m_ref[...] = jnp.full_like(m_ref, NEG)
    acc_ref[...] = jnp.zeros_like(acc_ref)


def _causal_bias(T, upper):
    key = lax.broadcasted_iota(jnp.int32, (T, T), 0)
    qry = lax.broadcasted_iota(jnp.int32, (T, T), 1)
    keep = (key > qry) if upper else (key <= qry)
    return jnp.where(keep, 0.0, NEG)


def _da_kernel(q_ref, k_ref, vt_ref, lam_ref, subg_ref, o_ref, q_sc, bias_sc, m_ref, acc_ref, *, lam_init):
    i = pl.program_id(2)
    HP, T, Dh = q_ref.shape
    cpt = T // CHUNK
    lf = lam_ref[...]
    lam = (jnp.exp(jnp.sum(lf[0:1] * lf[1:2], axis=-1, keepdims=True))
           - jnp.exp(jnp.sum(lf[2:3] * lf[3:4], axis=-1, keepdims=True)) + lam_init)
    first = lax.broadcasted_iota(jnp.int32, (1, Dh), 1) < DA_QK_DIM
    zero = jnp.zeros((), BF16)
    for hd in range(HP):
        q = q_ref[hd]
        q_sc[(2 * hd) * T:(2 * hd + 1) * T, :] = jnp.where(first, q, zero)
        q_sc[(2 * hd + 1) * T:(2 * hd + 2) * T, :] = jnp.where(first, zero, q)
    @pl.when(i == 0)
    def _():
        bias_sc[...] = _causal_bias(T, upper=False)

    _online_init(m_ref, acc_ref)

    def tile(j, masked):
        start = pl.multiple_of(j * T, T)
        steps = []
        for hd in range(HP):
            k = k_ref[hd, pl.ds(start, T), :]
            vt = vt_ref[hd, j]
            for cc in range(2 * cpt):
                lo = (cc % cpt) * CHUNK
                bias = (lambda lo=lo: bias_sc[:, lo:lo + CHUNK]) if masked else None
                c = hd * 2 * cpt + cc
                steps.append((c, c, k, vt, bias))
        return steps

    def run(steps):
        _online_steps(q_sc, steps, m_ref, acc_ref)

    def body(jj, carry):
        run(tile(2 * jj, False) + tile(2 * jj + 1, False))
        return carry

    lax.fori_loop(0, i // 2, body, 0)

    @pl.when(i % 2 == 0)
    def _():
        run(tile(i, True))

    @pl.when(i % 2 == 1)
    def _():
        run(tile(i - 1, False) + tile(i, True))

    outs = []
    for hd in range(HP):
        o1 = _normalized(2 * hd * cpt, cpt, acc_ref)
        o2 = _normalized((2 * hd + 1) * cpt, cpt, acc_ref)
        o = o1 - lam * o2
        o = o * lax.rsqrt(jnp.mean(o * o, axis=0, keepdims=True) + EPS) * subg_ref[...] * (1.0 - lam_init)
        outs.append(o)
    o_ref[...] = jnp.concatenate(outs, axis=0).T.astype(o_ref.dtype)


def _diff_attn(dq, dk, dvt, da_lambda, sub_g, lam_init):
    B, H, S, Dh = dq.shape
    T = ATT_TILE
    HP = H
    rows = HP * 2 * T
    return pl.pallas_call(
        functools.partial(_da_kernel, lam_init=lam_init),
        grid=(B, H // HP, S // T),
        in_specs=[
            pl.BlockSpec((None, HP, T, Dh), lambda b, h, i: (b, h, i, 0)),
            pl.BlockSpec((None, HP, S, Dh), lambda b, h, i: (b, h, 0, 0)),
            pl.BlockSpec((None, HP, S // T, Dh, T), lambda b, h, i: (b, h, 0, 0, 0)),
            pl.BlockSpec(da_lambda.shape, lambda b, h, i: (0, 0)),
            pl.BlockSpec((Dh, T), lambda b, h, i: (0, 0)),
        ],
        out_specs=pl.BlockSpec((None, T, HP * Dh), lambda b, h, i: (b, i, h)),
        out_shape=jax.ShapeDtypeStruct((B, S, H * Dh), BF16),
        scratch_shapes=[pltpu.VMEM((rows, Dh), BF16), pltpu.VMEM((T, T), F32),
                        pltpu.VMEM((rows // CHUNK, 1, CHUNK), F32),
                        pltpu.VMEM((rows // CHUNK, Dh + SUM_ROWS, CHUNK), F32)],
        compiler_params=_params(("arbitrary", "arbitrary", "arbitrary")),
        name="diff_attn",
    )(dq, dk, dvt, da_lambda, jnp.broadcast_to(sub_g[:, None], (Dh, T)))


def _compress_kernel(kc_ref, vc_ref, w1_ref, pe_ref, w2_ref, w2rot_ref, tab_ref, ko_ref, vo_ref):
    rows = kc_ref.shape[0]
    G, _, R = vo_ref.shape
    half = w1_ref.shape[1] // 2

    def hidden(x_ref, t):
        xr = x_ref[...]
        top = _dot(xr, w1_ref[t, :half, :])
        bot = _dot(xr, w1_ref[t, half:, :])
        pe_rows = jnp.broadcast_to(pe_ref[t], (SUBLANE, 2 * half)).astype(BF16)
        pe = _dot(pe_rows, w1_ref[t])[0:1]
        return _silu(top + pltpu.roll(bot, rows - 1, 0) + pe).astype(BF16)

    ak = hidden(kc_ref, 0)
    ko = _dot(ak, w2_ref[0]) * tab_ref[0] + _dot(ak, w2rot_ref[...]) * tab_ref[1]
    vo = _dot(hidden(vc_ref, 1), w2_ref[1])
    ko_ref[...] = ko.astype(BF16)
    vot = jnp.concatenate([ko, vo], axis=1).T[HEAD_DIM:].astype(BF16)
    for g in range(G):
        vo_ref[g] = vot[:, g * R:(g + 1) * R]


def _compress(kc, vc, w1, pe, w2, w2rot, tab):
    B, G, S, Dh = kc.shape
    R = S // CMP_STRIDE
    W = CMP_STRIDE * Dh
    blk = pl.BlockSpec((None, G * R, W), lambda b: (b, 0, 0))
    full = lambda a: pl.BlockSpec(a.shape, lambda b: (0,) * a.ndim)
    tab = jnp.tile(tab, (1, G, 1))
    kcmp, vcmpt = pl.pallas_call(
        _compress_kernel,
        grid=(B,),
        in_specs=[blk, blk, full(w1), full(pe), full(w2), full(w2rot), full(tab)],
        out_specs=[pl.BlockSpec((None, G * R, Dh), lambda b: (b, 0, 0)),
                   pl.BlockSpec((None, G, Dh, R), lambda b: (b, 0, 0, 0))],
        out_shape=[jax.ShapeDtypeStruct((B, G * R, Dh), BF16), jax.ShapeDtypeStruct((B, G, Dh, R), BF16)],
        compiler_params=_params(("arbitrary",)),
        name="nsa_compress",
    )(kc.reshape(B, G * R, W), vc.reshape(B, G * R, W), w1, pe, w2, w2rot, tab)
    return kcmp.reshape(B, G, R, Dh), vcmpt


def _cmp_kernel(q_ref, kc_ref, vct_ref, ovt_ref, o_ref, sel_ref, *, ns, topk):
    i = pl.program_id(1)
    H, T, Dh = q_ref.shape
    G, R, _ = kc_ref.shape
    Hg = H // G
    NB = sel_ref.shape[1]
    pos = i * T + lax.broadcasted_iota(jnp.int32, (R, T), 1)
    cend = lax.broadcasted_iota(jnp.int32, (R, T), 0) * CMP_STRIDE + (CMP_LEN - 1)
    mask = cend <= pos
    scores = [_dot_nt(kc_ref[hd // Hg], q_ref[hd]) for hd in range(H)]
    blk = lax.broadcasted_iota(jnp.int32, (NB, T), 0)
    cur = (i * T + lax.broadcasted_iota(jnp.int32, (NB, T), 1)) // SLC_LEN
    valid = blk <= cur
    forced = valid & ((blk == 0) | (blk >= cur - 1))
    ovt = ovt_ref[...]

    for g in range(G):
        ps = jnp.zeros((R, T), F32)
        for hd in range(g * Hg, (g + 1) * Hg):
            s = jnp.where(mask, scores[hd], NEG)
            e = jnp.where(mask, jnp.exp2(s - jnp.max(s, axis=0, keepdims=True)), 0.0)
            den = jnp.sum(e, axis=0, keepdims=True)
            p = e * (1.0 / jnp.where(den > 0.0, den, 1.0))
            o_ref[hd] = _dot(vct_ref[g], p.astype(BF16))
            ps = ps + p

        p_hi = ps.astype(BF16)
        p_lo = (ps - p_hi.astype(F32)).astype(BF16)
        imp = _dot(ovt, p_hi) + _dot(ovt, p_lo)
        score = jnp.where(forced, FORCE_BONUS, jnp.where(valid, imp, -1.0))
        score = jnp.where(blk < ns, score, -2.0)
        rows = [score[j:j + 1, :] for j in range(ns)]
        ranks = []
        for lo in range(0, NB, SUBLANE):
            sc = score[lo:lo + SUBLANE, :]
            blk_g = lo + lax.broadcasted_iota(jnp.int32, (SUBLANE, T), 0)
            rank = jnp.zeros((SUBLANE, T), F32)
            for j in range(ns):
                if j < lo:
                    ahead = rows[j] >= sc
                elif j >= lo + SUBLANE:
                    ahead = rows[j] > sc
                else:
                    ahead = (rows[j] > sc) | ((rows[j] == sc) & (blk_g > j))
                rank = rank + jnp.where(ahead, 1.0, 0.0)
            ranks.append(rank)
        rank = jnp.concatenate(ranks, axis=0)
        sel_ref[g] = jnp.where((rank < topk) & (score >= 0.0), 1.0, 0.0).astype(sel_ref.dtype)


def _cmp_attn(nq, kcmp, vcmpt, ovt, ns, topk):
    B, H, S, Dh = nq.shape
    G = kcmp.shape[1]
    R = kcmp.shape[2]
    NB = ovt.shape[0]
    T = ATT_TILE
    return pl.pallas_call(
        functools.partial(_cmp_kernel, ns=ns, topk=topk),
        grid=(B, S // T),
        in_specs=[pl.BlockSpec((None, H, T, Dh), lambda b, i: (b, 0, i, 0)),
                  pl.BlockSpec((None, G, R, Dh), lambda b, i: (b, 0, 0, 0)),
                  pl.BlockSpec((None, G, Dh, R), lambda b, i: (b, 0, 0, 0)),
                  pl.BlockSpec(ovt.shape, lambda b, i: (0, 0))],
        out_specs=[pl.BlockSpec((None, H, Dh, T), lambda b, i: (b, 0, 0, i)),
                   pl.BlockSpec((None, G, NB, T), lambda b, i: (b, 0, 0, i))],
        out_shape=[jax.ShapeDtypeStruct((B, H, Dh, S), F32), jax.ShapeDtypeStruct((B, G, NB, S), BF16)],
        compiler_params=_params(("arbitrary", "arbitrary")),
        name="nsa_cmp",
    )(nq, kcmp, vcmpt, ovt)


def _nsa_kernel(q_ref, ks_ref, vst_ref, kw_ref, vwt_ref, oct_ref, sel_ref, ngt_ref, ext_ref, o_ref,
                q_sc, bias_sc, m_ref, acc_ref):
    i = pl.program_id(1)
    H, T, Dh = q_ref.shape
    G = ks_ref.shape[0]
    Hg = H // G
    cpt = T // CHUNK
    n = Hg * cpt
    back = WIN // T
    LOWER, UPPER = 2 * G, 2 * G + 1
    q_sc[...] = q_ref[...].reshape(H * T, Dh)

    @pl.when(i == 0)
    def _():
        bias_sc[LOWER] = _causal_bias(T, upper=False)
        bias_sc[UPPER] = _causal_bias(T, upper=True)

    _online_init(m_ref, acc_ref)

    def tile(g, k_ref, vt_ref, j, slot, window):
        k = k_ref[g, pl.ds(pl.multiple_of(j * T, T), T), :]
        vt = vt_ref[g, j]
        steps = []
        for c in range(n):
            lo = (c % cpt) * CHUNK
            bias = None if slot is None else (lambda lo=lo: bias_sc[slot, :, lo:lo + CHUNK])
            steps.append((2 * n * g + (n if window else 0) + c, g * n + c, k, vt, bias))
        return steps

    def run(steps):
        _online_steps(q_sc, steps, m_ref, acc_ref)

    def slc_tile(g, which, j, diagonal):
        slot = 2 * g + which
        chosen = _dot(ext_ref[j], sel_ref[g]) > 0.5
        bias_sc[slot] = jnp.where(chosen, _causal_bias(T, upper=False) if diagonal else 0.0, NEG)
        return tile(g, ks_ref, vst_ref, j, slot, False)

    def both(fn):
        steps = []
        for g in range(G):
            steps = steps + fn(g)
        return steps

    def slc_body(jj, carry):
        run(both(lambda g: slc_tile(g, 0, 2 * jj, False) + slc_tile(g, 1, 2 * jj + 1, False)))
        return carry

    lax.fori_loop(0, i // 2, slc_body, 0)

    def last_steps(g, odd, n_back):
        steps = (slc_tile(g, 0, i - 1, False) + slc_tile(g, 1, i, True)) if odd else slc_tile(g, 0, i, True)
        steps = steps + tile(g, kw_ref, vwt_ref, i, LOWER, True)
        for d in range(1, n_back + 1):
            steps = steps + tile(g, kw_ref, vwt_ref, i - d, UPPER if d == back else None, True)
        return steps

    for n_back in range(back):
        @pl.when(i == n_back)
        def _():
            run(both(lambda g: last_steps(g, n_back % 2, n_back)))

    for odd in (0, 1):
        @pl.when((i >= back) & (i % 2 == odd))
        def _():
            run(both(lambda g: last_steps(g, odd, back)))

    outs = []
    for hd in range(H):
        g, c0 = hd // Hg, 2 * n * (hd // Hg) + (hd % Hg) * cpt
        gates = ngt_ref[g]
        r = 3 * (hd % Hg)
        o = (gates[r:r + 1, :] * oct_ref[hd]
             + gates[r + 1:r + 2, :] * _normalized(c0, cpt, acc_ref)
             + gates[r + 2:r + 3, :] * _normalized(c0 + n, cpt, acc_ref))
        outs.append(o)
    for pr in range(H // 2):
        o_ref[:, pr * LANE:(pr + 1) * LANE] = jnp.concatenate(outs[2 * pr:2 * pr + 2], axis=0).T.astype(o_ref.dtype)


def _nsa_main(nq, ks, vst, kw, vwt, o_cmpt, sel, ngt, expand_t):
    B, H, S, Dh = nq.shape
    G = ks.shape[1]
    T = ATT_TILE
    assert WIN % T == 0 and (H // G) % 2 == 0
    rows = H * T
    kspec = pl.BlockSpec((None, G, S, Dh), lambda b, i: (b, 0, 0, 0))
    vspec = pl.BlockSpec((None, G, S // T, Dh, T), lambda b, i: (b, 0, 0, 0, 0))
    NB = sel.shape[2]
    return pl.pallas_call(
        _nsa_kernel,
        grid=(B, S // T),
        in_specs=[pl.BlockSpec((None, H, T, Dh), lambda b, i: (b, 0, i, 0)),
                  kspec, vspec, kspec, vspec,
                  pl.BlockSpec((None, H, Dh, T), lambda b, i: (b, 0, 0, i)),
                  pl.BlockSpec((None, G, NB, T), lambda b, i: (b, 0, 0, i)),
                  pl.BlockSpec((None, G, GATE_ROWS, T), lambda b, i: (b, 0, 0, i)),
                  pl.BlockSpec(expand_t.shape, lambda b, i: (0, 0, 0))],
        out_specs=pl.BlockSpec((None, T, H * Dh), lambda b, i: (b, i, 0)),
        out_shape=jax.ShapeDtypeStruct((B, S, H * Dh), BF16),
        scratch_shapes=[pltpu.VMEM((rows, Dh), BF16), pltpu.VMEM((2 * G + 2, T, T), F32),
                        pltpu.VMEM((2 * rows // CHUNK, 1, CHUNK), F32),
                        pltpu.VMEM((2 * rows // CHUNK, Dh + SUM_ROWS, CHUNK), F32)],
        compiler_params=_params(("arbitrary", "arbitrary")),
        name="nsa_main",
    )(nq, ks, vst, kw, vwt, o_cmpt, sel, ngt, expand_t)


def _rope_table(pos, period, width):
    half = period // 2
    local = np.arange(width) % period
    inv = ROPE_THETA ** (-(local % half).astype(np.float32) / half)
    ang = pos.astype(jnp.float32)[:, None] * jnp.asarray(inv, F32)[None, :]
    return [jnp.cos(ang), jnp.sin(ang)]


def _rope_tables(S):
    pos = jnp.arange(S)
    tabs = _rope_table(pos, DA_QK_DIM, LANE) + _rope_table(pos, HEAD_DIM, LANE)
    cmp_end = jnp.arange(S // CMP_STRIDE) * CMP_STRIDE + CMP_LEN - 1
    return jnp.stack(tabs), jnp.stack(_rope_table(cmp_end, HEAD_DIM, HEAD_DIM))


def _rot_cols(w, period):
    d, n = w.shape
    half = period // 2
    c = w.reshape(d, n // period, period)
    return jnp.concatenate([-c[..., half:], c[..., :half]], axis=-1).reshape(d, n)


def _layout_w_in(w):
    depth, d, _ = w.shape
    cols = lambda off, n: w[:, :, off:off + n]
    kvw = NSA_KV_WIDTH
    w_tok = jnp.concatenate(
        [cols(OFF_GU, GM_WIDTH), cols(OFF_GV, GM_WIDTH), cols(OFF_DQ, 256), cols(OFF_DK, 256), cols(OFF_NQ, NSA_WIDTH),
         cols(OFF_KC, kvw), cols(OFF_VC, kvw), cols(OFF_KS, kvw), cols(OFF_KW, kvw)], axis=2)
    per_group = N_GATES // NSA_KV_GROUPS
    pad = jnp.zeros((depth, d, GATE_ROWS - per_group), w.dtype)
    feat = [cols(OFF_DV, DA_WIDTH), cols(OFF_VS, kvw), cols(OFF_VW, kvw)]
    for g in range(NSA_KV_GROUPS):
        feat += [cols(OFF_NG + g * per_group, per_group), pad]
    return w_tok.astype(BF16), jnp.swapaxes(jnp.concatenate(feat, axis=2), 1, 2).astype(BF16)


def _overlap_table_t(S, nb):
    nc = (S - CMP_LEN) // CMP_STRIDE + 1
    ns = S // SLC_LEN
    cs = np.arange(nc) * CMP_STRIDE
    bs = np.arange(ns) * SLC_LEN
    ov = np.clip(np.minimum(cs[:, None] + CMP_LEN, bs[None, :] + SLC_LEN) - np.maximum(cs[:, None], bs[None, :]), 0, None)
    out = np.zeros((nb, S // CMP_STRIDE), np.float32)
    out[:ns, :nc] = (ov / CMP_STRIDE).T
    return jnp.asarray(out, BF16)


def _expand_table_t(S, T, nb):
    key_block = np.arange(S).reshape(S // T, T, 1) // SLC_LEN
    return jnp.asarray(key_block == np.arange(nb).reshape(1, 1, nb), BF16)


def kernel(x, c, w_ada, b_ada, norm_g, ffn_w_gate, ffn_w_up, ffn_w_down, w_in, w_out,
           gm_ln_g, gm_w_s, gm_b_s, da_lambda, da_sub_g, nsa_cmp_pe, nsa_cmp_w1, nsa_cmp_w2, final_g):
    B, S, D = x.shape
    depth = w_in.shape[0]
    ns = S // SLC_LEN
    topk = min(SLC_TOPK, ns)
    nb = -(-ns // 16) * 16
    assert S % 512 == 0 and ATT_TILE % CHUNK == 0 and CHUNK % LANE == 0 and ffn_w_gate.shape[-1] % FF_CHUNK == 0

    mod_all = _adaln(c, w_ada, b_ada).reshape(depth, B, N_MOD, D)
    tabs, cmp_tab = _rope_tables(S)
    ovt = _overlap_table_t(S, nb)
    expand_t = _expand_table_t(S, ATT_TILE, nb)

    bf = lambda w: w.astype(BF16)
    wg, wu, wd, wo = bf(ffn_w_gate), bf(ffn_w_up), bf(ffn_w_down), bf(w_out)
    w_tok, w_feat = _layout_w_in(w_in)

    for l in range(depth):
        mod = mod_all[l]
        lam_init = 0.8 - 0.6 * math.exp(-0.3 * l)
        x = _ffn(x, mod, norm_g[l, 0], wg, wu, wd, (l, 0), rows=(0, 1, 2))

        (y_a, dq, dk, dvt, nq, kc, vc, ks, vst, kw, vwt, ngt) = _inproj(
            x, mod, norm_g[l, 1], w_tok, w_feat, l, tabs, gm_ln_g[l], gm_w_s[l], gm_b_s[l])
        y_b = _diff_attn(dq, dk, dvt, da_lambda[l], da_sub_g[l], lam_init)
        kcmp, vcmpt = _compress(
            kc, vc, bf(nsa_cmp_w1[l]), nsa_cmp_pe[l].reshape(2, 1, CMP_LEN * HEAD_DIM),
            bf(nsa_cmp_w2[l]), bf(_rot_cols(nsa_cmp_w2[l, 0], HEAD_DIM)), cmp_tab)
        o_cmpt, sel = _cmp_attn(nq, kcmp, vcmpt, ovt, ns, topk)
        y_c = _nsa_main(nq, ks, vst, kw, vwt, o_cmpt, sel, ngt, expand_t)

        x = _ffn(x, mod, norm_g[l, 2], wg, wu, wd, (l, 1), rows=(6, 7, 8),
                 mix=(y_a, y_b, y_c, wo), final_g=final_g if l == depth - 1 else None)
    return x
```

```python
import functools
import math

import numpy as np
import jax
import jax.numpy as jnp
from jax import lax
from jax.experimental import pallas as pl
from jax.experimental.pallas import tpu as pltpu

F32 = jnp.float32
BF16 = jnp.bfloat16

HEAD_DIM = 64
GM_GROUPS = 4
GM_CHUNK = 128
GM_WIDTH = GM_GROUPS * HEAD_DIM
DA_HEADS = 4
DA_QK_DIM = HEAD_DIM // 2
DA_WIDTH = DA_HEADS * HEAD_DIM
NSA_HEADS = 8
NSA_KV_GROUPS = 2
NSA_WIDTH = NSA_HEADS * HEAD_DIM
NSA_KV_WIDTH = NSA_KV_GROUPS * HEAD_DIM
N_GATES = 3 * NSA_HEADS
GATE_ROWS = 16
CMP_LEN = 32
CMP_STRIDE = 16
SLC_LEN = 64
SLC_TOPK = 16
WIN = 512
FORCE_BONUS = 1000.0
ROPE_THETA = 10000.0
EPS = 1e-6
N_MOD = 9
NEG = -1e30
LOG2E = math.log2(math.e)

LANE = 128
SUBLANE = 8
FF_CHUNK = 256
ATT_TILE = 256
CHUNK = 256
VMEM_LIMIT = 56 * 1024 * 1024

OFF_GU, OFF_GV, OFF_DQ, OFF_DK, OFF_DV = 0, 256, 512, 768, 1024
OFF_NQ, OFF_KC, OFF_VC, OFF_KS, OFF_VS, OFF_KW, OFF_VW, OFF_NG = 1280, 1792, 1920, 2048, 2176, 2304, 2432, 2560
TOK_GU, TOK_GV, TOK_DQ, TOK_DK, TOK_NQ, TOK_KC, TOK_KS, TOK_WIDTH = 0, 256, 512, 768, 1024, 1536, 1792, 2048
FEAT_DV, FEAT_VS, FEAT_VW, FEAT_NG = 0, 256, 384, 512
FEAT_ROWS = FEAT_NG + NSA_KV_GROUPS * GATE_ROWS


def _params(sem):
    return pltpu.CompilerParams(dimension_semantics=sem, vmem_limit_bytes=VMEM_LIMIT)


def _resident(shape, index_map):
    return pl.BlockSpec(shape, index_map, pipeline_mode=pl.Buffered(1))


def _dot(a, b):
    return jnp.dot(a, b, preferred_element_type=F32)


def _dot_nt(a, b):
    return lax.dot_general(a, b, (((1,), (1,)), ((), ())), preferred_element_type=F32)


def _mod_norm(x, g, shift, scale):
    ms = jnp.mean(x * x, axis=-1, keepdims=True)
    return x * lax.rsqrt(ms + EPS) * g * (1.0 + scale) + shift


def _gelu(x):
    c = math.sqrt(2.0 / math.pi)
    return 0.5 * x * (1.0 + jnp.tanh(c * (x + 0.044715 * (x * x * x))))


def _silu(x):
    return x * jax.nn.sigmoid(x)


def _adaln_kernel(c_ref, w_ref, b_ref, o_ref):
    ca = _silu(c_ref[...])
    o_ref[...] = _dot(ca, w_ref[...]) + b_ref[...]


def _adaln(c, w_ada, b_ada):
    L, D, ND = w_ada.shape
    B = c.shape[0]
    tn = ND // 8
    return pl.pallas_call(
        _adaln_kernel,
        grid=(L, ND // tn),
        in_specs=[
            pl.BlockSpec((B, D), lambda l, j: (0, 0)),
            pl.BlockSpec((None, D, tn), lambda l, j: (l, 0, j)),
            pl.BlockSpec((None, 1, tn), lambda l, j: (l, 0, j)),
        ],
        out_specs=pl.BlockSpec((None, B, tn), lambda l, j: (l, 0, j)),
        out_shape=jax.ShapeDtypeStruct((L, B, ND), F32),
        compiler_params=_params(("arbitrary", "arbitrary")),
        name="adaln",
    )(c, w_ada, b_ada.reshape(L, 1, ND))


def _ffn_kernel(*refs, rows, nchunk, mixed, final):
    refs = list(refs)
    x_ref, mod_ref, g_ref, wg_ref, wu_ref, wd_ref = refs[:6]
    rest = refs[6:]
    if mixed:
        ya_ref, yb_ref, yc_ref, wo_ref = rest[:4]
        rest = rest[4:]
    if final:
        fg_ref = rest[0]
        rest = rest[1:]
    o_ref, h_ref, a_ref = rest[:3]
    r_sh, r_sc, r_gt = rows
    if mixed:
        x_sc = rest[3]
        y = _dot(ya_ref[...], wo_ref[0:GM_WIDTH, :])
        y += _dot(yb_ref[...], wo_ref[GM_WIDTH:GM_WIDTH + DA_WIDTH, :])
        y += _dot(yc_ref[...], wo_ref[GM_WIDTH + DA_WIDTH:, :])
        x_sc[...] = x_ref[...] + mod_ref[5:6, :] * y
        x_ref = x_sc
    h_ref[...] = _mod_norm(x_ref[...], g_ref[...], mod_ref[r_sh:r_sh + 1, :], mod_ref[r_sc:r_sc + 1, :]).astype(BF16)
    for j in range(nchunk):
        cols = slice(j * FF_CHUNK, (j + 1) * FF_CHUNK)
        gate = _dot(h_ref[...], wg_ref[:, cols])
        up = _dot(h_ref[...], wu_ref[:, cols])
        a_ref[:, cols] = (_silu(gate) * up).astype(BF16)
    y = x_ref[...] + 0.5 * mod_ref[r_gt:r_gt + 1, :] * _dot(a_ref[...], wd_ref[...])
    if final:
        y = y * lax.rsqrt(jnp.mean(y * y, axis=-1, keepdims=True) + EPS) * fg_ref[...]
    o_ref[...] = y


def _ffn(x, mod, g, wg, wu, wd, which, rows, mix=None, final_g=None, tm=512):
    B, S, D = x.shape
    F = wd.shape[-2]
    mixed = mix is not None
    final = final_g is not None
    tok = lambda w: pl.BlockSpec((None, tm, w), lambda b, i: (b, i, 0))
    row = pl.BlockSpec((1, D), lambda b, i: (0, 0))
    stacked = lambda w: _resident((None, None) + w.shape[2:], lambda b, i: which + (0, 0))
    in_specs = [tok(D), pl.BlockSpec((None, N_MOD, D), lambda b, i: (b, 0, 0)), row,
                stacked(wg), stacked(wu), stacked(wd)]
    args = [x, mod, g.reshape(1, D), wg, wu, wd]
    scratch = [pltpu.VMEM((tm, D), BF16), pltpu.VMEM((tm, F), BF16)]
    if mixed:
        ya, yb, yc, w_out = mix
        in_specs += [tok(ya.shape[-1]), tok(yb.shape[-1]), tok(yc.shape[-1]),
                     _resident((None,) + w_out.shape[1:], lambda b, i: (which[0], 0, 0))]
        args += [ya, yb, yc, w_out]
        scratch.append(pltpu.VMEM((tm, D), F32))
    if final:
        in_specs.append(row)
        args.append(final_g.reshape(1, D))
    return pl.pallas_call(
        functools.partial(_ffn_kernel, rows=rows, nchunk=F // FF_CHUNK, mixed=mixed, final=final),
        grid=(B, S // tm),
        in_specs=in_specs,
        out_specs=tok(D),
        out_shape=jax.ShapeDtypeStruct((B, S, D), F32),
        scratch_shapes=scratch,
        compiler_params=_params(("arbitrary", "arbitrary")),
        name="ffn" + ("_mix" if mixed else "") + ("_final" if final else ""),
    )(*args)


def _inproj_kernel(x_ref, mod_ref, g_ref, w_ref, wf_ref, tab_ref, lng_ref, ws_ref, bs_ref, avg_ref,
                   ya_ref, dq_ref, dk_ref, dvt_ref, nq_ref,
                   kc_ref, vc_ref, ks_ref, vst_ref, kw_ref, vwt_ref, ngt_ref, h_ref):
    h_ref[...] = _mod_norm(x_ref[...], g_ref[...], mod_ref[3:4, :], mod_ref[4:5, :]).astype(BF16)
    tm = h_ref.shape[0]

    def mm(off, width):
        return _dot(h_ref[...], w_ref[:, off:off + width])

    def heads(ref, z):
        for hd in range(z.shape[1] // HEAD_DIM):
            ref[hd] = z[:, hd * HEAD_DIM:(hd + 1) * HEAD_DIM].astype(ref.dtype)

    def heads_t(ref, zt):
        for hd in range(zt.shape[0] // HEAD_DIM):
            ref[hd] = zt[hd * HEAD_DIM:(hd + 1) * HEAD_DIM].astype(ref.dtype)

    def rope(z, half, t, scale=1.0):
        cos = tab_ref[t] * scale
        sin = tab_ref[t + 1] * scale
        first = lax.broadcasted_iota(jnp.int32, (1, LANE), 1) % (2 * half) < half
        sin_lo = jnp.where(first, -sin, 0.0)
        sin_hi = jnp.where(first, 0.0, sin)
        blocks = []
        for c in range(0, z.shape[1], LANE):
            zb = z[:, c:c + LANE]
            blocks.append(zb * cos + pltpu.roll(zb, LANE - half, 1) * sin_lo + pltpu.roll(zb, half, 1) * sin_hi)
        return blocks[0] if len(blocks) == 1 else jnp.concatenate(blocks, axis=1)

    avg = avg_ref[...]

    def group_mean(t):
        hi = t.astype(BF16)
        lo = (t - hi.astype(F32)).astype(BF16)
        return _dot(hi, avg) + _dot(lo, avg)

    u_raw = mm(TOK_GU, GM_WIDTH)
    v_raw = mm(TOK_GV, GM_WIDTH)
    heads(dq_ref, rope(mm(TOK_DQ, 256), DA_QK_DIM // 2, 0, DA_QK_DIM ** -0.5 * LOG2E))
    v = _gelu(v_raw)
    d = v - group_mean(v)
    heads(dk_ref, rope(mm(TOK_DK, 256), DA_QK_DIM // 2, 0))
    var = group_mean(d * d)
    heads(nq_ref, rope(mm(TOK_NQ, NSA_WIDTH), HEAD_DIM // 2, 2, HEAD_DIM ** -0.5 * LOG2E))
    vn = (d * lax.rsqrt(var + EPS) * lng_ref[...]).astype(BF16)
    kk = rope(mm(TOK_KS, 2 * NSA_KV_WIDTH), HEAD_DIM // 2, 2)
    heads(ks_ref, kk[:, :NSA_KV_WIDTH])
    heads(kw_ref, kk[:, NSA_KV_WIDTH:])

    u = _gelu(u_raw)
    r = lax.broadcasted_iota(jnp.int32, (GM_CHUNK, GM_CHUNK), 0)
    c = lax.broadcasted_iota(jnp.int32, (GM_CHUNK, GM_CHUNK), 1)
    w_sp = [jnp.where(r >= c, ws_ref[g], 0.0).astype(BF16) for g in range(GM_GROUPS)]
    group = lax.broadcasted_iota(jnp.int32, (GM_CHUNK, GM_WIDTH), 1) // HEAD_DIM
    for ch in range(tm // GM_CHUNK):
        t0, t1 = ch * GM_CHUNK, (ch + 1) * GM_CHUNK
        s = bs_ref[...]
        for g in range(GM_GROUPS):
            s = s + jnp.where(group == g, _dot(w_sp[g], vn[t0:t1]), 0.0)
        ya_ref[t0:t1, :] = (u[t0:t1] * s).astype(ya_ref.dtype)

    zt = _dot_nt(wf_ref[...], h_ref[...])
    heads_t(dvt_ref, zt[FEAT_DV:FEAT_DV + DA_WIDTH])
    heads_t(vst_ref, zt[FEAT_VS:FEAT_VS + NSA_KV_WIDTH])
    heads_t(vwt_ref, zt[FEAT_VW:FEAT_VW + NSA_KV_WIDTH])
    for g in range(NSA_KV_GROUPS):
        ngt_ref[g] = jax.nn.sigmoid(zt[FEAT_NG + g * GATE_ROWS:FEAT_NG + (g + 1) * GATE_ROWS])
    kv = mm(TOK_KC, 2 * NSA_KV_WIDTH)
    heads(kc_ref, kv[:, :NSA_KV_WIDTH])
    heads(vc_ref, kv[:, NSA_KV_WIDTH:])


def _inproj(x, mod, g, w_tok, w_feat, layer, tabs, ln_g, w_s, b_s):
    B, S, D = x.shape
    assert w_tok.shape[1:] == (D, TOK_WIDTH) and w_feat.shape[1:] == (FEAT_ROWS, D)
    tm = ATT_TILE
    nt = S // tm

    def hm(nh):
        return (jax.ShapeDtypeStruct((B, nh, S, HEAD_DIM), BF16),
                pl.BlockSpec((None, nh, tm, HEAD_DIM), lambda b, i: (b, 0, i, 0)))

    def hm_t(nh):
        return (jax.ShapeDtypeStruct((B, nh, nt, HEAD_DIM, tm), BF16),
                pl.BlockSpec((None, nh, None, HEAD_DIM, tm), lambda b, i: (b, 0, i, 0, 0)))

    G = NSA_KV_GROUPS
    outs = [(jax.ShapeDtypeStruct((B, S, GM_WIDTH), BF16), pl.BlockSpec((None, tm, GM_WIDTH), lambda b, i: (b, i, 0))),
            hm(DA_HEADS), hm(DA_HEADS), hm_t(DA_HEADS), hm(NSA_HEADS),
            hm(G), hm(G), hm(G), hm_t(G), hm(G), hm_t(G),
            (jax.ShapeDtypeStruct((B, G, GATE_ROWS, S), F32),
             pl.BlockSpec((None, G, GATE_ROWS, tm), lambda b, i: (b, 0, 0, i)))]
    bias = jnp.repeat(b_s.T, HEAD_DIM, axis=1)
    lane_group = np.arange(GM_WIDTH) // HEAD_DIM
    avg = jnp.asarray((lane_group[:, None] == lane_group[None, :]) / HEAD_DIM, BF16)
    full = lambda a: pl.BlockSpec(a.shape, lambda b, i: (0,) * a.ndim)
    return pl.pallas_call(
        _inproj_kernel,
        grid=(B, nt),
        in_specs=[
            pl.BlockSpec((None, tm, D), lambda b, i: (b, i, 0)),
            pl.BlockSpec((None, N_MOD, D), lambda b, i: (b, 0, 0)),
            pl.BlockSpec((1, D), lambda b, i: (0, 0)),
            _resident((None,) + w_tok.shape[1:], lambda b, i: (layer, 0, 0)),
            _resident((None,) + w_feat.shape[1:], lambda b, i: (layer, 0, 0)),
            pl.BlockSpec((tabs.shape[0], tm, LANE), lambda b, i: (0, i, 0)),
            pl.BlockSpec((1, GM_WIDTH), lambda b, i: (0, 0)), full(w_s), full(bias), full(avg),
        ],
        out_specs=[o[1] for o in outs],
        out_shape=[o[0] for o in outs],
        scratch_shapes=[pltpu.VMEM((tm, D), BF16)],
        compiler_params=_params(("arbitrary", "arbitrary")),
        name="inproj",
    )(x, mod, g.reshape(1, D), w_tok, w_feat, tabs, ln_g.reshape(1, GM_WIDTH), w_s, bias, avg)


AHEAD = 6


SUM_ROWS = 16


def _online_steps(q_of, steps, m_ref, acc_ref):
    n = len(steps)
    keys = steps[0][3].shape[1]
    ones = jnp.where(lax.broadcasted_iota(jnp.int32, (SUM_ROWS, keys), 0) == 0, 1.0, 0.0).astype(BF16)

    def scores(t):
        _, qc, k, _, bias = steps[t]
        s = _dot_nt(k, q_of(qc))
        return s if bias is None else s + bias()

    s = [scores(t) if t < AHEAD else None for t in range(n)]
    for t in range(n):
        c, _, _, vt, _ = steps[t]
        m_old = m_ref[c]
        m_new = jnp.maximum(m_old, jnp.max(s[t], axis=0, keepdims=True))
        alpha = jnp.exp2(m_old - m_new)
        p = jnp.exp2(s[t] - m_new).astype(BF16)
        s[t] = None
        m_ref[c] = m_new
        if t + AHEAD < n:
            s[t + AHEAD] = scores(t + AHEAD)
        acc_ref[c] = alpha * acc_ref[c] + _dot(jnp.concatenate([vt, ones], axis=0), p)


def _normalized(c0, n, acc_ref):
    dv = acc_ref.shape[1] - SUM_ROWS
    parts = [acc_ref[c, :dv] * (1.0 / acc_ref[c, dv:dv + 1]) for c in range(c0, c0 + n)]
    return parts[0] if n == 1 else jnp.concatenate(parts, axis=1)


def _online_init(m_ref, acc_ref):
    m_ref[...] = jnp.full_like(m_ref, NEG)
    acc_ref[...] = jnp.zeros_like(acc_ref)


def _causal_bias(T, upper):
    key = lax.broadcasted_iota(jnp.int32, (T, T), 0)
    qry = lax.broadcasted_iota(jnp.int32, (T, T), 1)
    keep = (key > qry) if upper else (key <= qry)
    return jnp.where(keep, 0.0, NEG)


def _da_kernel(q_ref, k_ref, vt_ref, lam_ref, subg_ref, o_ref, q_sc, bias_sc, m_ref, acc_ref, *, lam_init):
    i = pl.program_id(2)
    HP, T, Dh = q_ref.shape
    cpt = T // CHUNK
    lf = lam_ref[...]
    lam = (jnp.exp(jnp.sum(lf[0:1] * lf[1:2], axis=-1, keepdims=True))
           - jnp.exp(jnp.sum(lf[2:3] * lf[3:4], axis=-1, keepdims=True)) + lam_init)
    first = lax.broadcasted_iota(jnp.int32, (1, Dh), 1) < DA_QK_DIM
    zero = jnp.zeros((), BF16)
    for hd in range(HP):
        q = q_ref[hd]
        q_sc[(2 * hd) * T:(2 * hd + 1) * T, :] = jnp.where(first, q, zero)
        q_sc[(2 * hd + 1) * T:(2 * hd + 2) * T, :] = jnp.where(first, zero, q)
    @pl.when(i == 0)
    def _():
        bias_sc[...] = _causal_bias(T, upper=False)

    _online_init(m_ref, acc_ref)

    def tile(j, masked):
        start = pl.multiple_of(j * T, T)
        steps = []
        for hd in range(HP):
            k = k_ref[hd, pl.ds(start, T), :]
            vt = vt_ref[hd, j]
            for cc in range(2 * cpt):
                lo = (cc % cpt) * CHUNK
                bias = (lambda lo=lo: bias_sc[:, lo:lo + CHUNK]) if masked else None
                c = hd * 2 * cpt + cc
                steps.append((c, c, k, vt, bias))
        return steps

    def run(steps):
        _online_steps(lambda c: q_sc[c * CHUNK:(c + 1) * CHUNK, :], steps, m_ref, acc_ref)

    def body(jj, carry):
        run(tile(2 * jj, False) + tile(2 * jj + 1, False))
        return carry

    lax.fori_loop(0, i // 2, body, 0)

    @pl.when(i % 2 == 0)
    def _():
        run(tile(i, True))

    @pl.when(i % 2 == 1)
    def _():
        run(tile(i - 1, False) + tile(i, True))

    outs = []
    for hd in range(HP):
        o1 = _normalized(2 * hd * cpt, cpt, acc_ref)
        o2 = _normalized((2 * hd + 1) * cpt, cpt, acc_ref)
        o = o1 - lam * o2
        o = o * lax.rsqrt(jnp.mean(o * o, axis=0, keepdims=True) + EPS) * subg_ref[...] * (1.0 - lam_init)
        outs.append(o)
    o_ref[...] = jnp.concatenate(outs, axis=0).T.astype(o_ref.dtype)


def _diff_attn(dq, dk, dvt, da_lambda, sub_g, lam_init):
    B, H, S, Dh = dq.shape
    T = ATT_TILE
    HP = H
    rows = HP * 2 * T
    return pl.pallas_call(
        functools.partial(_da_kernel, lam_init=lam_init),
        grid=(B, H // HP, S // T),
        in_specs=[
            pl.BlockSpec((None, HP, T, Dh), lambda b, h, i: (b, h, i, 0)),
            pl.BlockSpec((None, HP, S, Dh), lambda b, h, i: (b, h, 0, 0)),
            pl.BlockSpec((None, HP, S // T, Dh, T), lambda b, h, i: (b, h, 0, 0, 0)),
            pl.BlockSpec(da_lambda.shape, lambda b, h, i: (0, 0)),
            pl.BlockSpec((Dh, T), lambda b, h, i: (0, 0)),
        ],
        out_specs=pl.BlockSpec((None, T, HP * Dh), lambda b, h, i: (b, i, h)),
        out_shape=jax.ShapeDtypeStruct((B, S, H * Dh), BF16),
        scratch_shapes=[pltpu.VMEM((rows, Dh), BF16), pltpu.VMEM((T, T), F32),
                        pltpu.VMEM((rows // CHUNK, 1, CHUNK), F32),
                        pltpu.VMEM((rows // CHUNK, Dh + SUM_ROWS, CHUNK), F32)],
        compiler_params=_params(("arbitrary", "arbitrary", "arbitrary")),
        name="diff_attn",
    )(dq, dk, dvt, da_lambda, jnp.broadcast_to(sub_g[:, None], (Dh, T)))


def _compress_kernel(kc_ref, vc_ref, w1_ref, pe_ref, w2_ref, w2rot_ref, tab_ref, ko_ref, vo_ref):
    rows = kc_ref.shape[0]
    G, _, R = vo_ref.shape
    half = w1_ref.shape[1] // 2

    def hidden(x_ref, t):
        xr = x_ref[...]
        top = _dot(xr, w1_ref[t, :half, :])
        bot = _dot(xr, w1_ref[t, half:, :])
        pe_rows = jnp.broadcast_to(pe_ref[t], (SUBLANE, 2 * half)).astype(BF16)
        pe = _dot(pe_rows, w1_ref[t])[0:1]
        return _silu(top + pltpu.roll(bot, rows - 1, 0) + pe).astype(BF16)

    ak = hidden(kc_ref, 0)
    ko = _dot(ak, w2_ref[0]) * tab_ref[0] + _dot(ak, w2rot_ref[...]) * tab_ref[1]
    vo = _dot(hidden(vc_ref, 1), w2_ref[1])
    ko_ref[...] = ko.astype(BF16)
    vot = jnp.concatenate([ko, vo], axis=1).T[HEAD_DIM:].astype(BF16)
    for g in range(G):
        vo_ref[g] = vot[:, g * R:(g + 1) * R]


def _compress(kc, vc, w1, pe, w2, w2rot, tab):
    B, G, S, Dh = kc.shape
    R = S // CMP_STRIDE
    W = CMP_STRIDE * Dh
    blk = pl.BlockSpec((None, G * R, W), lambda b: (b, 0, 0))
    full = lambda a: pl.BlockSpec(a.shape, lambda b: (0,) * a.ndim)
    tab = jnp.tile(tab, (1, G, 1))
    kcmp, vcmpt = pl.pallas_call(
        _compress_kernel,
        grid=(B,),
        in_specs=[blk, blk, full(w1), full(pe), full(w2), full(w2rot), full(tab)],
        out_specs=[pl.BlockSpec((None, G * R, Dh), lambda b: (b, 0, 0)),
                   pl.BlockSpec((None, G, Dh, R), lambda b: (b, 0, 0, 0))],
        out_shape=[jax.ShapeDtypeStruct((B, G * R, Dh), BF16), jax.ShapeDtypeStruct((B, G, Dh, R), BF16)],
        compiler_params=_params(("arbitrary",)),
        name="nsa_compress",
    )(kc.reshape(B, G * R, W), vc.reshape(B, G * R, W), w1, pe, w2, w2rot, tab)
    return kcmp.reshape(B, G, R, Dh), vcmpt


def _cmp_kernel(q_ref, kc_ref, vct_ref, ovt_ref, o_ref, sel_ref, *, ns, topk):
    i = pl.program_id(1)
    H, T, Dh = q_ref.shape
    G, R, _ = kc_ref.shape
    Hg = H // G
    NB = sel_ref.shape[1]
    pos = i * T + lax.broadcasted_iota(jnp.int32, (R, T), 1)
    cend = lax.broadcasted_iota(jnp.int32, (R, T), 0) * CMP_STRIDE + (CMP_LEN - 1)
    mask = cend <= pos
    seen = i * T + lax.broadcasted_iota(jnp.int32, (1, T), 1) >= CMP_LEN - 1
    scores = [_dot_nt(kc_ref[hd // Hg], q_ref[hd]) for hd in range(H)]
    blk = lax.broadcasted_iota(jnp.int32, (NB, T), 0)
    cur = (i * T + lax.broadcasted_iota(jnp.int32, (NB, T), 1)) // SLC_LEN
    valid = blk <= cur
    forced = valid & ((blk == 0) | (blk >= cur - 1))
    ovt = ovt_ref[...]

    for g in range(G):
        ps = jnp.zeros((R, T), F32)
        for hd in range(g * Hg, (g + 1) * Hg):
            s = jnp.where(mask, scores[hd], NEG)
            e = jnp.exp2(s - jnp.max(s, axis=0, keepdims=True))
            p = e * jnp.where(seen, 1.0 / jnp.sum(e, axis=0, keepdims=True), 0.0)
            o_ref[hd] = _dot(vct_ref[g], p.astype(BF16))
            ps = ps + p

        p_hi = ps.astype(BF16)
        p_lo = (ps - p_hi.astype(F32)).astype(BF16)
        imp = _dot(ovt, p_hi) + _dot(ovt, p_lo)
        score = jnp.where(forced, FORCE_BONUS, jnp.where(valid, imp, -1.0))
        score = jnp.where(blk < ns, score, -2.0)
        rows = [score[j:j + 1, :] for j in range(ns)]
        ranks = []
        for lo in range(0, NB, SUBLANE):
            sc = score[lo:lo + SUBLANE, :]
            blk_g = lo + lax.broadcasted_iota(jnp.int32, (SUBLANE, T), 0)
            rank = jnp.zeros((SUBLANE, T), F32)
            for j in range(ns):
                if j < lo:
                    ahead = rows[j] >= sc
                elif j >= lo + SUBLANE:
                    ahead = rows[j] > sc
                else:
                    ahead = (rows[j] > sc) | ((rows[j] == sc) & (blk_g > j))
                rank = rank + jnp.where(ahead, 1.0, 0.0)
            ranks.append(rank)
        rank = jnp.concatenate(ranks, axis=0)
        sel_ref[g] = jnp.where((rank < topk) & (score >= 0.0), 1.0, 0.0).astype(sel_ref.dtype)


def _cmp_attn(nq, kcmp, vcmpt, ovt, ns, topk):
    B, H, S, Dh = nq.shape
    G = kcmp.shape[1]
    R = kcmp.shape[2]
    NB = ovt.shape[0]
    T = ATT_TILE
    return pl.pallas_call(
        functools.partial(_cmp_kernel, ns=ns, topk=topk),
        grid=(B, S // T),
        in_specs=[pl.BlockSpec((None, H, T, Dh), lambda b, i: (b, 0, i, 0)),
                  pl.BlockSpec((None, G, R, Dh), lambda b, i: (b, 0, 0, 0)),
                  pl.BlockSpec((None, G, Dh, R), lambda b, i: (b, 0, 0, 0)),
                  pl.BlockSpec(ovt.shape, lambda b, i: (0, 0))],
        out_specs=[pl.BlockSpec((None, H, Dh, T), lambda b, i: (b, 0, 0, i)),
                   pl.BlockSpec((None, G, NB, T), lambda b, i: (b, 0, 0, i))],
        out_shape=[jax.ShapeDtypeStruct((B, H, Dh, S), F32), jax.ShapeDtypeStruct((B, G, NB, S), BF16)],
        compiler_params=_params(("arbitrary", "arbitrary")),
        name="nsa_cmp",
    )(nq, kcmp, vcmpt, ovt)


def _nsa_kernel(q_ref, ks_ref, vst_ref, kw_ref, vwt_ref, oct_ref, sel_ref, ngt_ref, ext_ref, o_ref,
                bias_sc, m_ref, acc_ref):
    i = pl.program_id(1)
    H, T, Dh = q_ref.shape
    G = ks_ref.shape[0]
    Hg = H // G
    cpt = T // CHUNK
    n = Hg * cpt
    back = WIN // T
    LOWER, UPPER = 2 * G, 2 * G + 1
    @pl.when(i == 0)
    def _():
        bias_sc[LOWER] = _causal_bias(T, upper=False)
        bias_sc[UPPER] = _causal_bias(T, upper=True)

    _online_init(m_ref, acc_ref)

    def tile(g, k_ref, vt_ref, j, slot, window):
        k = k_ref[g, pl.ds(pl.multiple_of(j * T, T), T), :]
        vt = vt_ref[g, j]
        steps = []
        for c in range(n):
            lo = (c % cpt) * CHUNK
            bias = None if slot is None else (lambda lo=lo: bias_sc[slot, :, lo:lo + CHUNK])
            steps.append((2 * n * g + (n if window else 0) + c, g * n + c, k, vt, bias))
        return steps

    def run(steps):
        _online_steps(lambda c: q_ref[c // cpt, (c % cpt) * CHUNK:(c % cpt + 1) * CHUNK, :], steps, m_ref, acc_ref)

    def slc_tile(g, which, j, diagonal):
        slot = 2 * g + which
        chosen = _dot(ext_ref[j], sel_ref[g]) > 0.5
        bias_sc[slot] = jnp.where(chosen, _causal_bias(T, upper=False) if diagonal else 0.0, NEG)
        return tile(g, ks_ref, vst_ref, j, slot, False)

    def both(fn):
        steps = []
        for g in range(G):
            steps = steps + fn(g)
        return steps

    def slc_body(jj, carry):
        run(both(lambda g: slc_tile(g, 0, 2 * jj, False) + slc_tile(g, 1, 2 * jj + 1, False)))
        return carry

    lax.fori_loop(0, i // 2, slc_body, 0)

    def last_steps(g, odd, n_back):
        steps = (slc_tile(g, 0, i - 1, False) + slc_tile(g, 1, i, True)) if odd else slc_tile(g, 0, i, True)
        steps = steps + tile(g, kw_ref, vwt_ref, i, LOWER, True)
        for d in range(1, n_back + 1):
            steps = steps + tile(g, kw_ref, vwt_ref, i - d, UPPER if d == back else None, True)
        return steps

    for n_back in range(back):
        @pl.when(i == n_back)
        def _():
            run(both(lambda g: last_steps(g, n_back % 2, n_back)))

    for odd in (0, 1):
        @pl.when((i >= back) & (i % 2 == odd))
        def _():
            run(both(lambda g: last_steps(g, odd, back)))

    outs = []
    for hd in range(H):
        g, c0 = hd // Hg, 2 * n * (hd // Hg) + (hd % Hg) * cpt
        gates = ngt_ref[g]
        r = 3 * (hd % Hg)
        o = (gates[r:r + 1, :] * oct_ref[hd]
             + gates[r + 1:r + 2, :] * _normalized(c0, cpt, acc_ref)
             + gates[r + 2:r + 3, :] * _normalized(c0 + n, cpt, acc_ref))
        outs.append(o)
    for pr in range(H // 2):
        o_ref[:, pr * LANE:(pr + 1) * LANE] = jnp.concatenate(outs[2 * pr:2 * pr + 2], axis=0).T.astype(o_ref.dtype)


def _nsa_main(nq, ks, vst, kw, vwt, o_cmpt, sel, ngt, expand_t):
    B, H, S, Dh = nq.shape
    G = ks.shape[1]
    T = ATT_TILE
    assert WIN % T == 0 and (H // G) % 2 == 0
    rows = H * T
    kspec = pl.BlockSpec((None, G, S, Dh), lambda b, i: (b, 0, 0, 0))
    vspec = pl.BlockSpec((None, G, S // T, Dh, T), lambda b, i: (b, 0, 0, 0, 0))
    NB = sel.shape[2]
    return pl.pallas_call(
        _nsa_kernel,
        grid=(B, S // T),
        in_specs=[pl.BlockSpec((None, H, T, Dh), lambda b, i: (b, 0, i, 0)),
                  kspec, vspec, kspec, vspec,
                  pl.BlockSpec((None, H, Dh, T), lambda b, i: (b, 0, 0, i)),
                  pl.BlockSpec((None, G, NB, T), lambda b, i: (b, 0, 0, i)),
                  pl.BlockSpec((None, G, GATE_ROWS, T), lambda b, i: (b, 0, 0, i)),
                  pl.BlockSpec(expand_t.shape, lambda b, i: (0, 0, 0))],
        out_specs=pl.BlockSpec((None, T, H * Dh), lambda b, i: (b, i, 0)),
        out_shape=jax.ShapeDtypeStruct((B, S, H * Dh), BF16),
        scratch_shapes=[pltpu.VMEM((2 * G + 2, T, T), F32),
                        pltpu.VMEM((2 * rows // CHUNK, 1, CHUNK), F32),
                        pltpu.VMEM((2 * rows // CHUNK, Dh + SUM_ROWS, CHUNK), F32)],
        compiler_params=_params(("arbitrary", "arbitrary")),
        name="nsa_main",
    )(nq, ks, vst, kw, vwt, o_cmpt, sel, ngt, expand_t)


def _rope_table(pos, period, width):
    half = period // 2
    local = np.arange(width) % period
    inv = ROPE_THETA ** (-(local % half).astype(np.float32) / half)
    ang = pos.astype(jnp.float32)[:, None] * jnp.asarray(inv, F32)[None, :]
    return [jnp.cos(ang), jnp.sin(ang)]


def _rope_tables(S):
    pos = jnp.arange(S)
    tabs = _rope_table(pos, DA_QK_DIM, LANE) + _rope_table(pos, HEAD_DIM, LANE)
    cmp_end = jnp.arange(S // CMP_STRIDE) * CMP_STRIDE + CMP_LEN - 1
    return jnp.stack(tabs), jnp.stack(_rope_table(cmp_end, HEAD_DIM, HEAD_DIM))


def _rot_cols(w, period):
    d, n = w.shape
    half = period // 2
    c = w.reshape(d, n // period, period)
    return jnp.concatenate([-c[..., half:], c[..., :half]], axis=-1).reshape(d, n)


def _layout_w_in(w):
    depth, d, _ = w.shape
    cols = lambda off, n: w[:, :, off:off + n]
    kvw = NSA_KV_WIDTH
    w_tok = jnp.concatenate(
        [cols(OFF_GU, GM_WIDTH), cols(OFF_GV, GM_WIDTH), cols(OFF_DQ, 256), cols(OFF_DK, 256), cols(OFF_NQ, NSA_WIDTH),
         cols(OFF_KC, kvw), cols(OFF_VC, kvw), cols(OFF_KS, kvw), cols(OFF_KW, kvw)], axis=2)
    per_group = N_GATES // NSA_KV_GROUPS
    pad = jnp.zeros((depth, d, GATE_ROWS - per_group), w.dtype)
    feat = [cols(OFF_DV, DA_WIDTH), cols(OFF_VS, kvw), cols(OFF_VW, kvw)]
    for g in range(NSA_KV_GROUPS):
        feat += [cols(OFF_NG + g * per_group, per_group), pad]
    return w_tok.astype(BF16), jnp.swapaxes(jnp.concatenate(feat, axis=2), 1, 2).astype(BF16)


def _overlap_table_t(S, nb):
    nc = (S - CMP_LEN) // CMP_STRIDE + 1
    ns = S // SLC_LEN
    cs = np.arange(nc) * CMP_STRIDE
    bs = np.arange(ns) * SLC_LEN
    ov = np.clip(np.minimum(cs[:, None] + CMP_LEN, bs[None, :] + SLC_LEN) - np.maximum(cs[:, None], bs[None, :]), 0, None)
    out = np.zeros((nb, S // CMP_STRIDE), np.float32)
    out[:ns, :nc] = (ov / CMP_STRIDE).T
    return jnp.asarray(out, BF16)


def _expand_table_t(S, T, nb):
    key_block = np.arange(S).reshape(S // T, T, 1) // SLC_LEN
    return jnp.asarray(key_block == np.arange(nb).reshape(1, 1, nb), BF16)


def kernel(x, c, w_ada, b_ada, norm_g, ffn_w_gate, ffn_w_up, ffn_w_down, w_in, w_out,
           gm_ln_g, gm_w_s, gm_b_s, da_lambda, da_sub_g, nsa_cmp_pe, nsa_cmp_w1, nsa_cmp_w2, final_g):
    B, S, D = x.shape
    depth = w_in.shape[0]
    ns = S // SLC_LEN
    topk = min(SLC_TOPK, ns)
    nb = -(-ns // 16) * 16
    assert S % 512 == 0 and ATT_TILE % CHUNK == 0 and CHUNK % LANE == 0 and ffn_w_gate.shape[-1] % FF_CHUNK == 0

    mod_all = _adaln(c, w_ada, b_ada).reshape(depth, B, N_MOD, D)
    tabs, cmp_tab = _rope_tables(S)
    ovt = _overlap_table_t(S, nb)
    expand_t = _expand_table_t(S, ATT_TILE, nb)

    bf = lambda w: w.astype(BF16)
    wg, wu, wd, wo = bf(ffn_w_gate), bf(ffn_w_up), bf(ffn_w_down), bf(w_out)
    w_tok, w_feat = _layout_w_in(w_in)

    for l in range(depth):
        mod = mod_all[l]
        lam_init = 0.8 - 0.6 * math.exp(-0.3 * l)
        x = _ffn(x, mod, norm_g[l, 0], wg, wu, wd, (l, 0), rows=(0, 1, 2))

        (y_a, dq, dk, dvt, nq, kc, vc, ks, vst, kw, vwt, ngt) = _inproj(
            x, mod, norm_g[l, 1], w_tok, w_feat, l, tabs, gm_ln_g[l], gm_w_s[l], gm_b_s[l])
        y_b = _diff_attn(dq, dk, dvt, da_lambda[l], da_sub_g[l], lam_init)
        kcmp, vcmpt = _compress(
            kc, vc, bf(nsa_cmp_w1[l]), nsa_cmp_pe[l].reshape(2, 1, CMP_LEN * HEAD_DIM),
            bf(nsa_cmp_w2[l]), bf(_rot_cols(nsa_cmp_w2[l, 0], HEAD_DIM)), cmp_tab)
        o_cmpt, sel = _cmp_attn(nq, kcmp, vcmpt, ovt, ns, topk)
        y_c = _nsa_main(nq, ks, vst, kw, vwt, o_cmpt, sel, ngt, expand_t)

        x = _ffn(x, mod, norm_g[l, 2], wg, wu, wd, (l, 1), rows=(6, 7, 8),
                 mix=(y_a, y_b, y_c, wo), final_g=final_g if l == depth - 1 else None)
    return x
```

```python
import functools
import math

import numpy as np
import jax
import jax.numpy as jnp
from jax import lax
from jax.experimental import pallas as pl
from jax.experimental.pallas import tpu as pltpu

F32 = jnp.float32
BF16 = jnp.bfloat16

HEAD_DIM = 64
GM_GROUPS = 4
GM_CHUNK = 128
GM_WIDTH = GM_GROUPS * HEAD_DIM
DA_HEADS = 4
DA_QK_DIM = HEAD_DIM // 2
DA_WIDTH = DA_HEADS * HEAD_DIM
NSA_HEADS = 8
NSA_KV_GROUPS = 2
NSA_WIDTH = NSA_HEADS * HEAD_DIM
NSA_KV_WIDTH = NSA_KV_GROUPS * HEAD_DIM
N_GATES = 3 * NSA_HEADS
GATE_ROWS = 16
CMP_LEN = 32
CMP_STRIDE = 16
SLC_LEN = 64
SLC_TOPK = 16
WIN = 512
FORCE_BONUS = 1000.0
ROPE_THETA = 10000.0
EPS = 1e-6
N_MOD = 9
NEG = -1e30
LOG2E = math.log2(math.e)

LANE = 128
SUBLANE = 8
FF_CHUNK = 256
ATT_TILE = 256
CHUNK = 256
VMEM_LIMIT = 56 * 1024 * 1024

OFF_GU, OFF_GV, OFF_DQ, OFF_DK, OFF_DV = 0, 256, 512, 768, 1024
OFF_NQ, OFF_KC, OFF_VC, OFF_KS, OFF_VS, OFF_KW, OFF_VW, OFF_NG = 1280, 1792, 1920, 2048, 2176, 2304, 2432, 2560
TOK_GU, TOK_GV, TOK_DQ, TOK_DK, TOK_NQ, TOK_KC, TOK_KS, TOK_WIDTH = 0, 256, 512, 768, 1024, 1536, 1792, 2048
FEAT_DV, FEAT_VS, FEAT_VW, FEAT_NG = 0, 256, 384, 512
FEAT_ROWS = FEAT_NG + NSA_KV_GROUPS * GATE_ROWS


def _params(sem):
    return pltpu.CompilerParams(dimension_semantics=sem, vmem_limit_bytes=VMEM_LIMIT)


def _resident(shape, index_map):
    return pl.BlockSpec(shape, index_map, pipeline_mode=pl.Buffered(1))


def _dot(a, b):
    return jnp.dot(a, b, preferred_element_type=F32)


def _dot_nt(a, b):
    return lax.dot_general(a, b, (((1,), (1,)), ((), ())), preferred_element_type=F32)


def _mod_norm(x, g, shift, scale):
    ms = jnp.mean(x * x, axis=-1, keepdims=True)
    return x * lax.rsqrt(ms + EPS) * g * (1.0 + scale) + shift


def _gelu(x):
    c = math.sqrt(2.0 / math.pi)
    return 0.5 * x * (1.0 + jnp.tanh(c * (x + 0.044715 * (x * x * x))))


def _silu(x):
    return x * jax.nn.sigmoid(x)


def _adaln_kernel(c_ref, w_ref, b_ref, o_ref):
    ca = _silu(c_ref[...])
    o_ref[...] = _dot(ca, w_ref[...]) + b_ref[...]


def _adaln(c, w_ada, b_ada):
    L, D, ND = w_ada.shape
    B = c.shape[0]
    tn = ND // 8
    return pl.pallas_call(
        _adaln_kernel,
        grid=(L, ND // tn),
        in_specs=[
            pl.BlockSpec((B, D), lambda l, j: (0, 0)),
            pl.BlockSpec((None, D, tn), lambda l, j: (l, 0, j)),
            pl.BlockSpec((None, 1, tn), lambda l, j: (l, 0, j)),
        ],
        out_specs=pl.BlockSpec((None, B, tn), lambda l, j: (l, 0, j)),
        out_shape=jax.ShapeDtypeStruct((L, B, ND), F32),
        compiler_params=_params(("arbitrary", "arbitrary")),
        name="adaln",
    )(c, w_ada, b_ada.reshape(L, 1, ND))


def _ffn_kernel(*refs, rows, nchunk, mixed, final):
    refs = list(refs)
    x_ref, mod_ref, g_ref, wg_ref, wu_ref, wd_ref = refs[:6]
    rest = refs[6:]
    if mixed:
        ya_ref, yb_ref, yc_ref, wo_ref = rest[:4]
        rest = rest[4:]
    if final:
        fg_ref = rest[0]
        rest = rest[1:]
    o_ref, h_ref, a_ref = rest[:3]
    r_sh, r_sc, r_gt = rows
    if mixed:
        x_sc = rest[3]
        y = _dot(ya_ref[...], wo_ref[0:GM_WIDTH, :])
        y += _dot(yb_ref[...], wo_ref[GM_WIDTH:GM_WIDTH + DA_WIDTH, :])
        y += _dot(yc_ref[...], wo_ref[GM_WIDTH + DA_WIDTH:, :])
        x_sc[...] = x_ref[...] + mod_ref[5:6, :] * y
        x_ref = x_sc
    h_ref[...] = _mod_norm(x_ref[...], g_ref[...], mod_ref[r_sh:r_sh + 1, :], mod_ref[r_sc:r_sc + 1, :]).astype(BF16)
    for j in range(nchunk):
        cols = slice(j * FF_CHUNK, (j + 1) * FF_CHUNK)
        gate = _dot(h_ref[...], wg_ref[:, cols])
        up = _dot(h_ref[...], wu_ref[:, cols])
        a_ref[:, cols] = (_silu(gate) * up).astype(BF16)
    y = x_ref[...] + 0.5 * mod_ref[r_gt:r_gt + 1, :] * _dot(a_ref[...], wd_ref[...])
    if final:
        y = y * lax.rsqrt(jnp.mean(y * y, axis=-1, keepdims=True) + EPS) * fg_ref[...]
    o_ref[...] = y


def _ffn(x, mod, g, wg, wu, wd, which, rows, mix=None, final_g=None, tm=512):
    B, S, D = x.shape
    F = wd.shape[-2]
    mixed = mix is not None
    final = final_g is not None
    tok = lambda w: pl.BlockSpec((None, tm, w), lambda b, i: (b, i, 0))
    row = pl.BlockSpec((1, D), lambda b, i: (0, 0))
    stacked = lambda w: _resident((None, None) + w.shape[2:], lambda b, i: which + (0, 0))
    in_specs = [tok(D), pl.BlockSpec((None, N_MOD, D), lambda b, i: (b, 0, 0)), row,
                stacked(wg), stacked(wu), stacked(wd)]
    args = [x, mod, g.reshape(1, D), wg, wu, wd]
    scratch = [pltpu.VMEM((tm, D), BF16), pltpu.VMEM((tm, F), BF16)]
    if mixed:
        ya, yb, yc, w_out = mix
        in_specs += [tok(ya.shape[-1]), tok(yb.shape[-1]), tok(yc.shape[-1]),
                     _resident((None,) + w_out.shape[1:], lambda b, i: (which[0], 0, 0))]
        args += [ya, yb, yc, w_out]
        scratch.append(pltpu.VMEM((tm, D), F32))
    if final:
        in_specs.append(row)
        args.append(final_g.reshape(1, D))
    return pl.pallas_call(
        functools.partial(_ffn_kernel, rows=rows, nchunk=F // FF_CHUNK, mixed=mixed, final=final),
        grid=(B, S // tm),
        in_specs=in_specs,
        out_specs=tok(D),
        out_shape=jax.ShapeDtypeStruct((B, S, D), F32),
        scratch_shapes=scratch,
        compiler_params=_params(("arbitrary", "arbitrary")),
        name="ffn" + ("_mix" if mixed else "") + ("_final" if final else ""),
    )(*args)


def _inproj_kernel(x_ref, mod_ref, g_ref, w_ref, wf_ref, tab_ref, lng_ref, ws_ref, bs_ref, avg_ref,
                   ya_ref, dq_ref, dk_ref, dvt_ref, nq_ref,
                   kc_ref, vc_ref, ks_ref, vst_ref, kw_ref, vwt_ref, ngt_ref, h_ref):
    h_ref[...] = _mod_norm(x_ref[...], g_ref[...], mod_ref[3:4, :], mod_ref[4:5, :]).astype(BF16)
    tm = h_ref.shape[0]

    def mm(off, width):
        return _dot(h_ref[...], w_ref[:, off:off + width])

    def heads(ref, z):
        for hd in range(z.shape[1] // HEAD_DIM):
            ref[hd] = z[:, hd * HEAD_DIM:(hd + 1) * HEAD_DIM].astype(ref.dtype)

    def heads_t(ref, zt):
        for hd in range(zt.shape[0] // HEAD_DIM):
            ref[hd] = zt[hd * HEAD_DIM:(hd + 1) * HEAD_DIM].astype(ref.dtype)

    def rope(z, half, t, scale=1.0):
        cos = tab_ref[t] * scale
        sin = tab_ref[t + 1] * scale
        first = lax.broadcasted_iota(jnp.int32, (1, LANE), 1) % (2 * half) < half
        sin_lo = jnp.where(first, -sin, 0.0)
        sin_hi = jnp.where(first, 0.0, sin)
        blocks = []
        for c in range(0, z.shape[1], LANE):
            zb = z[:, c:c + LANE]
            blocks.append(zb * cos + pltpu.roll(zb, LANE - half, 1) * sin_lo + pltpu.roll(zb, half, 1) * sin_hi)
        return blocks[0] if len(blocks) == 1 else jnp.concatenate(blocks, axis=1)

    avg = avg_ref[...]

    def group_mean(t):
        hi = t.astype(BF16)
        lo = (t - hi.astype(F32)).astype(BF16)
        return _dot(hi, avg) + _dot(lo, avg)

    u_raw = mm(TOK_GU, GM_WIDTH)
    v_raw = mm(TOK_GV, GM_WIDTH)
    heads(dq_ref, rope(mm(TOK_DQ, 256), DA_QK_DIM // 2, 0, DA_QK_DIM ** -0.5 * LOG2E))
    v = _gelu(v_raw)
    d = v - group_mean(v)
    heads(dk_ref, rope(mm(TOK_DK, 256), DA_QK_DIM // 2, 0))
    var = group_mean(d * d)
    heads(nq_ref, rope(mm(TOK_NQ, NSA_WIDTH), HEAD_DIM // 2, 2, HEAD_DIM ** -0.5 * LOG2E))
    vn = (d * lax.rsqrt(var + EPS) * lng_ref[...]).astype(BF16)
    kk = rope(mm(TOK_KS, 2 * NSA_KV_WIDTH), HEAD_DIM // 2, 2)
    heads(ks_ref, kk[:, :NSA_KV_WIDTH])
    heads(kw_ref, kk[:, NSA_KV_WIDTH:])

    u = _gelu(u_raw)
    r = lax.broadcasted_iota(jnp.int32, (GM_CHUNK, GM_CHUNK), 0)
    c = lax.broadcasted_iota(jnp.int32, (GM_CHUNK, GM_CHUNK), 1)
    w_sp = [jnp.where(r >= c, ws_ref[g], 0.0).astype(BF16) for g in range(GM_GROUPS)]
    group = lax.broadcasted_iota(jnp.int32, (GM_CHUNK, GM_WIDTH), 1) // HEAD_DIM
    for ch in range(tm // GM_CHUNK):
        t0, t1 = ch * GM_CHUNK, (ch + 1) * GM_CHUNK
        s = bs_ref[...]
        for g in range(GM_GROUPS):
            s = s + jnp.where(group == g, _dot(w_sp[g], vn[t0:t1]), 0.0)
        ya_ref[t0:t1, :] = (u[t0:t1] * s).astype(ya_ref.dtype)

    zt = _dot_nt(wf_ref[...], h_ref[...])
    heads_t(dvt_ref, zt[FEAT_DV:FEAT_DV + DA_WIDTH])
    heads_t(vst_ref, zt[FEAT_VS:FEAT_VS + NSA_KV_WIDTH])
    heads_t(vwt_ref, zt[FEAT_VW:FEAT_VW + NSA_KV_WIDTH])
    for g in range(NSA_KV_GROUPS):
        ngt_ref[g] = jax.nn.sigmoid(zt[FEAT_NG + g * GATE_ROWS:FEAT_NG + (g + 1) * GATE_ROWS])
    kv = mm(TOK_KC, 2 * NSA_KV_WIDTH)
    heads(kc_ref, kv[:, :NSA_KV_WIDTH])
    heads(vc_ref, kv[:, NSA_KV_WIDTH:])


def _inproj(x, mod, g, w_tok, w_feat, layer, tabs, ln_g, w_s, b_s):
    B, S, D = x.shape
    assert w_tok.shape[1:] == (D, TOK_WIDTH) and w_feat.shape[1:] == (FEAT_ROWS, D)
    tm = ATT_TILE
    nt = S // tm

    def hm(nh):
        return (jax.ShapeDtypeStruct((B, nh, S, HEAD_DIM), BF16),
                pl.BlockSpec((None, nh, tm, HEAD_DIM), lambda b, i: (b, 0, i, 0)))

    def hm_t(nh):
        return (jax.ShapeDtypeStruct((B, nh, nt, HEAD_DIM, tm), BF16),
                pl.BlockSpec((None, nh, None, HEAD_DIM, tm), lambda b, i: (b, 0, i, 0, 0)))

    G = NSA_KV_GROUPS
    outs = [(jax.ShapeDtypeStruct((B, S, GM_WIDTH), BF16), pl.BlockSpec((None, tm, GM_WIDTH), lambda b, i: (b, i, 0))),
            hm(DA_HEADS), hm(DA_HEADS), hm_t(DA_HEADS), hm(NSA_HEADS),
            hm(G), hm(G), hm(G), hm_t(G), hm(G), hm_t(G),
            (jax.ShapeDtypeStruct((B, G, GATE_ROWS, S), F32),
             pl.BlockSpec((None, G, GATE_ROWS, tm), lambda b, i: (b, 0, 0, i)))]
    bias = jnp.repeat(b_s.T, HEAD_DIM, axis=1)
    lane_group = np.arange(GM_WIDTH) // HEAD_DIM
    avg = jnp.asarray((lane_group[:, None] == lane_group[None, :]) / HEAD_DIM, BF16)
    full = lambda a: pl.BlockSpec(a.shape, lambda b, i: (0,) * a.ndim)
    return pl.pallas_call(
        _inproj_kernel,
        grid=(B, nt),
        in_specs=[
            pl.BlockSpec((None, tm, D), lambda b, i: (b, i, 0)),
            pl.BlockSpec((None, N_MOD, D), lambda b, i: (b, 0, 0)),
            pl.BlockSpec((1, D), lambda b, i: (0, 0)),
            _resident((None,) + w_tok.shape[1:], lambda b, i: (layer, 0, 0)),
            _resident((None,) + w_feat.shape[1:], lambda b, i: (layer, 0, 0)),
            pl.BlockSpec((tabs.shape[0], tm, LANE), lambda b, i: (0, i, 0)),
            pl.BlockSpec((1, GM_WIDTH), lambda b, i: (0, 0)), full(w_s), full(bias), full(avg),
        ],
        out_specs=[o[1] for o in outs],
        out_shape=[o[0] for o in outs],
        scratch_shapes=[pltpu.VMEM((tm, D), BF16)],
        compiler_params=_params(("arbitrary", "arbitrary")),
        name="inproj",
    )(x, mod, g.reshape(1, D), w_tok, w_feat, tabs, ln_g.reshape(1, GM_WIDTH), w_s, bias, avg)


DA_AHEAD, NSA_AHEAD = 6, 5


SUM_ROWS = 16


def _online_steps(q_of, steps, m_ref, acc_ref, ahead):
    n = len(steps)
    keys = steps[0][3].shape[1]
    ones = jnp.where(lax.broadcasted_iota(jnp.int32, (SUM_ROWS, keys), 0) == 0, 1.0, 0.0).astype(BF16)

    def scores(t):
        _, qc, k, _, bias = steps[t]
        s = _dot_nt(k, q_of(qc))
        return s if bias is None else s + bias()

    s = [scores(t) if t < ahead else None for t in range(n)]
    for t in range(n):
        c, _, _, vt, _ = steps[t]
        m_old = m_ref[c]
        m_new = jnp.maximum(m_old, jnp.max(s[t], axis=0, keepdims=True))
        alpha = jnp.exp2(m_old - m_new)
        p = jnp.exp2(s[t] - m_new).astype(BF16)
        s[t] = None
        m_ref[c] = m_new
        if t + ahead < n:
            s[t + ahead] = scores(t + ahead)
        acc_ref[c] = alpha * acc_ref[c] + _dot(jnp.concatenate([vt, ones], axis=0), p)


def _normalized(c0, n, acc_ref):
    dv = acc_ref.shape[1] - SUM_ROWS
    parts = [acc_ref[c, :dv] * (1.0 / acc_ref[c, dv:dv + 1]) for c in range(c0, c0 + n)]
    return parts[0] if n == 1 else jnp.concatenate(parts, axis=1)


def _online_init(m_ref, acc_ref):
    m_ref[...] = jnp.full_like(m_ref, NEG)
    acc_ref[...] = jnp.zeros_like(acc_ref)


def _causal_bias(T, upper):
    key = lax.broadcasted_iota(jnp.int32, (T, T), 0)
    qry = lax.broadcasted_iota(jnp.int32, (T, T), 1)
    keep = (key > qry) if upper else (key <= qry)
    return jnp.where(keep, 0.0, NEG)


def _da_kernel(q_ref, k_ref, vt_ref, lam_ref, subg_ref, o_ref, q_sc, bias_sc, m_ref, acc_ref, *, lam_init):
    i = pl.program_id(2)
    HP, T, Dh = q_ref.shape
    cpt = T // CHUNK
    lf = lam_ref[...]
    lam = (jnp.exp(jnp.sum(lf[0:1] * lf[1:2], axis=-1, keepdims=True))
           - jnp.exp(jnp.sum(lf[2:3] * lf[3:4], axis=-1, keepdims=True)) + lam_init)
    first = lax.broadcasted_iota(jnp.int32, (1, Dh), 1) < DA_QK_DIM
    zero = jnp.zeros((), BF16)
    for hd in range(HP):
        q = q_ref[hd]
        q_sc[(2 * hd) * T:(2 * hd + 1) * T, :] = jnp.where(first, q, zero)
        q_sc[(2 * hd + 1) * T:(2 * hd + 2) * T, :] = jnp.where(first, zero, q)
    @pl.when(i == 0)
    def _():
        bias_sc[...] = _causal_bias(T, upper=False)

    _online_init(m_ref, acc_ref)

    def tile(j, masked):
        start = pl.multiple_of(j * T, T)
        steps = []
        for hd in range(HP):
            k = k_ref[hd, pl.ds(start, T), :]
            vt = vt_ref[hd, j]
            for cc in range(2 * cpt):
                lo = (cc % cpt) * CHUNK
                bias = (lambda lo=lo: bias_sc[:, lo:lo + CHUNK]) if masked else None
                c = hd * 2 * cpt + cc
                steps.append((c, c, k, vt, bias))
        return steps

    def run(steps):
        _online_steps(lambda c: q_sc[c * CHUNK:(c + 1) * CHUNK, :], steps, m_ref, acc_ref, DA_AHEAD)

    def body(jj, carry):
        run(tile(2 * jj, False) + tile(2 * jj + 1, False))
        return carry

    lax.fori_loop(0, i // 2, body, 0)

    @pl.when(i % 2 == 0)
    def _():
        run(tile(i, True))

    @pl.when(i % 2 == 1)
    def _():
        run(tile(i - 1, False) + tile(i, True))

    outs = []
    for hd in range(HP):
        o1 = _normalized(2 * hd * cpt, cpt, acc_ref)
        o2 = _normalized((2 * hd + 1) * cpt, cpt, acc_ref)
        o = o1 - lam * o2
        o = o * lax.rsqrt(jnp.mean(o * o, axis=0, keepdims=True) + EPS) * subg_ref[...] * (1.0 - lam_init)
        outs.append(o)
    o_ref[...] = jnp.concatenate(outs, axis=0).T.astype(o_ref.dtype)


def _diff_attn(dq, dk, dvt, da_lambda, sub_g, lam_init):
    B, H, S, Dh = dq.shape
    T = ATT_TILE
    HP = H
    rows = HP * 2 * T
    return pl.pallas_call(
        functools.partial(_da_kernel, lam_init=lam_init),
        grid=(B, H // HP, S // T),
        in_specs=[
            pl.BlockSpec((None, HP, T, Dh), lambda b, h, i: (b, h, i, 0)),
            pl.BlockSpec((None, HP, S, Dh), lambda b, h, i: (b, h, 0, 0)),
            pl.BlockSpec((None, HP, S // T, Dh, T), lambda b, h, i: (b, h, 0, 0, 0)),
            pl.BlockSpec(da_lambda.shape, lambda b, h, i: (0, 0)),
            pl.BlockSpec((Dh, T), lambda b, h, i: (0, 0)),
        ],
        out_specs=pl.BlockSpec((None, T, HP * Dh), lambda b, h, i: (b, i, h)),
        out_shape=jax.ShapeDtypeStruct((B, S, H * Dh), BF16),
        scratch_shapes=[pltpu.VMEM((rows, Dh), BF16), pltpu.VMEM((T, T), F32),
                        pltpu.VMEM((rows // CHUNK, 1, CHUNK), F32),
                        pltpu.VMEM((rows // CHUNK, Dh + SUM_ROWS, CHUNK), F32)],
        compiler_params=_params(("arbitrary", "arbitrary", "arbitrary")),
        name="diff_attn",
    )(dq, dk, dvt, da_lambda, jnp.broadcast_to(sub_g[:, None], (Dh, T)))


def _compress_kernel(kc_ref, vc_ref, w1_ref, pe_ref, w2_ref, w2rot_ref, tab_ref, ko_ref, vo_ref):
    rows = kc_ref.shape[0]
    G, _, R = vo_ref.shape
    half = w1_ref.shape[1] // 2

    def hidden(x_ref, t):
        xr = x_ref[...]
        top = _dot(xr, w1_ref[t, :half, :])
        bot = _dot(xr, w1_ref[t, half:, :])
        pe_rows = jnp.broadcast_to(pe_ref[t], (SUBLANE, 2 * half)).astype(BF16)
        pe = _dot(pe_rows, w1_ref[t])[0:1]
        return _silu(top + pltpu.roll(bot, rows - 1, 0) + pe).astype(BF16)

    ak = hidden(kc_ref, 0)
    ko = _dot(ak, w2_ref[0]) * tab_ref[0] + _dot(ak, w2rot_ref[...]) * tab_ref[1]
    vo = _dot(hidden(vc_ref, 1), w2_ref[1])
    ko_ref[...] = ko.astype(BF16)
    vot = jnp.concatenate([ko, vo], axis=1).T[HEAD_DIM:].astype(BF16)
    for g in range(G):
        vo_ref[g] = vot[:, g * R:(g + 1) * R]


def _compress(kc, vc, w1, pe, w2, w2rot, tab):
    B, G, S, Dh = kc.shape
    R = S // CMP_STRIDE
    W = CMP_STRIDE * Dh
    blk = pl.BlockSpec((None, G * R, W), lambda b: (b, 0, 0))
    full = lambda a: pl.BlockSpec(a.shape, lambda b: (0,) * a.ndim)
    tab = jnp.tile(tab, (1, G, 1))
    kcmp, vcmpt = pl.pallas_call(
        _compress_kernel,
        grid=(B,),
        in_specs=[blk, blk, full(w1), full(pe), full(w2), full(w2rot), full(tab)],
        out_specs=[pl.BlockSpec((None, G * R, Dh), lambda b: (b, 0, 0)),
                   pl.BlockSpec((None, G, Dh, R), lambda b: (b, 0, 0, 0))],
        out_shape=[jax.ShapeDtypeStruct((B, G * R, Dh), BF16), jax.ShapeDtypeStruct((B, G, Dh, R), BF16)],
        compiler_params=_params(("arbitrary",)),
        name="nsa_compress",
    )(kc.reshape(B, G * R, W), vc.reshape(B, G * R, W), w1, pe, w2, w2rot, tab)
    return kcmp.reshape(B, G, R, Dh), vcmpt


def _cmp_kernel(q_ref, kc_ref, vct_ref, ovt_ref, o_ref, sel_ref, *, ns, topk):
    i = pl.program_id(1)
    H, T, Dh = q_ref.shape
    G, R, _ = kc_ref.shape
    Hg = H // G
    NB = sel_ref.shape[1]
    pos = i * T + lax.broadcasted_iota(jnp.int32, (R, T), 1)
    cend = lax.broadcasted_iota(jnp.int32, (R, T), 0) * CMP_STRIDE + (CMP_LEN - 1)
    mask = cend <= pos
    seen = i * T + lax.broadcasted_iota(jnp.int32, (1, T), 1) >= CMP_LEN - 1
    scores = [_dot_nt(kc_ref[hd // Hg], q_ref[hd]) for hd in range(H)]
    blk = lax.broadcasted_iota(jnp.int32, (NB, T), 0)
    cur = (i * T + lax.broadcasted_iota(jnp.int32, (NB, T), 1)) // SLC_LEN
    valid = blk <= cur
    forced = valid & ((blk == 0) | (blk >= cur - 1))
    ovt = ovt_ref[...]

    for g in range(G):
        ps = jnp.zeros((R, T), F32)
        for hd in range(g * Hg, (g + 1) * Hg):
            s = jnp.where(mask, scores[hd], NEG)
            e = jnp.exp2(s - jnp.max(s, axis=0, keepdims=True))
            p = e * jnp.where(seen, 1.0 / jnp.sum(e, axis=0, keepdims=True), 0.0)
            o_ref[hd] = _dot(vct_ref[g], p.astype(BF16))
            ps = ps + p

        p_hi = ps.astype(BF16)
        p_lo = (ps - p_hi.astype(F32)).astype(BF16)
        imp = _dot(ovt, p_hi) + _dot(ovt, p_lo)
        score = jnp.where(forced, FORCE_BONUS, jnp.where(valid, imp, -1.0))
        score = jnp.where(blk < ns, score, -2.0)
        rows = [score[j:j + 1, :] for j in range(ns)]
        ranks = []
        for lo in range(0, NB, SUBLANE):
            sc = score[lo:lo + SUBLANE, :]
            blk_g = lo + lax.broadcasted_iota(jnp.int32, (SUBLANE, T), 0)
            rank = jnp.zeros((SUBLANE, T), F32)
            for j in range(ns):
                if j < lo:
                    ahead = rows[j] >= sc
                elif j >= lo + SUBLANE:
                    ahead = rows[j] > sc
                else:
                    ahead = (rows[j] > sc) | ((rows[j] == sc) & (blk_g > j))
                rank = rank + jnp.where(ahead, 1.0, 0.0)
            ranks.append(rank)
        rank = jnp.concatenate(ranks, axis=0)
        sel_ref[g] = jnp.where((rank < topk) & (score >= 0.0), 1.0, 0.0).astype(sel_ref.dtype)


def _cmp_attn(nq, kcmp, vcmpt, ovt, ns, topk):
    B, H, S, Dh = nq.shape
    G = kcmp.shape[1]
    R = kcmp.shape[2]
    NB = ovt.shape[0]
    T = ATT_TILE
    return pl.pallas_call(
        functools.partial(_cmp_kernel, ns=ns, topk=topk),
        grid=(B, S // T),
        in_specs=[pl.BlockSpec((None, H, T, Dh), lambda b, i: (b, 0, i, 0)),
                  pl.BlockSpec((None, G, R, Dh), lambda b, i: (b, 0, 0, 0)),
                  pl.BlockSpec((None, G, Dh, R), lambda b, i: (b, 0, 0, 0)),
                  pl.BlockSpec(ovt.shape, lambda b, i: (0, 0))],
        out_specs=[pl.BlockSpec((None, H, Dh, T), lambda b, i: (b, 0, 0, i)),
                   pl.BlockSpec((None, G, NB, T), lambda b, i: (b, 0, 0, i))],
        out_shape=[jax.ShapeDtypeStruct((B, H, Dh, S), F32), jax.ShapeDtypeStruct((B, G, NB, S), BF16)],
        compiler_params=_params(("arbitrary", "arbitrary")),
        name="nsa_cmp",
    )(nq, kcmp, vcmpt, ovt)


def _nsa_kernel(q_ref, ks_ref, vst_ref, kw_ref, vwt_ref, oct_ref, sel_ref, ngt_ref, ext_ref, o_ref,
                bias_sc, m_ref, acc_ref):
    i = pl.program_id(1)
    H, T, Dh = q_ref.shape
    G = ks_ref.shape[0]
    Hg = H // G
    cpt = T // CHUNK
    n = Hg * cpt
    back = WIN // T
    LOWER, UPPER = 2 * G, 2 * G + 1
    @pl.when(i == 0)
    def _():
        bias_sc[LOWER] = _causal_bias(T, upper=False)
        bias_sc[UPPER] = _causal_bias(T, upper=True)

    _online_init(m_ref, acc_ref)

    def tile(g, k_ref, vt_ref, j, slot, window):
        k = k_ref[g, pl.ds(pl.multiple_of(j * T, T), T), :]
        vt = vt_ref[g, j]
        steps = []
        for c in range(n):
            lo = (c % cpt) * CHUNK
            bias = None if slot is None else (lambda lo=lo: bias_sc[slot, :, lo:lo + CHUNK])
            steps.append((2 * n * g + (n if window else 0) + c, g * n + c, k, vt, bias))
        return steps

    def run(steps):
        q_of = lambda c: q_ref[c // cpt, (c % cpt) * CHUNK:(c % cpt + 1) * CHUNK, :]
        _online_steps(q_of, steps, m_ref, acc_ref, NSA_AHEAD)

    def slc_tile(g, which, j, diagonal):
        slot = 2 * g + which
        chosen = _dot(ext_ref[j], sel_ref[g]) > 0.5
        bias_sc[slot] = jnp.where(chosen, _causal_bias(T, upper=False) if diagonal else 0.0, NEG)
        return tile(g, ks_ref, vst_ref, j, slot, False)

    def both(fn):
        steps = []
        for g in range(G):
            steps = steps + fn(g)
        return steps

    def slc_body(jj, carry):
        run(both(lambda g: slc_tile(g, 0, 2 * jj, False) + slc_tile(g, 1, 2 * jj + 1, False)))
        return carry

    lax.fori_loop(0, i // 2, slc_body, 0)

    def last_steps(g, odd, n_back):
        steps = (slc_tile(g, 0, i - 1, False) + slc_tile(g, 1, i, True)) if odd else slc_tile(g, 0, i, True)
        steps = steps + tile(g, kw_ref, vwt_ref, i, LOWER, True)
        for d in range(1, n_back + 1):
            steps = steps + tile(g, kw_ref, vwt_ref, i - d, UPPER if d == back else None, True)
        return steps

    for n_back in range(back):
        @pl.when(i == n_back)
        def _():
            run(both(lambda g: last_steps(g, n_back % 2, n_back)))

    for odd in (0, 1):
        @pl.when((i >= back) & (i % 2 == odd))
        def _():
            run(both(lambda g: last_steps(g, odd, back)))

    outs = []
    for hd in range(H):
        g, c0 = hd // Hg, 2 * n * (hd // Hg) + (hd % Hg) * cpt
        gates = ngt_ref[g]
        r = 3 * (hd % Hg)
        o = (gates[r:r + 1, :] * oct_ref[hd]
             + gates[r + 1:r + 2, :] * _normalized(c0, cpt, acc_ref)
             + gates[r + 2:r + 3, :] * _normalized(c0 + n, cpt, acc_ref))
        outs.append(o)
    for pr in range(H // 2):
        o_ref[:, pr * LANE:(pr + 1) * LANE] = jnp.concatenate(outs[2 * pr:2 * pr + 2], axis=0).T.astype(o_ref.dtype)


def _nsa_main(nq, ks, vst, kw, vwt, o_cmpt, sel, ngt, expand_t):
    B, H, S, Dh = nq.shape
    G = ks.shape[1]
    T = ATT_TILE
    assert WIN % T == 0 and (H // G) % 2 == 0
    rows = H * T
    kspec = pl.BlockSpec((None, G, S, Dh), lambda b, i: (b, 0, 0, 0))
    vspec = pl.BlockSpec((None, G, S // T, Dh, T), lambda b, i: (b, 0, 0, 0, 0))
    NB = sel.shape[2]
    return pl.pallas_call(
        _nsa_kernel,
        grid=(B, S // T),
        in_specs=[pl.BlockSpec((None, H, T, Dh), lambda b, i: (b, 0, i, 0)),
                  kspec, vspec, kspec, vspec,
                  pl.BlockSpec((None, H, Dh, T), lambda b, i: (b, 0, 0, i)),
                  pl.BlockSpec((None, G, NB, T), lambda b, i: (b, 0, 0, i)),
                  pl.BlockSpec((None, G, GATE_ROWS, T), lambda b, i: (b, 0, 0, i)),
                  pl.BlockSpec(expand_t.shape, lambda b, i: (0, 0, 0))],
        out_specs=pl.BlockSpec((None, T, H * Dh), lambda b, i: (b, i, 0)),
        out_shape=jax.ShapeDtypeStruct((B, S, H * Dh), BF16),
        scratch_shapes=[pltpu.VMEM((2 * G + 2, T, T), F32),
                        pltpu.VMEM((2 * rows // CHUNK, 1, CHUNK), F32),
                        pltpu.VMEM((2 * rows // CHUNK, Dh + SUM_ROWS, CHUNK), F32)],
        compiler_params=_params(("arbitrary", "arbitrary")),
        name="nsa_main",
    )(nq, ks, vst, kw, vwt, o_cmpt, sel, ngt, expand_t)


def _rope_table(pos, period, width):
    half = period // 2
    local = np.arange(width) % period
    inv = ROPE_THETA ** (-(local % half).astype(np.float32) / half)
    ang = pos.astype(jnp.float32)[:, None] * jnp.asarray(inv, F32)[None, :]
    return [jnp.cos(ang), jnp.sin(ang)]


def _rope_tables(S):
    pos = jnp.arange(S)
    tabs = _rope_table(pos, DA_QK_DIM, LANE) + _rope_table(pos, HEAD_DIM, LANE)
    cmp_end = jnp.arange(S // CMP_STRIDE) * CMP_STRIDE + CMP_LEN - 1
    return jnp.stack(tabs), jnp.stack(_rope_table(cmp_end, HEAD_DIM, HEAD_DIM))


def _rot_cols(w, period):
    d, n = w.shape
    half = period // 2
    c = w.reshape(d, n // period, period)
    return jnp.concatenate([-c[..., half:], c[..., :half]], axis=-1).reshape(d, n)


def _layout_w_in(w):
    depth, d, _ = w.shape
    cols = lambda off, n: w[:, :, off:off + n]
    kvw = NSA_KV_WIDTH
    w_tok = jnp.concatenate(
        [cols(OFF_GU, GM_WIDTH), cols(OFF_GV, GM_WIDTH), cols(OFF_DQ, 256), cols(OFF_DK, 256), cols(OFF_NQ, NSA_WIDTH),
         cols(OFF_KC, kvw), cols(OFF_VC, kvw), cols(OFF_KS, kvw), cols(OFF_KW, kvw)], axis=2)
    per_group = N_GATES // NSA_KV_GROUPS
    pad = jnp.zeros((depth, d, GATE_ROWS - per_group), w.dtype)
    feat = [cols(OFF_DV, DA_WIDTH), cols(OFF_VS, kvw), cols(OFF_VW, kvw)]
    for g in range(NSA_KV_GROUPS):
        feat += [cols(OFF_NG + g * per_group, per_group), pad]
    return w_tok.astype(BF16), jnp.swapaxes(jnp.concatenate(feat, axis=2), 1, 2).astype(BF16)


def _overlap_table_t(S, nb):
    nc = (S - CMP_LEN) // CMP_STRIDE + 1
    ns = S // SLC_LEN
    cs = np.arange(nc) * CMP_STRIDE
    bs = np.arange(ns) * SLC_LEN
    ov = np.clip(np.minimum(cs[:, None] + CMP_LEN, bs[None, :] + SLC_LEN) - np.maximum(cs[:, None], bs[None, :]), 0, None)
    out = np.zeros((nb, S // CMP_STRIDE), np.float32)
    out[:ns, :nc] = (ov / CMP_STRIDE).T
    return jnp.asarray(out, BF16)


def _expand_table_t(S, T, nb):
    key_block = np.arange(S).reshape(S // T, T, 1) // SLC_LEN
    return jnp.asarray(key_block == np.arange(nb).reshape(1, 1, nb), BF16)


def kernel(x, c, w_ada, b_ada, norm_g, ffn_w_gate, ffn_w_up, ffn_w_down, w_in, w_out,
           gm_ln_g, gm_w_s, gm_b_s, da_lambda, da_sub_g, nsa_cmp_pe, nsa_cmp_w1, nsa_cmp_w2, final_g):
    B, S, D = x.shape
    depth = w_in.shape[0]
    ns = S // SLC_LEN
    topk = min(SLC_TOPK, ns)
    nb = -(-ns // 16) * 16
    assert S % 512 == 0 and ATT_TILE % CHUNK == 0 and CHUNK % LANE == 0 and ffn_w_gate.shape[-1] % FF_CHUNK == 0

    mod_all = _adaln(c, w_ada, b_ada).reshape(depth, B, N_MOD, D)
    tabs, cmp_tab = _rope_tables(S)
    ovt = _overlap_table_t(S, nb)
    expand_t = _expand_table_t(S, ATT_TILE, nb)

    bf = lambda w: w.astype(BF16)
    wg, wu, wd, wo = bf(ffn_w_gate), bf(ffn_w_up), bf(ffn_w_down), bf(w_out)
    w_tok, w_feat = _layout_w_in(w_in)

    for l in range(depth):
        mod = mod_all[l]
        lam_init = 0.8 - 0.6 * math.exp(-0.3 * l)
        x = _ffn(x, mod, norm_g[l, 0], wg, wu, wd, (l, 0), rows=(0, 1, 2))

        (y_a, dq, dk, dvt, nq, kc, vc, ks, vst, kw, vwt, ngt) = _inproj(
            x, mod, norm_g[l, 1], w_tok, w_feat, l, tabs, gm_ln_g[l], gm_w_s[l], gm_b_s[l])
        y_b = _diff_attn(dq, dk, dvt, da_lambda[l], da_sub_g[l], lam_init)
        kcmp, vcmpt = _compress(
            kc, vc, bf(nsa_cmp_w1[l]), nsa_cmp_pe[l].reshape(2, 1, CMP_LEN * HEAD_DIM),
            bf(nsa_cmp_w2[l]), bf(_rot_cols(nsa_cmp_w2[l, 0], HEAD_DIM)), cmp_tab)
        o_cmpt, sel = _cmp_attn(nq, kcmp, vcmpt, ovt, ns, topk)
        y_c = _nsa_main(nq, ks, vst, kw, vwt, o_cmpt, sel, ngt, expand_t)

        x = _ffn(x, mod, norm_g[l, 2], wg, wu, wd, (l, 1), rows=(6, 7, 8),
                 mix=(y_a, y_b, y_c, wo), final_g=final_g if l == depth - 1 else None)
    return x
```

```python
import functools
import math

import numpy as np
import jax
import jax.numpy as jnp
from jax import lax
from jax.experimental import pallas as pl
from jax.experimental.pallas import tpu as pltpu

F32 = jnp.float32
BF16 = jnp.bfloat16

HEAD_DIM = 64
GM_GROUPS = 4
GM_CHUNK = 128
GM_WIDTH = GM_GROUPS * HEAD_DIM
DA_HEADS = 4
DA_QK_DIM = HEAD_DIM // 2
DA_WIDTH = DA_HEADS * HEAD_DIM
NSA_HEADS = 8
NSA_KV_GROUPS = 2
NSA_WIDTH = NSA_HEADS * HEAD_DIM
NSA_KV_WIDTH = NSA_KV_GROUPS * HEAD_DIM
N_GATES = 3 * NSA_HEADS
GATE_ROWS = 16
CMP_LEN = 32
CMP_STRIDE = 16
SLC_LEN = 64
SLC_TOPK = 16
WIN = 512
FORCE_BONUS = 1000.0
ROPE_THETA = 10000.0
EPS = 1e-6
N_MOD = 9
NEG = -1e30
LOG2E = math.log2(math.e)

LANE = 128
SUBLANE = 8
FF_CHUNK = 256
ATT_TILE = 256
CHUNK = 256
VMEM_LIMIT = 56 * 1024 * 1024

OFF_GU, OFF_GV, OFF_DQ, OFF_DK, OFF_DV = 0, 256, 512, 768, 1024
OFF_NQ, OFF_KC, OFF_VC, OFF_KS, OFF_VS, OFF_KW, OFF_VW, OFF_NG = 1280, 1792, 1920, 2048, 2176, 2304, 2432, 2560
TOK_GU, TOK_GV, TOK_DQ, TOK_DK, TOK_NQ, TOK_KC, TOK_KS, TOK_WIDTH = 0, 256, 512, 768, 1024, 1536, 1792, 2048
FEAT_DV, FEAT_VS, FEAT_VW, FEAT_NG = 0, 256, 384, 512
FEAT_ROWS = FEAT_NG + NSA_KV_GROUPS * GATE_ROWS


def _params(sem):
    return pltpu.CompilerParams(dimension_semantics=sem, vmem_limit_bytes=VMEM_LIMIT)


def _resident(shape, index_map):
    return pl.BlockSpec(shape, index_map, pipeline_mode=pl.Buffered(1))


def _dot(a, b):
    return jnp.dot(a, b, preferred_element_type=F32)


def _dot_nt(a, b):
    return lax.dot_general(a, b, (((1,), (1,)), ((), ())), preferred_element_type=F32)


def _mod_norm(x, g, shift, scale):
    ms = jnp.mean(x * x, axis=-1, keepdims=True)
    return x * lax.rsqrt(ms + EPS) * g * (1.0 + scale) + shift


def _gelu(x):
    c = math.sqrt(2.0 / math.pi)
    return 0.5 * x * (1.0 + jnp.tanh(c * (x + 0.044715 * (x * x * x))))


def _silu(x):
    return x * jax.nn.sigmoid(x)


def _adaln_kernel(c_ref, w_ref, b_ref, o_ref):
    ca = _silu(c_ref[...])
    o_ref[...] = _dot(ca, w_ref[...]) + b_ref[...]


def _adaln(c, w_ada, b_ada):
    L, D, ND = w_ada.shape
    B = c.shape[0]
    tn = ND // 8
    return pl.pallas_call(
        _adaln_kernel,
        grid=(L, ND // tn),
        in_specs=[
            pl.BlockSpec((B, D), lambda l, j: (0, 0)),
            pl.BlockSpec((None, D, tn), lambda l, j: (l, 0, j)),
            pl.BlockSpec((None, 1, tn), lambda l, j: (l, 0, j)),
        ],
        out_specs=pl.BlockSpec((None, B, tn), lambda l, j: (l, 0, j)),
        out_shape=jax.ShapeDtypeStruct((L, B, ND), F32),
        compiler_params=_params(("arbitrary", "arbitrary")),
        name="adaln",
    )(c, w_ada, b_ada.reshape(L, 1, ND))


def _ffn_kernel(*refs, rows, nchunk, mixed, final):
    refs = list(refs)
    x_ref, mod_ref, g_ref, wg_ref, wu_ref, wd_ref = refs[:6]
    rest = refs[6:]
    if mixed:
        ya_ref, yb_ref, yc_ref, wo_ref = rest[:4]
        rest = rest[4:]
    if final:
        fg_ref = rest[0]
        rest = rest[1:]
    o_ref, h_ref, a_ref = rest[:3]
    r_sh, r_sc, r_gt = rows
    if mixed:
        x_sc = rest[3]
        y = _dot(ya_ref[...], wo_ref[0:GM_WIDTH, :])
        y += _dot(yb_ref[...], wo_ref[GM_WIDTH:GM_WIDTH + DA_WIDTH, :])
        y += _dot(yc_ref[...], wo_ref[GM_WIDTH + DA_WIDTH:, :])
        x_sc[...] = x_ref[...] + mod_ref[5:6, :] * y
        x_ref = x_sc
    h_ref[...] = _mod_norm(x_ref[...], g_ref[...], mod_ref[r_sh:r_sh + 1, :], mod_ref[r_sc:r_sc + 1, :]).astype(BF16)
    for j in range(nchunk):
        cols = slice(j * FF_CHUNK, (j + 1) * FF_CHUNK)
        gate = _dot(h_ref[...], wg_ref[:, cols])
        up = _dot(h_ref[...], wu_ref[:, cols])
        a_ref[:, cols] = (_silu(gate) * up).astype(BF16)
    y = x_ref[...] + 0.5 * mod_ref[r_gt:r_gt + 1, :] * _dot(a_ref[...], wd_ref[...])
    if final:
        y = y * lax.rsqrt(jnp.mean(y * y, axis=-1, keepdims=True) + EPS) * fg_ref[...]
    o_ref[...] = y


def _ffn(x, mod, g, wg, wu, wd, which, rows, mix=None, final_g=None, tm=512):
    B, S, D = x.shape
    F = wd.shape[-2]
    mixed = mix is not None
    final = final_g is not None
    tok = lambda w: pl.BlockSpec((None, tm, w), lambda b, i: (b, i, 0))
    row = pl.BlockSpec((1, D), lambda b, i: (0, 0))
    stacked = lambda w: _resident((None, None) + w.shape[2:], lambda b, i: which + (0, 0))
    in_specs = [tok(D), pl.BlockSpec((None, N_MOD, D), lambda b, i: (b, 0, 0)), row,
                stacked(wg), stacked(wu), stacked(wd)]
    args = [x, mod, g.reshape(1, D), wg, wu, wd]
    scratch = [pltpu.VMEM((tm, D), BF16), pltpu.VMEM((tm, F), BF16)]
    if mixed:
        ya, yb, yc, w_out = mix
        in_specs += [tok(ya.shape[-1]), tok(yb.shape[-1]), tok(yc.shape[-1]),
                     _resident((None,) + w_out.shape[1:], lambda b, i: (which[0], 0, 0))]
        args += [ya, yb, yc, w_out]
        scratch.append(pltpu.VMEM((tm, D), F32))
    if final:
        in_specs.append(row)
        args.append(final_g.reshape(1, D))
    return pl.pallas_call(
        functools.partial(_ffn_kernel, rows=rows, nchunk=F // FF_CHUNK, mixed=mixed, final=final),
        grid=(B, S // tm),
        in_specs=in_specs,
        out_specs=tok(D),
        out_shape=jax.ShapeDtypeStruct((B, S, D), F32),
        scratch_shapes=scratch,
        compiler_params=_params(("arbitrary", "arbitrary")),
        name="ffn" + ("_mix" if mixed else "") + ("_final" if final else ""),
    )(*args)


def _inproj_kernel(x_ref, mod_ref, g_ref, w_ref, wf_ref, tab_ref, lng_ref, ws_ref, bs_ref, avg_ref,
                   ya_ref, dq_ref, dk_ref, dvt_ref, nq_ref,
                   kc_ref, vc_ref, ks_ref, vst_ref, kw_ref, vwt_ref, ngt_ref, h_ref):
    h_ref[...] = _mod_norm(x_ref[...], g_ref[...], mod_ref[3:4, :], mod_ref[4:5, :]).astype(BF16)
    tm = h_ref.shape[0]

    def mm(off, width):
        return _dot(h_ref[...], w_ref[:, off:off + width])

    def heads(ref, z):
        for hd in range(z.shape[1] // HEAD_DIM):
            ref[hd] = z[:, hd * HEAD_DIM:(hd + 1) * HEAD_DIM].astype(ref.dtype)

    def heads_t(ref, zt):
        for hd in range(zt.shape[0] // HEAD_DIM):
            ref[hd] = zt[hd * HEAD_DIM:(hd + 1) * HEAD_DIM].astype(ref.dtype)

    def rope(z, half, t, scale=1.0):
        cos = tab_ref[t] * scale
        sin = tab_ref[t + 1] * scale
        first = lax.broadcasted_iota(jnp.int32, (1, LANE), 1) % (2 * half) < half
        sin_lo = jnp.where(first, -sin, 0.0)
        sin_hi = jnp.where(first, 0.0, sin)
        blocks = []
        for c in range(0, z.shape[1], LANE):
            zb = z[:, c:c + LANE]
            blocks.append(zb * cos + pltpu.roll(zb, LANE - half, 1) * sin_lo + pltpu.roll(zb, half, 1) * sin_hi)
        return blocks[0] if len(blocks) == 1 else jnp.concatenate(blocks, axis=1)

    avg = avg_ref[...]

    def group_mean(t):
        hi = t.astype(BF16)
        lo = (t - hi.astype(F32)).astype(BF16)
        return _dot(hi, avg) + _dot(lo, avg)

    u_raw = mm(TOK_GU, GM_WIDTH)
    v_raw = mm(TOK_GV, GM_WIDTH)
    heads(dq_ref, rope(mm(TOK_DQ, 256), DA_QK_DIM // 2, 0, DA_QK_DIM ** -0.5 * LOG2E))
    v = _gelu(v_raw)
    d = v - group_mean(v)
    heads(dk_ref, rope(mm(TOK_DK, 256), DA_QK_DIM // 2, 0))
    var = group_mean(d * d)
    heads(nq_ref, rope(mm(TOK_NQ, NSA_WIDTH), HEAD_DIM // 2, 2, HEAD_DIM ** -0.5 * LOG2E))
    vn = (d * lax.rsqrt(var + EPS) * lng_ref[...]).astype(BF16)
    kk = rope(mm(TOK_KS, 2 * NSA_KV_WIDTH), HEAD_DIM // 2, 2)
    heads(ks_ref, kk[:, :NSA_KV_WIDTH])
    heads(kw_ref, kk[:, NSA_KV_WIDTH:])

    u = _gelu(u_raw)
    r = lax.broadcasted_iota(jnp.int32, (GM_CHUNK, GM_CHUNK), 0)
    c = lax.broadcasted_iota(jnp.int32, (GM_CHUNK, GM_CHUNK), 1)
    w_sp = [jnp.where(r >= c, ws_ref[g], 0.0).astype(BF16) for g in range(GM_GROUPS)]
    group = lax.broadcasted_iota(jnp.int32, (GM_CHUNK, GM_WIDTH), 1) // HEAD_DIM
    for ch in range(tm // GM_CHUNK):
        t0, t1 = ch * GM_CHUNK, (ch + 1) * GM_CHUNK
        s = bs_ref[...]
        for g in range(GM_GROUPS):
            s = s + jnp.where(group == g, _dot(w_sp[g], vn[t0:t1]), 0.0)
        ya_ref[t0:t1, :] = (u[t0:t1] * s).astype(ya_ref.dtype)

    zt = _dot_nt(wf_ref[...], h_ref[...])
    heads_t(dvt_ref, zt[FEAT_DV:FEAT_DV + DA_WIDTH])
    heads_t(vst_ref, zt[FEAT_VS:FEAT_VS + NSA_KV_WIDTH])
    heads_t(vwt_ref, zt[FEAT_VW:FEAT_VW + NSA_KV_WIDTH])
    for g in range(NSA_KV_GROUPS):
        ngt_ref[g] = jax.nn.sigmoid(zt[FEAT_NG + g * GATE_ROWS:FEAT_NG + (g + 1) * GATE_ROWS])
    kv = mm(TOK_KC, 2 * NSA_KV_WIDTH)
    heads(kc_ref, kv[:, :NSA_KV_WIDTH])
    heads(vc_ref, kv[:, NSA_KV_WIDTH:])


def _inproj(x, mod, g, w_tok, w_feat, layer, tabs, ln_g, w_s, b_s):
    B, S, D = x.shape
    assert w_tok.shape[1:] == (D, TOK_WIDTH) and w_feat.shape[1:] == (FEAT_ROWS, D)
    tm = ATT_TILE
    nt = S // tm

    def hm(nh):
        return (jax.ShapeDtypeStruct((B, nh, S, HEAD_DIM), BF16),
                pl.BlockSpec((None, nh, tm, HEAD_DIM), lambda b, i: (b, 0, i, 0)))

    def hm_t(nh):
        return (jax.ShapeDtypeStruct((B, nh, nt, HEAD_DIM, tm), BF16),
                pl.BlockSpec((None, nh, None, HEAD_DIM, tm), lambda b, i: (b, 0, i, 0, 0)))

    G = NSA_KV_GROUPS
    outs = [(jax.ShapeDtypeStruct((B, S, GM_WIDTH), BF16), pl.BlockSpec((None, tm, GM_WIDTH), lambda b, i: (b, i, 0))),
            hm(DA_HEADS), hm(DA_HEADS), hm_t(DA_HEADS), hm(NSA_HEADS),
            hm(G), hm(G), hm(G), hm_t(G), hm(G), hm_t(G),
            (jax.ShapeDtypeStruct((B, G, GATE_ROWS, S), F32),
             pl.BlockSpec((None, G, GATE_ROWS, tm), lambda b, i: (b, 0, 0, i)))]
    bias = jnp.repeat(b_s.T, HEAD_DIM, axis=1)
    lane_group = np.arange(GM_WIDTH) // HEAD_DIM
    avg = jnp.asarray((lane_group[:, None] == lane_group[None, :]) / HEAD_DIM, BF16)
    full = lambda a: pl.BlockSpec(a.shape, lambda b, i: (0,) * a.ndim)
    return pl.pallas_call(
        _inproj_kernel,
        grid=(B, nt),
        in_specs=[
            pl.BlockSpec((None, tm, D), lambda b, i: (b, i, 0)),
            pl.BlockSpec((None, N_MOD, D), lambda b, i: (b, 0, 0)),
            pl.BlockSpec((1, D), lambda b, i: (0, 0)),
            _resident((None,) + w_tok.shape[1:], lambda b, i: (layer, 0, 0)),
            _resident((None,) + w_feat.shape[1:], lambda b, i: (layer, 0, 0)),
            pl.BlockSpec((tabs.shape[0], tm, LANE), lambda b, i: (0, i, 0)),
            pl.BlockSpec((1, GM_WIDTH), lambda b, i: (0, 0)), full(w_s), full(bias), full(avg),
        ],
        out_specs=[o[1] for o in outs],
        out_shape=[o[0] for o in outs],
        scratch_shapes=[pltpu.VMEM((tm, D), BF16)],
        compiler_params=_params(("arbitrary", "arbitrary")),
        name="inproj",
    )(x, mod, g.reshape(1, D), w_tok, w_feat, tabs, ln_g.reshape(1, GM_WIDTH), w_s, bias, avg)


DA_AHEAD, NSA_AHEAD = 6, 5


SUM_ROWS = 16


def _online_steps(q_of, steps, m_ref, acc_ref, ahead):
    n = len(steps)
    keys = steps[0][3].shape[1]
    ones = jnp.where(lax.broadcasted_iota(jnp.int32, (SUM_ROWS, keys), 0) == 0, 1.0, 0.0).astype(BF16)

    def scores(t):
        _, qc, k, _, bias = steps[t]
        s = _dot_nt(k, q_of(qc))
        return s if bias is None else s + bias()

    s = [scores(t) if t < ahead else None for t in range(n)]
    for t in range(n):
        c, _, _, vt, _ = steps[t]
        m_old = m_ref[c]
        m_new = jnp.maximum(m_old, jnp.max(s[t], axis=0, keepdims=True))
        alpha = jnp.exp2(m_old - m_new)
        p = jnp.exp2(s[t] - m_new).astype(BF16)
        s[t] = None
        m_ref[c] = m_new
        if t + ahead < n:
            s[t + ahead] = scores(t + ahead)
        acc_ref[c] = alpha * acc_ref[c] + _dot(jnp.concatenate([vt, ones], axis=0), p)


def _normalized(c0, n, acc_ref):
    dv = acc_ref.shape[1] - SUM_ROWS
    parts = [acc_ref[c, :dv] * (1.0 / acc_ref[c, dv:dv + 1]) for c in range(c0, c0 + n)]
    return parts[0] if n == 1 else jnp.concatenate(parts, axis=1)


def _online_init(m_ref, acc_ref):
    m_ref[...] = jnp.full_like(m_ref, NEG)
    acc_ref[...] = jnp.zeros_like(acc_ref)


def _causal_bias(T, upper):
    key = lax.broadcasted_iota(jnp.int32, (T, T), 0)
    qry = lax.broadcasted_iota(jnp.int32, (T, T), 1)
    keep = (key > qry) if upper else (key <= qry)
    return jnp.where(keep, 0.0, NEG)


def _da_kernel(q_ref, k_ref, vt_ref, lam_ref, subg_ref, o_ref, q_sc, bias_sc, m_ref, acc_ref, *, lam_init):
    i = pl.program_id(2)
    HP, T, Dh = q_ref.shape
    cpt = T // CHUNK
    lf = lam_ref[...]
    lam = (jnp.exp(jnp.sum(lf[0:1] * lf[1:2], axis=-1, keepdims=True))
           - jnp.exp(jnp.sum(lf[2:3] * lf[3:4], axis=-1, keepdims=True)) + lam_init)
    first = lax.broadcasted_iota(jnp.int32, (1, Dh), 1) < DA_QK_DIM
    zero = jnp.zeros((), BF16)
    for hd in range(HP):
        q = q_ref[hd]
        q_sc[(2 * hd) * T:(2 * hd + 1) * T, :] = jnp.where(first, q, zero)
        q_sc[(2 * hd + 1) * T:(2 * hd + 2) * T, :] = jnp.where(first, zero, q)
    @pl.when(i == 0)
    def _():
        bias_sc[...] = _causal_bias(T, upper=False)

    _online_init(m_ref, acc_ref)

    def tile(j, masked):
        start = pl.multiple_of(j * T, T)
        steps = []
        for hd in range(HP):
            k = k_ref[hd, pl.ds(start, T), :]
            vt = vt_ref[hd, j]
            for cc in range(2 * cpt):
                lo = (cc % cpt) * CHUNK
                bias = (lambda lo=lo: bias_sc[:, lo:lo + CHUNK]) if masked else None
                c = hd * 2 * cpt + cc
                steps.append((c, c, k, vt, bias))
        return steps

    def run(steps):
        _online_steps(lambda c: q_sc[c * CHUNK:(c + 1) * CHUNK, :], steps, m_ref, acc_ref, DA_AHEAD)

    def body(jj, carry):
        run(tile(2 * jj, False) + tile(2 * jj + 1, False))
        return carry

    lax.fori_loop(0, i // 2, body, 0)

    @pl.when(i % 2 == 0)
    def _():
        run(tile(i, True))

    @pl.when(i % 2 == 1)
    def _():
        run(tile(i - 1, False) + tile(i, True))

    outs = []
    for hd in range(HP):
        o1 = _normalized(2 * hd * cpt, cpt, acc_ref)
        o2 = _normalized((2 * hd + 1) * cpt, cpt, acc_ref)
        o = o1 - lam * o2
        o = o * lax.rsqrt(jnp.mean(o * o, axis=0, keepdims=True) + EPS) * subg_ref[...] * (1.0 - lam_init)
        outs.append(o)
    o_ref[...] = jnp.concatenate(outs, axis=0).T.astype(o_ref.dtype)


def _diff_attn(dq, dk, dvt, da_lambda, sub_g, lam_init):
    B, H, S, Dh = dq.shape
    T = ATT_TILE
    HP = H
    rows = HP * 2 * T
    return pl.pallas_call(
        functools.partial(_da_kernel, lam_init=lam_init),
        grid=(B, H // HP, S // T),
        in_specs=[
            pl.BlockSpec((None, HP, T, Dh), lambda b, h, i: (b, h, i, 0)),
            pl.BlockSpec((None, HP, S, Dh), lambda b, h, i: (b, h, 0, 0)),
            pl.BlockSpec((None, HP, S // T, Dh, T), lambda b, h, i: (b, h, 0, 0, 0)),
            pl.BlockSpec(da_lambda.shape, lambda b, h, i: (0, 0)),
            pl.BlockSpec((Dh, T), lambda b, h, i: (0, 0)),
        ],
        out_specs=pl.BlockSpec((None, T, HP * Dh), lambda b, h, i: (b, i, h)),
        out_shape=jax.ShapeDtypeStruct((B, S, H * Dh), BF16),
        scratch_shapes=[pltpu.VMEM((rows, Dh), BF16), pltpu.VMEM((T, T), F32),
                        pltpu.VMEM((rows // CHUNK, 1, CHUNK), F32),
                        pltpu.VMEM((rows // CHUNK, Dh + SUM_ROWS, CHUNK), F32)],
        compiler_params=_params(("arbitrary", "arbitrary", "arbitrary")),
        name="diff_attn",
    )(dq, dk, dvt, da_lambda, jnp.broadcast_to(sub_g[:, None], (Dh, T)))


def _compress_kernel(kc_ref, vc_ref, w1_ref, pe_ref, w2_ref, w2rot_ref, tab_ref, ko_ref, vo_ref):
    rows = kc_ref.shape[0]
    G, _, R = vo_ref.shape
    half = w1_ref.shape[1] // 2

    def hidden(x_ref, t):
        xr = x_ref[...]
        top = _dot(xr, w1_ref[t, :half, :])
        bot = _dot(xr, w1_ref[t, half:, :])
        pe_rows = jnp.broadcast_to(pe_ref[t], (SUBLANE, 2 * half)).astype(BF16)
        pe = _dot(pe_rows, w1_ref[t])[0:1]
        return _silu(top + pltpu.roll(bot, rows - 1, 0) + pe).astype(BF16)

    ak = hidden(kc_ref, 0)
    ko = _dot(ak, w2_ref[0]) * tab_ref[0] + _dot(ak, w2rot_ref[...]) * tab_ref[1]
    vo = _dot(hidden(vc_ref, 1), w2_ref[1])
    ko_ref[...] = ko.astype(BF16)
    vot = jnp.concatenate([ko, vo], axis=1).T[HEAD_DIM:].astype(BF16)
    for g in range(G):
        vo_ref[g] = vot[:, g * R:(g + 1) * R]


def _compress(kc, vc, w1, pe, w2, w2rot, tab):
    B, G, S, Dh = kc.shape
    R = S // CMP_STRIDE
    W = CMP_STRIDE * Dh
    blk = pl.BlockSpec((None, G * R, W), lambda b: (b, 0, 0))
    full = lambda a: pl.BlockSpec(a.shape, lambda b: (0,) * a.ndim)
    tab = jnp.tile(tab, (1, G, 1))
    kcmp, vcmpt = pl.pallas_call(
        _compress_kernel,
        grid=(B,),
        in_specs=[blk, blk, full(w1), full(pe), full(w2), full(w2rot), full(tab)],
        out_specs=[pl.BlockSpec((None, G * R, Dh), lambda b: (b, 0, 0)),
                   pl.BlockSpec((None, G, Dh, R), lambda b: (b, 0, 0, 0))],
        out_shape=[jax.ShapeDtypeStruct((B, G * R, Dh), BF16), jax.ShapeDtypeStruct((B, G, Dh, R), BF16)],
        compiler_params=_params(("arbitrary",)),
        name="nsa_compress",
    )(kc.reshape(B, G * R, W), vc.reshape(B, G * R, W), w1, pe, w2, w2rot, tab)
    return kcmp.reshape(B, G, R, Dh), vcmpt


def _cmp_kernel(q_ref, kc_ref, vct_ref, ovt_ref, o_ref, sel_ref, *, ns, topk):
    i = pl.program_id(1)
    H, T, Dh = q_ref.shape
    G, R, _ = kc_ref.shape
    Hg = H // G
    NB = sel_ref.shape[1]
    pos = i * T + lax.broadcasted_iota(jnp.int32, (R, T), 1)
    cend = lax.broadcasted_iota(jnp.int32, (R, T), 0) * CMP_STRIDE + (CMP_LEN - 1)
    mask = cend <= pos
    seen = i * T + lax.broadcasted_iota(jnp.int32, (1, T), 1) >= CMP_LEN - 1
    scores = [_dot_nt(kc_ref[hd // Hg], q_ref[hd]) for hd in range(H)]
    blk = lax.broadcasted_iota(jnp.int32, (NB, T), 0)
    cur = (i * T + lax.broadcasted_iota(jnp.int32, (NB, T), 1)) // SLC_LEN
    valid = blk <= cur
    forced = valid & ((blk == 0) | (blk >= cur - 1))
    ovt = ovt_ref[...]

    for g in range(G):
        ps = jnp.zeros((R, T), F32)
        for hd in range(g * Hg, (g + 1) * Hg):
            s = jnp.where(mask, scores[hd], NEG)
            e = jnp.exp2(s - jnp.max(s, axis=0, keepdims=True))
            p = e * jnp.where(seen, 1.0 / jnp.sum(e, axis=0, keepdims=True), 0.0)
            o_ref[hd] = _dot(vct_ref[g], p.astype(BF16))
            ps = ps + p

        p_hi = ps.astype(BF16)
        p_lo = (ps - p_hi.astype(F32)).astype(BF16)
        imp = _dot(ovt, p_hi) + _dot(ovt, p_lo)
        score = jnp.where(forced, FORCE_BONUS, jnp.where(valid, imp, -1.0))
        score = jnp.where(blk < ns, score, -2.0)
        rows = [score[j:j + 1, :] for j in range(ns)]
        ranks = []
        for lo in range(0, NB, SUBLANE):
            sc = score[lo:lo + SUBLANE, :]
            blk_g = lo + lax.broadcasted_iota(jnp.int32, (SUBLANE, T), 0)
            rank = jnp.zeros((SUBLANE, T), F32)
            for j in range(ns):
                if j < lo:
                    ahead = rows[j] >= sc
                elif j >= lo + SUBLANE:
                    ahead = rows[j] > sc
                else:
                    ahead = (rows[j] > sc) | ((rows[j] == sc) & (blk_g > j))
                rank = rank + jnp.where(ahead, 1.0, 0.0)
            ranks.append(rank)
        rank = jnp.concatenate(ranks, axis=0)
        sel_ref[g] = jnp.where((rank < topk) & (score >= 0.0), 1.0, 0.0).astype(sel_ref.dtype)


def _cmp_attn(nq, kcmp, vcmpt, ovt, ns, topk):
    B, H, S, Dh = nq.shape
    G = kcmp.shape[1]
    R = kcmp.shape[2]
    NB = ovt.shape[0]
    T = 2 * ATT_TILE
    return pl.pallas_call(
        functools.partial(_cmp_kernel, ns=ns, topk=topk),
        grid=(B, S // T),
        in_specs=[pl.BlockSpec((None, H, T, Dh), lambda b, i: (b, 0, i, 0)),
                  pl.BlockSpec((None, G, R, Dh), lambda b, i: (b, 0, 0, 0)),
                  pl.BlockSpec((None, G, Dh, R), lambda b, i: (b, 0, 0, 0)),
                  pl.BlockSpec(ovt.shape, lambda b, i: (0, 0))],
        out_specs=[pl.BlockSpec((None, H, Dh, T), lambda b, i: (b, 0, 0, i)),
                   pl.BlockSpec((None, G, NB, T), lambda b, i: (b, 0, 0, i))],
        out_shape=[jax.ShapeDtypeStruct((B, H, Dh, S), F32), jax.ShapeDtypeStruct((B, G, NB, S), BF16)],
        compiler_params=_params(("arbitrary", "arbitrary")),
        name="nsa_cmp",
    )(nq, kcmp, vcmpt, ovt)


def _nsa_kernel(q_ref, ks_ref, vst_ref, kw_ref, vwt_ref, oct_ref, sel_ref, ngt_ref, ext_ref, o_ref,
                bias_sc, m_ref, acc_ref):
    i = pl.program_id(1)
    H, T, Dh = q_ref.shape
    G = ks_ref.shape[0]
    Hg = H // G
    cpt = T // CHUNK
    n = Hg * cpt
    back = WIN // T
    LOWER, UPPER = 2 * G, 2 * G + 1
    @pl.when(i == 0)
    def _():
        bias_sc[LOWER] = _causal_bias(T, upper=False)
        bias_sc[UPPER] = _causal_bias(T, upper=True)

    _online_init(m_ref, acc_ref)

    def tile(g, k_ref, vt_ref, j, slot, window):
        k = k_ref[g, pl.ds(pl.multiple_of(j * T, T), T), :]
        vt = vt_ref[g, j]
        steps = []
        for c in range(n):
            lo = (c % cpt) * CHUNK
            bias = None if slot is None else (lambda lo=lo: bias_sc[slot, :, lo:lo + CHUNK])
            steps.append((2 * n * g + (n if window else 0) + c, g * n + c, k, vt, bias))
        return steps

    def run(steps):
        q_of = lambda c: q_ref[c // cpt, (c % cpt) * CHUNK:(c % cpt + 1) * CHUNK, :]
        _online_steps(q_of, steps, m_ref, acc_ref, NSA_AHEAD)

    def slc_tile(g, which, j, diagonal):
        slot = 2 * g + which
        chosen = _dot(ext_ref[j], sel_ref[g]) > 0.5
        bias_sc[slot] = jnp.where(chosen, _causal_bias(T, upper=False) if diagonal else 0.0, NEG)
        return tile(g, ks_ref, vst_ref, j, slot, False)

    def both(fn):
        steps = []
        for g in range(G):
            steps = steps + fn(g)
        return steps

    def slc_body(jj, carry):
        run(both(lambda g: slc_tile(g, 0, 2 * jj, False) + slc_tile(g, 1, 2 * jj + 1, False)))
        return carry

    lax.fori_loop(0, i // 2, slc_body, 0)

    def last_steps(g, odd, n_back):
        steps = (slc_tile(g, 0, i - 1, False) + slc_tile(g, 1, i, True)) if odd else slc_tile(g, 0, i, True)
        steps = steps + tile(g, kw_ref, vwt_ref, i, LOWER, True)
        for d in range(1, n_back + 1):
            steps = steps + tile(g, kw_ref, vwt_ref, i - d, UPPER if d == back else None, True)
        return steps

    for n_back in range(back):
        @pl.when(i == n_back)
        def _():
            run(both(lambda g: last_steps(g, n_back % 2, n_back)))

    for odd in (0, 1):
        @pl.when((i >= back) & (i % 2 == odd))
        def _():
            run(both(lambda g: last_steps(g, odd, back)))

    outs = []
    for hd in range(H):
        g, c0 = hd // Hg, 2 * n * (hd // Hg) + (hd % Hg) * cpt
        gates = ngt_ref[g]
        r = 3 * (hd % Hg)
        o = (gates[r:r + 1, :] * oct_ref[hd]
             + gates[r + 1:r + 2, :] * _normalized(c0, cpt, acc_ref)
             + gates[r + 2:r + 3, :] * _normalized(c0 + n, cpt, acc_ref))
        outs.append(o)
    for pr in range(H // 2):
        o_ref[:, pr * LANE:(pr + 1) * LANE] = jnp.concatenate(outs[2 * pr:2 * pr + 2], axis=0).T.astype(o_ref.dtype)


def _nsa_main(nq, ks, vst, kw, vwt, o_cmpt, sel, ngt, expand_t):
    B, H, S, Dh = nq.shape
    G = ks.shape[1]
    T = ATT_TILE
    assert WIN % T == 0 and (H // G) % 2 == 0
    rows = H * T
    kspec = pl.BlockSpec((None, G, S, Dh), lambda b, i: (b, 0, 0, 0))
    vspec = pl.BlockSpec((None, G, S // T, Dh, T), lambda b, i: (b, 0, 0, 0, 0))
    NB = sel.shape[2]
    return pl.pallas_call(
        _nsa_kernel,
        grid=(B, S // T),
        in_specs=[pl.BlockSpec((None, H, T, Dh), lambda b, i: (b, 0, i, 0)),
                  kspec, vspec, kspec, vspec,
                  pl.BlockSpec((None, H, Dh, T), lambda b, i: (b, 0, 0, i)),
                  pl.BlockSpec((None, G, NB, T), lambda b, i: (b, 0, 0, i)),
                  pl.BlockSpec((None, G, GATE_ROWS, T), lambda b, i: (b, 0, 0, i)),
                  pl.BlockSpec(expand_t.shape, lambda b, i: (0, 0, 0))],
        out_specs=pl.BlockSpec((None, T, H * Dh), lambda b, i: (b, i, 0)),
        out_shape=jax.ShapeDtypeStruct((B, S, H * Dh), BF16),
        scratch_shapes=[pltpu.VMEM((2 * G + 2, T, T), F32),
                        pltpu.VMEM((2 * rows // CHUNK, 1, CHUNK), F32),
                        pltpu.VMEM((2 * rows // CHUNK, Dh + SUM_ROWS, CHUNK), F32)],
        compiler_params=_params(("arbitrary", "arbitrary")),
        name="nsa_main",
    )(nq, ks, vst, kw, vwt, o_cmpt, sel, ngt, expand_t)


def _rope_table(pos, period, width):
    half = period // 2
    local = np.arange(width) % period
    inv = ROPE_THETA ** (-(local % half).astype(np.float32) / half)
    ang = pos.astype(jnp.float32)[:, None] * jnp.asarray(inv, F32)[None, :]
    return [jnp.cos(ang), jnp.sin(ang)]


def _rope_tables(S):
    pos = jnp.arange(S)
    tabs = _rope_table(pos, DA_QK_DIM, LANE) + _rope_table(pos, HEAD_DIM, LANE)
    cmp_end = jnp.arange(S // CMP_STRIDE) * CMP_STRIDE + CMP_LEN - 1
    return jnp.stack(tabs), jnp.stack(_rope_table(cmp_end, HEAD_DIM, HEAD_DIM))


def _rot_cols(w, period):
    d, n = w.shape
    half = period // 2
    c = w.reshape(d, n // period, period)
    return jnp.concatenate([-c[..., half:], c[..., :half]], axis=-1).reshape(d, n)


def _layout_w_in(w):
    depth, d, _ = w.shape
    cols = lambda off, n: w[:, :, off:off + n]
    kvw = NSA_KV_WIDTH
    w_tok = jnp.concatenate(
        [cols(OFF_GU, GM_WIDTH), cols(OFF_GV, GM_WIDTH), cols(OFF_DQ, 256), cols(OFF_DK, 256), cols(OFF_NQ, NSA_WIDTH),
         cols(OFF_KC, kvw), cols(OFF_VC, kvw), cols(OFF_KS, kvw), cols(OFF_KW, kvw)], axis=2)
    per_group = N_GATES // NSA_KV_GROUPS
    pad = jnp.zeros((depth, d, GATE_ROWS - per_group), w.dtype)
    feat = [cols(OFF_DV, DA_WIDTH), cols(OFF_VS, kvw), cols(OFF_VW, kvw)]
    for g in range(NSA_KV_GROUPS):
        feat += [cols(OFF_NG + g * per_group, per_group), pad]
    return w_tok.astype(BF16), jnp.swapaxes(jnp.concatenate(feat, axis=2), 1, 2).astype(BF16)


def _overlap_table_t(S, nb):
    nc = (S - CMP_LEN) // CMP_STRIDE + 1
    ns = S // SLC_LEN
    cs = np.arange(nc) * CMP_STRIDE
    bs = np.arange(ns) * SLC_LEN
    ov = np.clip(np.minimum(cs[:, None] + CMP_LEN, bs[None, :] + SLC_LEN) - np.maximum(cs[:, None], bs[None, :]), 0, None)
    out = np.zeros((nb, S // CMP_STRIDE), np.float32)
    out[:ns, :nc] = (ov / CMP_STRIDE).T
    return jnp.asarray(out, BF16)


def _expand_table_t(S, T, nb):
    key_block = np.arange(S).reshape(S // T, T, 1) // SLC_LEN
    return jnp.asarray(key_block == np.arange(nb).reshape(1, 1, nb), BF16)


def kernel(x, c, w_ada, b_ada, norm_g, ffn_w_gate, ffn_w_up, ffn_w_down, w_in, w_out,
           gm_ln_g, gm_w_s, gm_b_s, da_lambda, da_sub_g, nsa_cmp_pe, nsa_cmp_w1, nsa_cmp_w2, final_g):
    B, S, D = x.shape
    depth = w_in.shape[0]
    ns = S // SLC_LEN
    topk = min(SLC_TOPK, ns)
    nb = -(-ns // 16) * 16
    assert S % 512 == 0 and ATT_TILE % CHUNK == 0 and CHUNK % LANE == 0 and ffn_w_gate.shape[-1] % FF_CHUNK == 0

    mod_all = _adaln(c, w_ada, b_ada).reshape(depth, B, N_MOD, D)
    tabs, cmp_tab = _rope_tables(S)
    ovt = _overlap_table_t(S, nb)
    expand_t = _expand_table_t(S, ATT_TILE, nb)

    bf = lambda w: w.astype(BF16)
    wg, wu, wd, wo = bf(ffn_w_gate), bf(ffn_w_up), bf(ffn_w_down), bf(w_out)
    w_tok, w_feat = _layout_w_in(w_in)

    for l in range(depth):
        mod = mod_all[l]
        lam_init = 0.8 - 0.6 * math.exp(-0.3 * l)
        x = _ffn(x, mod, norm_g[l, 0], wg, wu, wd, (l, 0), rows=(0, 1, 2))

        (y_a, dq, dk, dvt, nq, kc, vc, ks, vst, kw, vwt, ngt) = _inproj(
            x, mod, norm_g[l, 1], w_tok, w_feat, l, tabs, gm_ln_g[l], gm_w_s[l], gm_b_s[l])
        y_b = _diff_attn(dq, dk, dvt, da_lambda[l], da_sub_g[l], lam_init)
        kcmp, vcmpt = _compress(
            kc, vc, bf(nsa_cmp_w1[l]), nsa_cmp_pe[l].reshape(2, 1, CMP_LEN * HEAD_DIM),
            bf(nsa_cmp_w2[l]), bf(_rot_cols(nsa_cmp_w2[l, 0], HEAD_DIM)), cmp_tab)
        o_cmpt, sel = _cmp_attn(nq, kcmp, vcmpt, ovt, ns, topk)
        y_c = _nsa_main(nq, ks, vst, kw, vwt, o_cmpt, sel, ngt, expand_t)

        x = _ffn(x, mod, norm_g[l, 2], wg, wu, wd, (l, 1), rows=(6, 7, 8),
                 mix=(y_a, y_b, y_c, wo), final_g=final_g if l == depth - 1 else None)
    return x
```
